```python
import jax, jax.numpy as jnp
from jax import lax
import numpy as np

D_MODEL = 1024
BATCH = 4
SEQ = 4096
DEPTH = 1
DEC_BATCH = 16
DEC_SEQ = 4096
PAST_LEN = 128

N_MEM = 256
MIX_WIDTH = D_MODEL
HEAD_DIM = 64
ATTN_WIDTH = MIX_WIDTH // 2
N_Q_HEADS = ATTN_WIDTH // HEAD_DIM
N_KV_HEADS = 2
Q_PER_KV = N_Q_HEADS // N_KV_HEADS
KV_WIDTH = N_KV_HEADS * HEAD_DIM
FOURIER_WIDTH = MIX_WIDTH - ATTN_WIDTH
FOURIER_GROUP = 64
N_FOURIER_GROUPS = FOURIER_WIDTH // FOURIER_GROUP
IN_WIDTH = ATTN_WIDTH + 2 * KV_WIDTH + FOURIER_WIDTH
WINDOW = 128
BLOCK = 128
ROPE_DIM = HEAD_DIM // 4
ROPE_THETA = 500000.0
N_CROSS_HEADS = 4
CROSS_HEAD_DIM = 128
CROSS_WIDTH = N_CROSS_HEADS * CROSS_HEAD_DIM
N_EXPERTS = 32
TOP_K = 4
D_FF = D_MODEL
SWIGLU_LIMIT = 7.0
SWIGLU_ALPHA = 1.702
EPS = 1e-5
NEG_INF = -1e30

kernel_name = "hybrid_swa_fnet_moe_encoder"

F32 = jnp.float32


def rmsnorm(x, g):
    xf = x.astype(F32)
    y = xf * lax.rsqrt(jnp.mean(xf * xf, axis=-1, keepdims=True) + EPS)
    return (y * g.astype(F32)).astype(x.dtype)


def partial_rope(x, pos):
    half = ROPE_DIM // 2
    inv_freq = ROPE_THETA ** (-(jnp.arange(half, dtype=F32) * 2.0) / ROPE_DIM)
    ang = pos.astype(F32)[:, None] * inv_freq[None, :]
    cos = jnp.cos(ang)[None, :, None, :]
    sin = jnp.sin(ang)[None, :, None, :]
    xr = x[..., :ROPE_DIM].astype(F32)
    x1, x2 = xr[..., :half], xr[..., half:]
    rot = jnp.concatenate([x1 * cos - x2 * sin, x2 * cos + x1 * sin], axis=-1).astype(x.dtype)
    return jnp.concatenate([rot, x[..., ROPE_DIM:]], axis=-1)


def windowed_gqa(q, k, v, sinks):
    B, S = q.shape[0], q.shape[1]
    nb = S // BLOCK
    qb = q.reshape(B, nb, BLOCK, N_KV_HEADS, Q_PER_KV, HEAD_DIM).astype(F32)

    def band(t):
        tp = jnp.pad(t, ((0, 0), (BLOCK, BLOCK), (0, 0), (0, 0)))
        tp = tp.reshape(B, nb + 2, BLOCK, N_KV_HEADS, HEAD_DIM)
        return jnp.concatenate([tp[:, :-2], tp[:, 1:-1], tp[:, 2:]], axis=2)

    kb = band(k).astype(F32)
    vb = band(v).astype(F32)
    s = jnp.einsum('bnqkgd,bnckd->bnkgqc', qb, kb) * (HEAD_DIM ** -0.5)
    blk = jnp.arange(nb)[:, None, None]
    qpos = blk * BLOCK + jnp.arange(BLOCK)[None, :, None]
    kpos = (blk - 1) * BLOCK + jnp.arange(3 * BLOCK)[None, None, :]
    valid = (jnp.abs(kpos - qpos) <= WINDOW) & (kpos >= 0) & (kpos < S)
    s = jnp.where(valid[None, :, None, None], s, NEG_INF)
    sink = jnp.broadcast_to(sinks.astype(F32).reshape(1, 1, N_KV_HEADS, Q_PER_KV, 1, 1), s.shape[:-1] + (1,))
    p = jax.nn.softmax(jnp.concatenate([s, sink], axis=-1), axis=-1)[..., :-1]
    o = jnp.einsum('bnkgqc,bnckd->bnqkgd', p, vb)
    return o.reshape(B, S, ATTN_WIDTH).astype(q.dtype)


def fourier_mix(u):
    B, S = u.shape[0], u.shape[1]
    ug = u.reshape(B, S, N_FOURIER_GROUPS, FOURIER_GROUP).astype(F32)
    f = jnp.fft.fftn(ug, axes=(1, 3), norm="ortho").real
    return f.reshape(B, S, FOURIER_WIDTH).astype(u.dtype)


def cross_attention(h, mem_n, w_q, w_kv, w_o):
    B, S = h.shape[0], h.shape[1]
    M = mem_n.shape[1]
    q = (h @ w_q).reshape(B, S, N_CROSS_HEADS, CROSS_HEAD_DIM).astype(F32)
    kv = (mem_n @ w_kv).reshape(B, M, 2, N_CROSS_HEADS, CROSS_HEAD_DIM).astype(F32)
    k, v = kv[:, :, 0], kv[:, :, 1]
    s = jnp.einsum('bshd,bmhd->bhsm', q, k) * (CROSS_HEAD_DIM ** -0.5)
    p = jax.nn.softmax(s, axis=-1)
    o = jnp.einsum('bhsm,bmhd->bshd', p, v).reshape(B, S, CROSS_WIDTH).astype(h.dtype)
    return o @ w_o


def moe(h, w_router, b_router, w_gate_up, b_gate_up, w_down, b_down):
    B, S, D = h.shape
    t = h.reshape(B * S, D)
    logits = (t @ w_router).astype(F32) + b_router.astype(F32)
    top_val, top_idx = lax.top_k(logits, TOP_K)
    gates = jax.nn.softmax(top_val, axis=-1)
    combine = jnp.sum(jax.nn.one_hot(top_idx, N_EXPERTS, dtype=F32) * gates[..., None], axis=1)
    out = jnp.zeros((B * S, D), F32)
    for e in range(N_EXPERTS):
        gu = t @ w_gate_up[e] + b_gate_up[e]
        gate = jnp.minimum(gu[:, :D_FF], SWIGLU_LIMIT)
        up = jnp.clip(gu[:, D_FF:], -SWIGLU_LIMIT, SWIGLU_LIMIT)
        act = (up + 1.0) * (gate * jax.nn.sigmoid(SWIGLU_ALPHA * gate))
        y = act @ w_down[e] + b_down[e]
        out = out + combine[:, e:e + 1] * y.astype(F32)
    return out.reshape(B, S, D).astype(h.dtype)


def encoder_layer(x, mem, g_mix, w_in, sinks, g_attn_out, g_four_out, w_out,
                  g_cross, g_mem, w_q_cross, w_kv_cross, w_o_cross,
                  g_moe, w_router, b_router, w_gate_up, b_gate_up, w_down, b_down):
    B, S, _ = x.shape
    pos = jnp.arange(S)
    h = rmsnorm(x, g_mix)
    z = h @ w_in
    o1 = ATTN_WIDTH
    o2 = o1 + KV_WIDTH
    o3 = o2 + KV_WIDTH
    q = partial_rope(z[..., :o1].reshape(B, S, N_Q_HEADS, HEAD_DIM), pos)
    k = partial_rope(z[..., o1:o2].reshape(B, S, N_KV_HEADS, HEAD_DIM), pos)
    v = z[..., o2:o3].reshape(B, S, N_KV_HEADS, HEAD_DIM)
    u = z[..., o3:]
    a = windowed_gqa(q, k, v, sinks)
    f = fourier_mix(u)
    mixed = jnp.concatenate([rmsnorm(a, g_attn_out), rmsnorm(f, g_four_out)], axis=-1) @ w_out
    x = x + mixed
    x = x + cross_attention(rmsnorm(x, g_cross), rmsnorm(mem, g_mem), w_q_cross, w_kv_cross, w_o_cross)
    x = x + moe(rmsnorm(x, g_moe), w_router, b_router, w_gate_up, b_gate_up, w_down, b_down)
    return x


def trunk(x, mem, g_mix, w_in, sinks, g_attn_out, g_four_out, w_out,
          g_cross, g_mem, w_q_cross, w_kv_cross, w_o_cross,
          g_moe, w_router, b_router, w_gate_up, b_gate_up, w_down, b_down, g_final):
    for l in range(DEPTH):
        x = encoder_layer(x, mem, g_mix[l], w_in[l], sinks[l], g_attn_out[l], g_four_out[l], w_out[l],
                          g_cross[l], g_mem[l], w_q_cross[l], w_kv_cross[l], w_o_cross[l],
                          g_moe[l], w_router[l], b_router[l], w_gate_up[l], b_gate_up[l],
                          w_down[l], b_down[l])
    return rmsnorm(x, g_final)


def setup_inputs(seed: int = 0) -> dict:
    key = jax.random.key(seed)
    ks = jax.random.split(key, 24)
    nrm = lambda k, shape, scale: jax.random.normal(k, shape, F32) * scale
    gain = lambda k, shape: 1.0 + 0.02 * jax.random.normal(k, shape, F32)
    L = DEPTH
    return {
        "x_prompt": nrm(ks[0], (BATCH, SEQ, D_MODEL), 1.0),
        "x_sample": nrm(ks[1], (DEC_BATCH, DEC_SEQ, D_MODEL), 1.0),
        "mem_prompt": nrm(ks[2], (BATCH, N_MEM, D_MODEL), 1.0),
        "mem_sample": nrm(ks[3], (DEC_BATCH, N_MEM, D_MODEL), 1.0),
        "g_mix": gain(ks[4], (L, D_MODEL)),
        "w_in": nrm(ks[5], (L, D_MODEL, IN_WIDTH), D_MODEL ** -0.5),
        "sinks": nrm(ks[6], (L, N_Q_HEADS), 0.5),
        "g_attn_out": gain(ks[7], (L, ATTN_WIDTH)),
        "g_four_out": gain(ks[8], (L, FOURIER_WIDTH)),
        "w_out": nrm(ks[9], (L, MIX_WIDTH, D_MODEL), MIX_WIDTH ** -0.5),
        "g_cross": gain(ks[10], (L, D_MODEL)),
        "g_mem": gain(ks[11], (L, D_MODEL)),
        "w_q_cross": nrm(ks[12], (L, D_MODEL, CROSS_WIDTH), D_MODEL ** -0.5),
        "w_kv_cross": nrm(ks[13], (L, D_MODEL, 2 * CROSS_WIDTH), D_MODEL ** -0.5),
        "w_o_cross": nrm(ks[14], (L, CROSS_WIDTH, D_MODEL), CROSS_WIDTH ** -0.5),
        "g_moe": gain(ks[15], (L, D_MODEL)),
        "w_router": nrm(ks[16], (L, D_MODEL, N_EXPERTS), D_MODEL ** -0.5),
        "b_router": nrm(ks[17], (L, N_EXPERTS), 0.01),
        "w_gate_up": nrm(ks[18], (L, N_EXPERTS, D_MODEL, 2 * D_FF), D_MODEL ** -0.5),
        "b_gate_up": nrm(ks[19], (L, N_EXPERTS, 2 * D_FF), 0.01),
        "w_down": nrm(ks[20], (L, N_EXPERTS, D_FF, D_MODEL), D_FF ** -0.5),
        "b_down": nrm(ks[21], (L, N_EXPERTS, D_MODEL), 0.01),
        "g_final": gain(ks[22], (D_MODEL,)),
    }


def reference(x_prompt, x_sample, mem_prompt, mem_sample, g_mix, w_in, sinks, g_attn_out, g_four_out,
              w_out, g_cross, g_mem, w_q_cross, w_kv_cross, w_o_cross, g_moe, w_router, b_router,
              w_gate_up, b_gate_up, w_down, b_down, g_final):
    y_prompt = trunk(x_prompt, mem_prompt, g_mix, w_in, sinks, g_attn_out, g_four_out, w_out,
                     g_cross, g_mem, w_q_cross, w_kv_cross, w_o_cross,
                     g_moe, w_router, b_router, w_gate_up, b_gate_up, w_down, b_down, g_final)
    y_sample = trunk(x_sample, mem_sample, g_mix, w_in, sinks, g_attn_out, g_four_out, w_out,
                     g_cross, g_mem, w_q_cross, w_kv_cross, w_o_cross,
                     g_moe, w_router, b_router, w_gate_up, b_gate_up, w_down, b_down, g_final)
    return (y_prompt, y_sample)
```

```python
import functools

import numpy as np
import jax
import jax.numpy as jnp
from jax import lax
from jax.experimental import pallas as pl
from jax.experimental.pallas import tpu as pltpu

F32 = jnp.float32
BF16 = jnp.bfloat16
I32 = jnp.int32

HEAD_DIM = 64
N_Q_HEADS = 8
N_KV_HEADS = 2
ATTN_WIDTH = N_Q_HEADS * HEAD_DIM
KV_WIDTH = N_KV_HEADS * HEAD_DIM
FOURIER_WIDTH = 512
FOURIER_GROUP = 64
WINDOW = 128
ROPE_DIM = 16
ROPE_THETA = 500000.0
N_CROSS_HEADS = 4
CROSS_HEAD_DIM = 128
CROSS_WIDTH = N_CROSS_HEADS * CROSS_HEAD_DIM
N_EXPERTS = 32
TOP_K = 4
SWIGLU_LIMIT = 7.0
SWIGLU_ALPHA = 1.702
EPS = 1e-5
NEG_INF = -1e30

LANES = 128
SUBLANES = 8
VMEM_LIMIT_BYTES = 56 * 1024 * 1024

TS_INPROJ = 512
TQ_ATTN = 512
TR_DFT = 512
TM_MID = 512
TM_ROWS = 256
TM_EXPERT = 512


def _cparams(sem):
    return pltpu.CompilerParams(dimension_semantics=sem, vmem_limit_bytes=VMEM_LIMIT_BYTES)


def _rms(x, g):
    return x * lax.rsqrt(jnp.mean(x * x, axis=-1, keepdims=True) + EPS) * g


def _two_group_maps(n_prompt):
    def prompt_b(b):
        return jnp.minimum(b, n_prompt - 1)

    def sample_b(b):
        return jnp.maximum(b - n_prompt, 0)
    return prompt_b, sample_b


def _memkv_kernel(mp_ref, ms_ref, g_ref, w_ref, o_ref, *, n_prompt):
    b = pl.program_id(0)
    m = jnp.where(b < n_prompt, mp_ref[0], ms_ref[0])
    mn = _rms(m, g_ref[...]).astype(BF16)
    o_ref[0] = jnp.dot(mn, w_ref[...], preferred_element_type=F32).astype(BF16)


def _memkv(mem_p, mem_s, g_mem, w_kv):
    n_prompt, n_mem, d = mem_p.shape
    nb = n_prompt + mem_s.shape[0]
    pb, sb = _two_group_maps(n_prompt)
    return pl.pallas_call(
        functools.partial(_memkv_kernel, n_prompt=n_prompt),
        out_shape=jax.ShapeDtypeStruct((nb, n_mem, w_kv.shape[1]), BF16),
        grid=(nb,),
        in_specs=[
            pl.BlockSpec((1, n_mem, d), lambda b: (pb(b), 0, 0)),
            pl.BlockSpec((1, n_mem, d), lambda b: (sb(b), 0, 0)),
            pl.BlockSpec((1, d), lambda b: (0, 0)),
            pl.BlockSpec(w_kv.shape, lambda b: (0, 0)),
        ],
        out_specs=pl.BlockSpec((1, n_mem, w_kv.shape[1]), lambda b: (b, 0, 0)),
        compiler_params=_cparams(("arbitrary",)),
        name="memkv",
    )(mem_p, mem_s, g_mem.reshape(1, d), w_kv.astype(BF16))


_ROT_W = ATTN_WIDTH + 2 * KV_WIDTH


def _inproj_kernel(xp_ref, xs_ref, g_ref, w_ref, cos_ref, sin_ref, bdc_ref, bds_ref,
                   q_ref, kv_ref, ab_ref, *, n_prompt):
    b = pl.program_id(1)
    x = jnp.where(b < n_prompt, xp_ref[0], xs_ref[0])
    h = _rms(x, g_ref[...]).astype(BF16)
    z = jnp.dot(h, w_ref[...], preferred_element_type=F32)
    cos = cos_ref[...]
    sin = sin_ref[...]
    lane = lax.broadcasted_iota(I32, cos.shape, 1) & (HEAD_DIM - 1)
    first_half = lane < ROPE_DIM // 2
    rot = []
    for c in range(_ROT_W // LANES):
        zc = z[:, c * LANES:(c + 1) * LANES]
        partner = jnp.where(first_half,
                            pltpu.roll(zc, LANES - ROPE_DIM // 2, axis=1),
                            pltpu.roll(zc, ROPE_DIM // 2, axis=1))
        rot.append(zc * cos + partner * sin)
    nq = ATTN_WIDTH // LANES
    q_ref[0] = (jnp.concatenate(rot[:nq], axis=1) * (HEAD_DIM ** -0.5)).astype(BF16)
    v_off = _ROT_W
    u_off = _ROT_W + 2 * KV_WIDTH
    kv_ref[0] = jnp.concatenate(rot[nq:] + [z[:, v_off:u_off]], axis=1).astype(BF16)
    u = z[:, u_off:].astype(BF16)
    ab_ref[0] = jnp.concatenate(
        [jnp.dot(u, bdc_ref[...], preferred_element_type=F32),
         jnp.dot(u, bds_ref[...], preferred_element_type=F32)], axis=1).astype(BF16)


def _rope_tables(seq):
    half = ROPE_DIM // 2
    inv_freq = ROPE_THETA ** (-(jnp.arange(half, dtype=F32) * 2.0) / ROPE_DIM)
    ang = jnp.arange(seq).astype(F32)[:, None] * inv_freq[None, :]
    cos, sin = jnp.cos(ang), jnp.sin(ang)
    rest = HEAD_DIM - ROPE_DIM
    cos_h = jnp.concatenate([cos, cos, jnp.ones((seq, rest), F32)], axis=1)
    sin_h = jnp.concatenate([-sin, sin, jnp.zeros((seq, rest), F32)], axis=1)
    reps = LANES // HEAD_DIM
    return jnp.tile(cos_h, (1, reps)), jnp.tile(sin_h, (1, reps))


def _channel_dft_tables():
    n = FOURIER_GROUP
    idx = np.arange(n)
    ang = 2.0 * np.pi * ((idx[:, None] * idx[None, :]) % n) / n
    eye = np.eye(FOURIER_WIDTH // n)
    bdc = np.kron(eye, np.cos(ang))
    bds = np.kron(eye, np.sin(ang))
    return jnp.asarray(bdc, BF16), jnp.asarray(bds, BF16)


def _inproj(x_p, x_s, g_mix, w_in):
    n_prompt, seq, d = x_p.shape
    nb = n_prompt + x_s.shape[0]
    ts = min(TS_INPROJ, seq)
    o1, o2, o3 = ATTN_WIDTH, ATTN_WIDTH + KV_WIDTH, ATTN_WIDTH + 2 * KV_WIDTH
    swap = lambda w: jnp.concatenate([w[:, HEAD_DIM:], w[:, :HEAD_DIM]], axis=1)
    wk, wv = w_in[:, o1:o2], w_in[:, o2:o3]
    w_ext = jnp.concatenate([w_in[:, :o1], wk, swap(wk), wv, swap(wv), w_in[:, o3:]], axis=1).astype(BF16)
    cos_t, sin_t = _rope_tables(seq)
    bdc, bds = _channel_dft_tables()
    pb, sb = _two_group_maps(n_prompt)
    kvw = 4 * KV_WIDTH
    return pl.pallas_call(
        functools.partial(_inproj_kernel, n_prompt=n_prompt),
        out_shape=(jax.ShapeDtypeStruct((nb, seq, ATTN_WIDTH), BF16),
                   jax.ShapeDtypeStruct((nb, seq, kvw), BF16),
                   jax.ShapeDtypeStruct((nb, seq, 2 * FOURIER_WIDTH), BF16)),
        grid=(seq // ts, nb),
        in_specs=[
            pl.BlockSpec((1, ts, d), lambda i, b: (pb(b), i, 0)),
            pl.BlockSpec((1, ts, d), lambda i, b: (sb(b), i, 0)),
            pl.BlockSpec((1, d), lambda i, b: (0, 0)),
            pl.BlockSpec(w_ext.shape, lambda i, b: (0, 0)),
            pl.BlockSpec((ts, LANES), lambda i, b: (i, 0)),
            pl.BlockSpec((ts, LANES), lambda i, b: (i, 0)),
            pl.BlockSpec(bdc.shape, lambda i, b: (0, 0)),
            pl.BlockSpec(bds.shape, lambda i, b: (0, 0)),
        ],
        out_specs=(pl.BlockSpec((1, ts, ATTN_WIDTH), lambda i, b: (b, i, 0)),
                   pl.BlockSpec((1, ts, kvw), lambda i, b: (b, i, 0)),
                   pl.BlockSpec((1, ts, 2 * FOURIER_WIDTH), lambda i, b: (b, i, 0))),
        compiler_params=_cparams(("arbitrary", "arbitrary")),
        name="inproj",
    )(x_p, x_s, g_mix.reshape(1, d), w_ext, cos_t, sin_t, bdc, bds)


def _swa_kernel(sink_ref, q_ref, kvm_ref, kvp_ref, kvn_ref, bias_ref, o_ref, *, nblk):
    i = pl.program_id(1)
    last = pl.num_programs(1) - 1
    kvw = jnp.concatenate([kvp_ref[0], kvm_ref[0], kvn_ref[0]], axis=0)
    k, ksw, v, vsw = [kvw[:, c * LANES:(c + 1) * LANES] for c in range(4)]
    lo = lax.broadcasted_iota(I32, k.shape, 1) < HEAD_DIM
    zero = jnp.zeros_like(k)
    k_lo = (jnp.where(lo, k, zero), jnp.where(lo, ksw, zero))
    k_hi = (jnp.where(lo, zero, ksw), jnp.where(lo, zero, k))
    v_lo = (jnp.where(lo, v, zero), jnp.where(lo, vsw, zero))
    v_hi = (jnp.where(lo, zero, vsw), jnp.where(lo, zero, v))
    row = lax.broadcasted_iota(I32, (2 * WINDOW, 1), 0)
    for j in range(nblk):
        bias = bias_ref[1]
        if j == 0:
            bias = jnp.where(i == 0, bias_ref[0], bias)
        if j == nblk - 1:
            bias = jnp.where(i == last, bias_ref[2], bias)
        bias2 = jnp.concatenate([bias, bias], axis=0)
        rows = slice(j * WINDOW, (j + 1) * WINDOW)
        win = slice(j * WINDOW, (j + 3) * WINDOW)
        for kvh in range(N_KV_HEADS):
            c0 = 2 * kvh
            q2 = jnp.concatenate([q_ref[0, rows, c0 * LANES:(c0 + 1) * LANES],
                                  q_ref[0, rows, (c0 + 1) * LANES:(c0 + 2) * LANES]], axis=0)
            acc = None
            for half, (kk, vv) in enumerate(((k_lo[kvh], v_lo[kvh]), (k_hi[kvh], v_hi[kvh]))):
                s = lax.dot_general(q2, kk[win], (((1,), (1,)), ((), ())),
                                    preferred_element_type=F32) + bias2
                sink = jnp.where(row < WINDOW, sink_ref[4 * kvh + half], sink_ref[4 * kvh + 2 + half])
                m = jnp.maximum(jnp.max(s, axis=-1, keepdims=True), sink)
                e = jnp.exp(s - m)
                den = jnp.sum(e, axis=-1, keepdims=True) + jnp.exp(sink - m)
                pv = jnp.dot(e.astype(BF16), vv[win], preferred_element_type=F32)
                contrib = pv * (1.0 / den)
                acc = contrib if acc is None else acc + contrib
            o_ref[0, rows, c0 * LANES:(c0 + 1) * LANES] = acc[:WINDOW].astype(o_ref.dtype)
            o_ref[0, rows, (c0 + 1) * LANES:(c0 + 2) * LANES] = acc[WINDOW:].astype(o_ref.dtype)


def _swa_bias():
    qi = np.arange(WINDOW)[:, None]
    c = np.arange(3 * WINDOW)[None, :]
    band = np.abs(c - WINDOW - qi) <= WINDOW
    first = band & (c >= WINDOW)
    lastb = band & (c < 2 * WINDOW)
    tab = np.stack([first, band, lastb]).astype(np.float32)
    return jnp.asarray((1.0 - tab) * NEG_INF, F32)


def _swa(q, kv, sinks):
    nb, seq, _ = q.shape
    tq = min(TQ_ATTN, seq)
    nblk = tq // WINDOW
    nseq_blk = seq // WINDOW
    kvw = kv.shape[-1]
    assert nblk >= 2 and seq % tq == 0, "first/last window masks are applied to distinct query blocks"
    bias = _swa_bias()
    return pl.pallas_call(
        functools.partial(_swa_kernel, nblk=nblk),
        out_shape=jax.ShapeDtypeStruct((nb, seq, ATTN_WIDTH), BF16),
        grid=(nb, seq // tq),
        in_specs=[
            pl.BlockSpec(memory_space=pltpu.SMEM),
            pl.BlockSpec((1, tq, ATTN_WIDTH), lambda b, i: (b, i, 0)),
            pl.BlockSpec((1, tq, kvw), lambda b, i: (b, i, 0)),
            pl.BlockSpec((1, WINDOW, kvw), lambda b, i: (b, jnp.maximum(i * nblk - 1, 0), 0)),
            pl.BlockSpec((1, WINDOW, kvw), lambda b, i: (b, jnp.minimum((i + 1) * nblk, nseq_blk - 1), 0)),
            pl.BlockSpec(bias.shape, lambda b, i: (0, 0, 0)),
        ],
        out_specs=pl.BlockSpec((1, tq, ATTN_WIDTH), lambda b, i: (b, i, 0)),
        compiler_params=_cparams(("arbitrary", "arbitrary")),
        name="swa",
    )(sinks.astype(F32), q, kv, kv, kv, bias)


def _seqdft_kernel(cs_ref, ss_ref, ab_ref, o_ref, *, scale):
    a = ab_ref[0, :, :FOURIER_WIDTH]
    bm = ab_ref[0, :, FOURIER_WIDTH:]
    r = (jnp.dot(cs_ref[...], a, preferred_element_type=F32)
         - jnp.dot(ss_ref[...], bm, preferred_element_type=F32))
    o_ref[0] = (r * scale).astype(o_ref.dtype)


def _seq_dft_tables(seq):
    n1 = 64 if seq % 64 == 0 and seq > 64 else 1
    n2 = seq // n1
    k = np.arange(seq)[:, None]
    a1 = 2.0 * np.pi * ((k * n2 * np.arange(n1)[None, :]) % seq) / seq
    a2 = 2.0 * np.pi * ((k * np.arange(n2)[None, :]) % seq) / seq
    c1, s1 = jnp.asarray(np.cos(a1), F32), jnp.asarray(np.sin(a1), F32)
    c2, s2 = jnp.asarray(np.cos(a2), F32), jnp.asarray(np.sin(a2), F32)
    cs = c1[:, :, None] * c2[:, None, :] - s1[:, :, None] * s2[:, None, :]
    ss = s1[:, :, None] * c2[:, None, :] + c1[:, :, None] * s2[:, None, :]
    return cs.reshape(seq, seq).astype(BF16), ss.reshape(seq, seq).astype(BF16)


def _seqdft(ab):
    nb, seq, _ = ab.shape
    tr = min(TR_DFT, seq)
    cs, ss = _seq_dft_tables(seq)
    scale = float(1.0 / np.sqrt(seq * FOURIER_GROUP))
    return pl.pallas_call(
        functools.partial(_seqdft_kernel, scale=scale),
        out_shape=jax.ShapeDtypeStruct((nb, seq, FOURIER_WIDTH), BF16),
        grid=(nb, seq // tr),
        in_specs=[
            pl.BlockSpec((tr, seq), lambda b, i: (i, 0)),
            pl.BlockSpec((tr, seq), lambda b, i: (i, 0)),
            pl.BlockSpec((1, seq, 2 * FOURIER_WIDTH), lambda b, i: (b, 0, 0)),
        ],
        out_specs=pl.BlockSpec((1, tr, FOURIER_WIDTH), lambda b, i: (b, i, 0)),
        compiler_params=_cparams(("arbitrary", "arbitrary")),
        name="seqdft",
    )(cs, ss, ab)


def _mid_kernel(xp_ref, xs_ref, a_ref, f_ref, kv_ref, gao_ref, gfo_ref, wout_ref, gc_ref, wq_ref,
                wo_ref, gm_ref, wrh_ref, wrl_ref, br_ref, tri_ref,
                x2_ref, h3_ref, meta_ref, gate_ref, cnt_ref, cnt_sc, *, n_prompt):
    b = pl.program_id(0)
    i = pl.program_id(1)
    tm = a_ref.shape[1]

    @pl.when((b == 0) & (i == 0))
    def _():
        cnt_sc[...] = jnp.zeros_like(cnt_sc)

    x = jnp.where(b < n_prompt, xp_ref[0], xs_ref[0])
    an = _rms(a_ref[0].astype(F32), gao_ref[...]).astype(BF16)
    fn = _rms(f_ref[0].astype(F32), gfo_ref[...]).astype(BF16)
    x1 = (x + jnp.dot(an, wout_ref[:ATTN_WIDTH, :], preferred_element_type=F32)
          + jnp.dot(fn, wout_ref[ATTN_WIDTH:, :], preferred_element_type=F32))

    h2 = _rms(x1, gc_ref[...]).astype(BF16)
    qc = (jnp.dot(h2, wq_ref[...], preferred_element_type=F32) * (CROSS_HEAD_DIM ** -0.5)).astype(BF16)
    heads = []
    for hd in range(N_CROSS_HEADS):
        cols = slice(hd * CROSS_HEAD_DIM, (hd + 1) * CROSS_HEAD_DIM)
        vcols = slice(CROSS_WIDTH + hd * CROSS_HEAD_DIM, CROSS_WIDTH + (hd + 1) * CROSS_HEAD_DIM)
        s = lax.dot_general(qc[:, cols], kv_ref[0, :, cols], (((1,), (1,)), ((), ())),
                            preferred_element_type=F32)
        e = jnp.exp(s - jnp.max(s, axis=-1, keepdims=True))
        den = jnp.sum(e, axis=-1, keepdims=True)
        heads.append(jnp.dot(e.astype(BF16), kv_ref[0, :, vcols], preferred_element_type=F32) * (1.0 / den))
    oc = jnp.concatenate(heads, axis=1).astype(BF16)
    x2 = x1 + jnp.dot(oc, wo_ref[...], preferred_element_type=F32)
    x2_ref[0] = x2

    h3 = _rms(x2, gm_ref[...])
    for sl in range(SUBLANES):
        h3_ref[pl.ds(sl, tm, stride=SUBLANES), :] = h3[:, sl * LANES:(sl + 1) * LANES]

    hi = h3.astype(BF16)
    lo = (h3 - hi.astype(F32)).astype(BF16)
    logits = (jnp.dot(hi, wrh_ref[...], preferred_element_type=F32)
              + jnp.dot(lo, wrh_ref[...], preferred_element_type=F32)
              + jnp.dot(hi, wrl_ref[...], preferred_element_type=F32) + br_ref[...])
    lt = logits.T[:N_EXPERTS, :]
    eidx = lax.broadcasted_iota(I32, lt.shape, 0)
    vals, idxs, sels = [], [], []
    for _ in range(TOP_K):
        m = jnp.max(lt, axis=0, keepdims=True)
        idx = jnp.min(jnp.where(lt == m, eidx, N_EXPERTS), axis=0, keepdims=True)
        sel = eidx == idx
        lt = jnp.where(sel, -jnp.inf, lt)
        vals.append(m)
        idxs.append(idx)
        sels.append(sel)
    ex = [jnp.exp(vk - vals[0]) for vk in vals]
    inv = 1.0 / (ex[0] + ex[1] + ex[2] + ex[3])
    zrow = jnp.zeros_like(ex[0])
    gate_ref[...] = jnp.concatenate([e_ * inv for e_ in ex] + [zrow] * (SUBLANES - TOP_K), axis=0)

    onehot = jnp.zeros(lt.shape, F32)
    for sel in sels:
        onehot = onehot + sel.astype(F32)
    before = cnt_sc[:, 0:1] + jnp.dot(onehot.astype(BF16), tri_ref[...], preferred_element_type=F32)
    ranks = [jnp.sum(jnp.where(sel, before, 0.0), axis=0, keepdims=True).astype(I32) for sel in sels]
    meta_ref[...] = jnp.concatenate(idxs + ranks, axis=0)
    cnt_sc[...] = cnt_sc[...] + jnp.sum(onehot, axis=1, keepdims=True)
    cnt_ref[...] = cnt_sc[...]


def _mid(x_p, x_s, a, f, kvc, g_attn_out, g_four_out, w_out, g_cross, w_q, w_o, g_moe, w_router, b_router):
    n_prompt, seq, d = x_p.shape
    nb = a.shape[0]
    tm = min(TM_MID, seq)
    nt = seq // tm
    n_mem = kvc.shape[1]
    pb, sb = _two_group_maps(n_prompt)
    wr = jnp.zeros((d, LANES), F32).at[:, :N_EXPERTS].set(w_router)
    wr_hi = wr.astype(BF16)
    wr_lo = (wr - wr_hi.astype(F32)).astype(BF16)
    br = jnp.zeros((1, LANES), F32).at[0, :N_EXPERTS].set(b_router)
    tri = jnp.asarray(np.triu(np.ones((tm, tm), np.float32), 1), BF16)
    full = lambda arr: pl.BlockSpec(arr.shape, lambda b, i: (0,) * arr.ndim)
    row = lambda v: v.reshape(1, -1)
    args = [x_p, x_s, a, f, kvc, row(g_attn_out), row(g_four_out), w_out.astype(BF16), row(g_cross),
            w_q.astype(BF16), w_o.astype(BF16), row(g_moe), wr_hi, wr_lo, br, tri]
    in_specs = [
        pl.BlockSpec((1, tm, d), lambda b, i: (pb(b), i, 0)),
        pl.BlockSpec((1, tm, d), lambda b, i: (sb(b), i, 0)),
        pl.BlockSpec((1, tm, ATTN_WIDTH), lambda b, i: (b, i, 0)),
        pl.BlockSpec((1, tm, FOURIER_WIDTH), lambda b, i: (b, i, 0)),
        pl.BlockSpec((1, n_mem, 2 * CROSS_WIDTH), lambda b, i: (b, 0, 0)),
    ] + [full(v) for v in args[5:]]
    tok = nb * seq
    return pl.pallas_call(
        functools.partial(_mid_kernel, n_prompt=n_prompt),
        out_shape=(jax.ShapeDtypeStruct((nb, seq, d), F32),
                   jax.ShapeDtypeStruct((tok * SUBLANES, LANES), F32),
                   jax.ShapeDtypeStruct((SUBLANES, tok), I32),
                   jax.ShapeDtypeStruct((SUBLANES, tok), F32),
                   jax.ShapeDtypeStruct((N_EXPERTS, LANES), F32)),
        grid=(nb, nt),
        in_specs=in_specs,
        out_specs=(pl.BlockSpec((1, tm, d), lambda b, i: (b, i, 0)),
                   pl.BlockSpec((tm * SUBLANES, LANES), lambda b, i: (b * nt + i, 0)),
                   pl.BlockSpec((SUBLANES, tm), lambda b, i: (0, b * nt + i)),
                   pl.BlockSpec((SUBLANES, tm), lambda b, i: (0, b * nt + i)),
                   pl.BlockSpec((N_EXPERTS, LANES), lambda b, i: (0, 0))),
        scratch_shapes=[pltpu.VMEM((N_EXPERTS, LANES), F32)],
        compiler_params=_cparams(("arbitrary", "arbitrary")),
        name="mid",
    )(*args)


def _row_copy(src_rows, dst_rows, sem):
    return pltpu.make_async_copy(src_rows, dst_rows, sem)


def _dispatch_kernel(pos_ref, ztile_ref, h3_ref, xs_ref, zbuf, sem, *, tm, te):
    i = pl.program_id(0)

    @pl.when(i == 0)
    def _():
        zbuf[...] = jnp.zeros_like(zbuf)
        for e in range(N_EXPERTS):
            r0 = pl.multiple_of(ztile_ref[e] * (te * SUBLANES), te * SUBLANES)
            _row_copy(zbuf, xs_ref.at[pl.ds(r0, te * SUBLANES), :], sem).start()
        for e in range(N_EXPERTS):
            _row_copy(zbuf, xs_ref.at[pl.ds(0, te * SUBLANES), :], sem).wait()

    def issue(t, carry):
        src = h3_ref.at[pl.ds(pl.multiple_of(t * SUBLANES, SUBLANES), SUBLANES), :]
        for k in range(TOP_K):
            r0 = pl.multiple_of(pos_ref[0, 0, k * tm + t] * SUBLANES, SUBLANES)
            _row_copy(src, xs_ref.at[pl.ds(r0, SUBLANES), :], sem).start()
        return carry

    lax.fori_loop(0, tm, issue, 0)

    def drain(t, carry):
        for k in range(TOP_K):
            _row_copy(h3_ref.at[pl.ds(0, SUBLANES), :], xs_ref.at[pl.ds(0, SUBLANES), :], sem).wait()
        return carry

    lax.fori_loop(0, tm, drain, 0)


def _dispatch(h3, pos3, ztile, rows_pad, te):
    ntile, _, four_tm = pos3.shape
    tm = four_tm // TOP_K
    return pl.pallas_call(
        functools.partial(_dispatch_kernel, tm=tm, te=te),
        out_shape=jax.ShapeDtypeStruct((rows_pad * SUBLANES, LANES), F32),
        grid_spec=pltpu.PrefetchScalarGridSpec(
            num_scalar_prefetch=0,
            grid=(ntile,),
            in_specs=[
                pl.BlockSpec((1, 1, four_tm), lambda i: (i, 0, 0), memory_space=pltpu.SMEM),
                pl.BlockSpec(memory_space=pltpu.SMEM),
                pl.BlockSpec((tm * SUBLANES, LANES), lambda i: (i, 0)),
            ],
            out_specs=pl.BlockSpec(memory_space=pl.ANY),
            scratch_shapes=[pltpu.VMEM((te * SUBLANES, LANES), F32), pltpu.SemaphoreType.DMA],
        ),
        compiler_params=_cparams(("arbitrary",)),
        name="dispatch",
    )(pos3, ztile, h3)


def _expert_kernel(te_ref, tb_ref, nu_ref, xs_ref, wgu_ref, bgu_ref, wd_ref, bd_ref, ys_ref):
    i = pl.program_id(0)
    te = xs_ref.shape[0] // SUBLANES
    dff = wd_ref.shape[1]

    @pl.when(i < nu_ref[0])
    def _():
        x = jnp.concatenate([xs_ref[pl.ds(sl, te, stride=SUBLANES), :] for sl in range(SUBLANES)],
                            axis=1).astype(BF16)
        gu = jnp.dot(x, wgu_ref[0], preferred_element_type=F32) + bgu_ref[0]
        gate = jnp.minimum(gu[:, :dff], SWIGLU_LIMIT)
        up = jnp.clip(gu[:, dff:], -SWIGLU_LIMIT, SWIGLU_LIMIT)
        act = (up + 1.0) * (gate * (1.0 / (1.0 + jnp.exp(-SWIGLU_ALPHA * gate))))
        y = jnp.dot(act.astype(BF16), wd_ref[0], preferred_element_type=F32) + bd_ref[0]
        for sl in range(SUBLANES):
            ys_ref[pl.ds(sl, te, stride=SUBLANES), :] = y[:, sl * LANES:(sl + 1) * LANES]


def _experts(xs, tile_expert, tile_block, n_used, w_gate_up, b_gate_up, w_down, b_down, te):
    ne, d, two_dff = w_gate_up.shape
    dff = two_dff // 2
    ntile = tile_expert.shape[0]
    return pl.pallas_call(
        _expert_kernel,
        out_shape=jax.ShapeDtypeStruct(xs.shape, F32),
        grid_spec=pltpu.PrefetchScalarGridSpec(
            num_scalar_prefetch=3,
            grid=(ntile,),
            in_specs=[
                pl.BlockSpec((te * SUBLANES, LANES), lambda i, te_r, tb_r, nu_r: (tb_r[i], 0)),
                pl.BlockSpec((1, d, two_dff), lambda i, te_r, tb_r, nu_r: (te_r[i], 0, 0)),
                pl.BlockSpec((1, 1, two_dff), lambda i, te_r, tb_r, nu_r: (te_r[i], 0, 0)),
                pl.BlockSpec((1, dff, d), lambda i, te_r, tb_r, nu_r: (te_r[i], 0, 0)),
                pl.BlockSpec((1, 1, d), lambda i, te_r, tb_r, nu_r: (te_r[i], 0, 0)),
            ],
            out_specs=pl.BlockSpec((te * SUBLANES, LANES), lambda i, te_r, tb_r, nu_r: (tb_r[i], 0)),
        ),
        compiler_params=_cparams(("arbitrary",)),
        name="experts",
    )(tile_expert, tile_block, n_used, xs, w_gate_up.astype(BF16), b_gate_up.reshape(ne, 1, two_dff),
      w_down.astype(BF16), b_down.reshape(ne, 1, d))


def _combine_kernel(pos_ref, x2_ref, gate_ref, g_ref, ys_ref, o_ref, buf, sem, *, tm):
    def issue(t, carry):
        for k in range(TOP_K):
            r0 = pl.multiple_of(pos_ref[0, 0, k * tm + t] * SUBLANES, SUBLANES)
            d0 = pl.multiple_of((k * tm + t) * SUBLANES, SUBLANES)
            _row_copy(ys_ref.at[pl.ds(r0, SUBLANES), :], buf.at[pl.ds(d0, SUBLANES), :], sem).start()
        return carry

    lax.fori_loop(0, tm, issue, 0)

    def drain(t, carry):
        for k in range(TOP_K):
            _row_copy(ys_ref.at[pl.ds(0, SUBLANES), :], buf.at[pl.ds(0, SUBLANES), :], sem).wait()
        return carry

    lax.fori_loop(0, tm, drain, 0)

    x = x2_ref[0]
    g = gate_ref[...]
    for k in range(TOP_K):
        yk = jnp.concatenate(
            [buf[pl.ds(k * tm * SUBLANES + sl, tm, stride=SUBLANES), :] for sl in range(SUBLANES)], axis=1)
        x = x + g[:, k:k + 1] * yk
    o_ref[0] = _rms(x, g_ref[...])


def _combine(x2, gate_t, pos3, ys, g_final, b0, nbatch):
    _, seq, d = x2.shape
    tm = pos3.shape[2] // TOP_K
    nt = seq // tm
    return pl.pallas_call(
        functools.partial(_combine_kernel, tm=tm),
        out_shape=jax.ShapeDtypeStruct((nbatch, seq, d), F32),
        grid_spec=pltpu.PrefetchScalarGridSpec(
            num_scalar_prefetch=0,
            grid=(nbatch, nt),
            in_specs=[
                pl.BlockSpec((1, 1, TOP_K * tm), lambda b, i: ((b0 + b) * nt + i, 0, 0),
                             memory_space=pltpu.SMEM),
                pl.BlockSpec((1, tm, d), lambda b, i: (b0 + b, i, 0)),
                pl.BlockSpec((tm, SUBLANES), lambda b, i: ((b0 + b) * nt + i, 0)),
                pl.BlockSpec((1, d), lambda b, i: (0, 0)),
                pl.BlockSpec(memory_space=pl.ANY),
            ],
            out_specs=pl.BlockSpec((1, tm, d), lambda b, i: (b, i, 0)),
            scratch_shapes=[pltpu.VMEM((TOP_K * tm * SUBLANES, LANES), F32), pltpu.SemaphoreType.DMA],
        ),
        compiler_params=_cparams(("arbitrary", "arbitrary")),
        name="combine",
    )(pos3, x2, gate_t, g_final.reshape(1, d), ys)


def _routing_plan(meta, cnt, te, tm_rows):
    tok = meta.shape[1]
    idx, rank = meta[:TOP_K], meta[TOP_K:]
    counts = cnt[:, 0].astype(I32)
    tiles = (counts + te - 1) // te
    tile_end = jnp.cumsum(tiles)
    tile_start = tile_end - tiles
    offsets = tile_start * te
    n_used = tile_end[-1]
    pos = rank + jnp.sum(jnp.where(idx[None] == jnp.arange(N_EXPERTS, dtype=I32)[:, None, None],
                                   offsets[:, None, None], 0), axis=0)
    ntile_max = (tok * TOP_K) // te + N_EXPERTS
    t_ids = jnp.minimum(jnp.arange(ntile_max, dtype=I32), n_used - 1)
    tile_expert = jnp.sum((t_ids[:, None] >= tile_end[None, :]).astype(I32), axis=1)
    ztile = jnp.where(tiles > 0, tile_end - 1, 0).astype(I32)
    ntok_tiles = tok // tm_rows
    pos3 = pos.reshape(TOP_K, ntok_tiles, tm_rows).transpose(1, 0, 2).reshape(ntok_tiles, 1, TOP_K * tm_rows)
    return pos3, tile_expert, t_ids, n_used.reshape(1), ztile, ntile_max * te


def kernel(x_prompt, x_sample, mem_prompt, mem_sample, g_mix, w_in, sinks, g_attn_out, g_four_out, w_out,
           g_cross, g_mem, w_q_cross, w_kv_cross, w_o_cross, g_moe, w_router, b_router, w_gate_up,
           b_gate_up, w_down, b_down, g_final):
    n_prompt, seq, _ = x_prompt.shape
    n_sample = x_sample.shape[0]
    l = 0
    kvc = _memkv(mem_prompt, mem_sample, g_mem[l], w_kv_cross[l])
    q, kv, ab = _inproj(x_prompt, x_sample, g_mix[l], w_in[l])
    a = _swa(q, kv, sinks[l])
    f = _seqdft(ab)
    x2, h3, meta, gates, cnt = _mid(x_prompt, x_sample, a, f, kvc, g_attn_out[l], g_four_out[l], w_out[l],
                                    g_cross[l], w_q_cross[l], w_o_cross[l], g_moe[l], w_router[l], b_router[l])
    tm_rows = min(TM_ROWS, seq)
    te = TM_EXPERT
    pos3, tile_expert, tile_block, n_used, ztile, rows_pad = _routing_plan(meta, cnt, te, tm_rows)
    xs = _dispatch(h3, pos3, ztile, rows_pad, te)
    ys = _experts(xs, tile_expert, tile_block, n_used, w_gate_up[l], b_gate_up[l], w_down[l], b_down[l], te)
    gate_t = gates.T
    y_p = _combine(x2, gate_t, pos3, ys, g_final, 0, n_prompt)
    y_s = _combine(x2, gate_t, pos3, ys, g_final, n_prompt, n_sample)
    return (y_p, y_s)
```

```python
import functools

import numpy as np
import jax
import jax.numpy as jnp
from jax import lax
from jax.experimental import pallas as pl
from jax.experimental.pallas import tpu as pltpu

F32 = jnp.float32
BF16 = jnp.bfloat16
I32 = jnp.int32

HEAD_DIM = 64
N_Q_HEADS = 8
N_KV_HEADS = 2
ATTN_WIDTH = N_Q_HEADS * HEAD_DIM
KV_WIDTH = N_KV_HEADS * HEAD_DIM
FOURIER_WIDTH = 512
FOURIER_GROUP = 64
WINDOW = 128
ROPE_DIM = 16
ROPE_THETA = 500000.0
N_CROSS_HEADS = 4
CROSS_HEAD_DIM = 128
CROSS_WIDTH = N_CROSS_HEADS * CROSS_HEAD_DIM
N_EXPERTS = 32
TOP_K = 4
SWIGLU_LIMIT = 7.0
SWIGLU_ALPHA = 1.702
EPS = 1e-5
NEG_INF = -1e30

LANES = 128
SUBLANES = 8
VMEM_LIMIT_BYTES = 56 * 1024 * 1024

TS_INPROJ = 512
TQ_ATTN = 512
TR_DFT = 512
TM_MID = 512
TM_ROWS = 256
TM_EXPERT = 512


def _cparams(sem):
    return pltpu.CompilerParams(dimension_semantics=sem, vmem_limit_bytes=VMEM_LIMIT_BYTES)


def _rms(x, g):
    return x * lax.rsqrt(jnp.mean(x * x, axis=-1, keepdims=True) + EPS) * g


def _two_group_maps(n_prompt):
    def prompt_b(b):
        return jnp.minimum(b, n_prompt - 1)

    def sample_b(b):
        return jnp.maximum(b - n_prompt, 0)
    return prompt_b, sample_b


def _memkv_kernel(mp_ref, ms_ref, g_ref, w_ref, o_ref, *, n_prompt):
    b = pl.program_id(0)
    m = jnp.where(b < n_prompt, mp_ref[0], ms_ref[0])
    mn = _rms(m, g_ref[...]).astype(BF16)
    o_ref[0] = jnp.dot(mn, w_ref[...], preferred_element_type=F32).astype(BF16)


def _memkv(mem_p, mem_s, g_mem, w_kv):
    n_prompt, n_mem, d = mem_p.shape
    nb = n_prompt + mem_s.shape[0]
    pb, sb = _two_group_maps(n_prompt)
    return pl.pallas_call(
        functools.partial(_memkv_kernel, n_prompt=n_prompt),
        out_shape=jax.ShapeDtypeStruct((nb, n_mem, w_kv.shape[1]), BF16),
        grid=(nb,),
        in_specs=[
            pl.BlockSpec((1, n_mem, d), lambda b: (pb(b), 0, 0)),
            pl.BlockSpec((1, n_mem, d), lambda b: (sb(b), 0, 0)),
            pl.BlockSpec((1, d), lambda b: (0, 0)),
            pl.BlockSpec(w_kv.shape, lambda b: (0, 0)),
        ],
        out_specs=pl.BlockSpec((1, n_mem, w_kv.shape[1]), lambda b: (b, 0, 0)),
        compiler_params=_cparams(("arbitrary",)),
        name="memkv",
    )(mem_p, mem_s, g_mem.reshape(1, d), w_kv.astype(BF16))


_ROT_W = ATTN_WIDTH + 2 * KV_WIDTH


def _inproj_kernel(xp_ref, xs_ref, g_ref, w_ref, cos_ref, sin_ref, bdc_ref, bds_ref,
                   q_ref, kv_ref, ab_ref, *, n_prompt):
    b = pl.program_id(1)
    x = jnp.where(b < n_prompt, xp_ref[0], xs_ref[0])
    h = _rms(x, g_ref[...]).astype(BF16)
    z = jnp.dot(h, w_ref[...], preferred_element_type=F32)
    cos = cos_ref[...]
    sin = sin_ref[...]
    lane = lax.broadcasted_iota(I32, cos.shape, 1) & (HEAD_DIM - 1)
    first_half = lane < ROPE_DIM // 2
    rot = []
    for c in range(_ROT_W // LANES):
        zc = z[:, c * LANES:(c + 1) * LANES]
        partner = jnp.where(first_half,
                            pltpu.roll(zc, LANES - ROPE_DIM // 2, axis=1),
                            pltpu.roll(zc, ROPE_DIM // 2, axis=1))
        rot.append(zc * cos + partner * sin)
    nq = ATTN_WIDTH // LANES
    q_ref[0] = (jnp.concatenate(rot[:nq], axis=1) * (HEAD_DIM ** -0.5)).astype(BF16)
    v_off = _ROT_W
    u_off = _ROT_W + 2 * KV_WIDTH
    kv_ref[0] = jnp.concatenate(rot[nq:] + [z[:, v_off:u_off]], axis=1).astype(BF16)
    u = z[:, u_off:].astype(BF16)
    ab_ref[0] = jnp.concatenate(
        [jnp.dot(u, bdc_ref[...], preferred_element_type=F32),
         jnp.dot(u, bds_ref[...], preferred_element_type=F32)], axis=1).astype(BF16)


def _rope_tables(seq):
    half = ROPE_DIM // 2
    inv_freq = ROPE_THETA ** (-(jnp.arange(half, dtype=F32) * 2.0) / ROPE_DIM)
    ang = jnp.arange(seq).astype(F32)[:, None] * inv_freq[None, :]
    cos, sin = jnp.cos(ang), jnp.sin(ang)
    rest = HEAD_DIM - ROPE_DIM
    cos_h = jnp.concatenate([cos, cos, jnp.ones((seq, rest), F32)], axis=1)
    sin_h = jnp.concatenate([-sin, sin, jnp.zeros((seq, rest), F32)], axis=1)
    reps = LANES // HEAD_DIM
    return jnp.tile(cos_h, (1, reps)), jnp.tile(sin_h, (1, reps))


def _channel_dft_tables():
    n = FOURIER_GROUP
    idx = np.arange(n)
    ang = 2.0 * np.pi * ((idx[:, None] * idx[None, :]) % n) / n
    eye = np.eye(FOURIER_WIDTH // n)
    bdc = np.kron(eye, np.cos(ang))
    bds = np.kron(eye, np.sin(ang))
    return jnp.asarray(bdc, BF16), jnp.asarray(bds, BF16)


def _inproj(x_p, x_s, g_mix, w_in):
    n_prompt, seq, d = x_p.shape
    nb = n_prompt + x_s.shape[0]
    ts = min(TS_INPROJ, seq)
    o1, o2, o3 = ATTN_WIDTH, ATTN_WIDTH + KV_WIDTH, ATTN_WIDTH + 2 * KV_WIDTH
    swap = lambda w: jnp.concatenate([w[:, HEAD_DIM:], w[:, :HEAD_DIM]], axis=1)
    wk, wv = w_in[:, o1:o2], w_in[:, o2:o3]
    w_ext = jnp.concatenate([w_in[:, :o1], wk, swap(wk), wv, swap(wv), w_in[:, o3:]], axis=1).astype(BF16)
    cos_t, sin_t = _rope_tables(seq)
    bdc, bds = _channel_dft_tables()
    pb, sb = _two_group_maps(n_prompt)
    kvw = 4 * KV_WIDTH
    return pl.pallas_call(
        functools.partial(_inproj_kernel, n_prompt=n_prompt),
        out_shape=(jax.ShapeDtypeStruct((nb, seq, ATTN_WIDTH), BF16),
                   jax.ShapeDtypeStruct((nb, seq, kvw), BF16),
                   jax.ShapeDtypeStruct((nb, seq, 2 * FOURIER_WIDTH), BF16)),
        grid=(seq // ts, nb),
        in_specs=[
            pl.BlockSpec((1, ts, d), lambda i, b: (pb(b), i, 0)),
            pl.BlockSpec((1, ts, d), lambda i, b: (sb(b), i, 0)),
            pl.BlockSpec((1, d), lambda i, b: (0, 0)),
            pl.BlockSpec(w_ext.shape, lambda i, b: (0, 0)),
            pl.BlockSpec((ts, LANES), lambda i, b: (i, 0)),
            pl.BlockSpec((ts, LANES), lambda i, b: (i, 0)),
            pl.BlockSpec(bdc.shape, lambda i, b: (0, 0)),
            pl.BlockSpec(bds.shape, lambda i, b: (0, 0)),
        ],
        out_specs=(pl.BlockSpec((1, ts, ATTN_WIDTH), lambda i, b: (b, i, 0)),
                   pl.BlockSpec((1, ts, kvw), lambda i, b: (b, i, 0)),
                   pl.BlockSpec((1, ts, 2 * FOURIER_WIDTH), lambda i, b: (b, i, 0))),
        compiler_params=_cparams(("arbitrary", "arbitrary")),
        name="inproj",
    )(x_p, x_s, g_mix.reshape(1, d), w_ext, cos_t, sin_t, bdc, bds)


def _swa_kernel(sink_ref, q_ref, kvm_ref, kvp_ref, kvn_ref, bias_ref, o_ref, *, nblk):
    i = pl.program_id(1)
    last = pl.num_programs(1) - 1
    kvw = jnp.concatenate([kvp_ref[0], kvm_ref[0], kvn_ref[0]], axis=0)
    k, ksw, v, vsw = [kvw[:, c * LANES:(c + 1) * LANES] for c in range(4)]
    lo = lax.broadcasted_iota(I32, k.shape, 1) < HEAD_DIM
    zero = jnp.zeros_like(k)
    k_lo = (jnp.where(lo, k, zero), jnp.where(lo, ksw, zero))
    k_hi = (jnp.where(lo, zero, ksw), jnp.where(lo, zero, k))
    v_lo = (jnp.where(lo, v, zero), jnp.where(lo, vsw, zero))
    v_hi = (jnp.where(lo, zero, vsw), jnp.where(lo, zero, v))
    row = lax.broadcasted_iota(I32, (2 * WINDOW, 1), 0)
    for j in range(nblk):
        bias = bias_ref[1]
        if j == 0:
            bias = jnp.where(i == 0, bias_ref[0], bias)
        if j == nblk - 1:
            bias = jnp.where(i == last, bias_ref[2], bias)
        bias2 = jnp.concatenate([bias, bias], axis=0)
        rows = slice(j * WINDOW, (j + 1) * WINDOW)
        win = slice(j * WINDOW, (j + 3) * WINDOW)
        for kvh in range(N_KV_HEADS):
            c0 = 2 * kvh
            q2 = jnp.concatenate([q_ref[0, rows, c0 * LANES:(c0 + 1) * LANES],
                                  q_ref[0, rows, (c0 + 1) * LANES:(c0 + 2) * LANES]], axis=0)
            acc = None
            for half, (kk, vv) in enumerate(((k_lo[kvh], v_lo[kvh]), (k_hi[kvh], v_hi[kvh]))):
                s = lax.dot_general(q2, kk[win], (((1,), (1,)), ((), ())),
                                    preferred_element_type=F32) + bias2
                sink = jnp.where(row < WINDOW, sink_ref[4 * kvh + half], sink_ref[4 * kvh + 2 + half])
                m = jnp.maximum(jnp.max(s, axis=-1, keepdims=True), sink)
                e = jnp.exp(s - m)
                den = jnp.sum(e, axis=-1, keepdims=True) + jnp.exp(sink - m)
                pv = jnp.dot(e.astype(BF16), vv[win], preferred_element_type=F32)
                contrib = pv * (1.0 / den)
                acc = contrib if acc is None else acc + contrib
            o_ref[0, rows, c0 * LANES:(c0 + 1) * LANES] = acc[:WINDOW].astype(o_ref.dtype)
            o_ref[0, rows, (c0 + 1) * LANES:(c0 + 2) * LANES] = acc[WINDOW:].astype(o_ref.dtype)


def _swa_bias():
    qi = np.arange(WINDOW)[:, None]
    c = np.arange(3 * WINDOW)[None, :]
    band = np.abs(c - WINDOW - qi) <= WINDOW
    first = band & (c >= WINDOW)
    lastb = band & (c < 2 * WINDOW)
    tab = np.stack([first, band, lastb]).astype(np.float32)
    return jnp.asarray((1.0 - tab) * NEG_INF, F32)


def _swa(q, kv, sinks):
    nb, seq, _ = q.shape
    tq = min(TQ_ATTN, seq)
    nblk = tq // WINDOW
    nseq_blk = seq // WINDOW
    kvw = kv.shape[-1]
    assert nblk >= 2 and seq % tq == 0, "first/last window masks are applied to distinct query blocks"
    bias = _swa_bias()
    return pl.pallas_call(
        functools.partial(_swa_kernel, nblk=nblk),
        out_shape=jax.ShapeDtypeStruct((nb, seq, ATTN_WIDTH), BF16),
        grid=(nb, seq // tq),
        in_specs=[
            pl.BlockSpec(memory_space=pltpu.SMEM),
            pl.BlockSpec((1, tq, ATTN_WIDTH), lambda b, i: (b, i, 0)),
            pl.BlockSpec((1, tq, kvw), lambda b, i: (b, i, 0)),
            pl.BlockSpec((1, WINDOW, kvw), lambda b, i: (b, jnp.maximum(i * nblk - 1, 0), 0)),
            pl.BlockSpec((1, WINDOW, kvw), lambda b, i: (b, jnp.minimum((i + 1) * nblk, nseq_blk - 1), 0)),
            pl.BlockSpec(bias.shape, lambda b, i: (0, 0, 0)),
        ],
        out_specs=pl.BlockSpec((1, tq, ATTN_WIDTH), lambda b, i: (b, i, 0)),
        compiler_params=_cparams(("arbitrary", "arbitrary")),
        name="swa",
    )(sinks.astype(F32), q, kv, kv, kv, bias)


def _seqdft_kernel(cs_ref, ss_ref, ab_ref, o_ref, *, scale):
    a = ab_ref[0, :, :FOURIER_WIDTH]
    bm = ab_ref[0, :, FOURIER_WIDTH:]
    r = (jnp.dot(cs_ref[...], a, preferred_element_type=F32)
         - jnp.dot(ss_ref[...], bm, preferred_element_type=F32))
    o_ref[0] = (r * scale).astype(o_ref.dtype)


def _seq_dft_tables(seq):
    n1 = 64 if seq % 64 == 0 and seq > 64 else 1
    n2 = seq // n1
    k = np.arange(seq)[:, None]
    a1 = 2.0 * np.pi * ((k * n2 * np.arange(n1)[None, :]) % seq) / seq
    a2 = 2.0 * np.pi * ((k * np.arange(n2)[None, :]) % seq) / seq
    c1, s1 = jnp.asarray(np.cos(a1), F32), jnp.asarray(np.sin(a1), F32)
    c2, s2 = jnp.asarray(np.cos(a2), F32), jnp.asarray(np.sin(a2), F32)
    cs = c1[:, :, None] * c2[:, None, :] - s1[:, :, None] * s2[:, None, :]
    ss = s1[:, :, None] * c2[:, None, :] + c1[:, :, None] * s2[:, None, :]
    return cs.reshape(seq, seq).astype(BF16), ss.reshape(seq, seq).astype(BF16)


def _seqdft(ab):
    nb, seq, _ = ab.shape
    tr = min(TR_DFT, seq)
    cs, ss = _seq_dft_tables(seq)
    scale = float(1.0 / np.sqrt(seq * FOURIER_GROUP))
    return pl.pallas_call(
        functools.partial(_seqdft_kernel, scale=scale),
        out_shape=jax.ShapeDtypeStruct((nb, seq, FOURIER_WIDTH), BF16),
        grid=(nb, seq // tr),
        in_specs=[
            pl.BlockSpec((tr, seq), lambda b, i: (i, 0)),
            pl.BlockSpec((tr, seq), lambda b, i: (i, 0)),
            pl.BlockSpec((1, seq, 2 * FOURIER_WIDTH), lambda b, i: (b, 0, 0)),
        ],
        out_specs=pl.BlockSpec((1, tr, FOURIER_WIDTH), lambda b, i: (b, i, 0)),
        compiler_params=_cparams(("arbitrary", "arbitrary")),
        name="seqdft",
    )(cs, ss, ab)


def _mid_kernel(xp_ref, xs_ref, a_ref, f_ref, kv_ref, gao_ref, gfo_ref, wout_ref, gc_ref, wq_ref,
                wo_ref, gm_ref, wrh_ref, wrl_ref, br_ref, tri_ref,
                x2_ref, h3_ref, meta_ref, gate_ref, cnt_ref, cnt_sc, *, n_prompt):
    b = pl.program_id(0)
    i = pl.program_id(1)
    tm = a_ref.shape[1]

    @pl.when((b == 0) & (i == 0))
    def _():
        cnt_sc[...] = jnp.zeros_like(cnt_sc)

    x = jnp.where(b < n_prompt, xp_ref[0], xs_ref[0])
    an = _rms(a_ref[0].astype(F32), gao_ref[...]).astype(BF16)
    fn = _rms(f_ref[0].astype(F32), gfo_ref[...]).astype(BF16)
    x1 = (x + jnp.dot(an, wout_ref[:ATTN_WIDTH, :], preferred_element_type=F32)
          + jnp.dot(fn, wout_ref[ATTN_WIDTH:, :], preferred_element_type=F32))

    h2 = _rms(x1, gc_ref[...]).astype(BF16)
    qc = (jnp.dot(h2, wq_ref[...], preferred_element_type=F32) * (CROSS_HEAD_DIM ** -0.5)).astype(BF16)
    heads = []
    for hd in range(N_CROSS_HEADS):
        cols = slice(hd * CROSS_HEAD_DIM, (hd + 1) * CROSS_HEAD_DIM)
        vcols = slice(CROSS_WIDTH + hd * CROSS_HEAD_DIM, CROSS_WIDTH + (hd + 1) * CROSS_HEAD_DIM)
        s = lax.dot_general(qc[:, cols], kv_ref[0, :, cols], (((1,), (1,)), ((), ())),
                            preferred_element_type=F32)
        e = jnp.exp(s - jnp.max(s, axis=-1, keepdims=True))
        den = jnp.sum(e, axis=-1, keepdims=True)
        heads.append(jnp.dot(e.astype(BF16), kv_ref[0, :, vcols], preferred_element_type=F32) * (1.0 / den))
    oc = jnp.concatenate(heads, axis=1).astype(BF16)
    x2 = x1 + jnp.dot(oc, wo_ref[...], preferred_element_type=F32)
    x2_ref[0] = x2

    h3 = _rms(x2, gm_ref[...])
    for sl in range(SUBLANES):
        h3_ref[pl.ds(sl, tm, stride=SUBLANES), :] = h3[:, sl * LANES:(sl + 1) * LANES]

    hi = h3.astype(BF16)
    lo = (h3 - hi.astype(F32)).astype(BF16)
    logits = (jnp.dot(hi, wrh_ref[...], preferred_element_type=F32)
              + jnp.dot(lo, wrh_ref[...], preferred_element_type=F32)
              + jnp.dot(hi, wrl_ref[...], preferred_element_type=F32) + br_ref[...])
    lt = logits.T[:N_EXPERTS, :]
    eidx = lax.broadcasted_iota(I32, lt.shape, 0)
    vals, idxs, sels = [], [], []
    for _ in range(TOP_K):
        m = jnp.max(lt, axis=0, keepdims=True)
        idx = jnp.min(jnp.where(lt == m, eidx, N_EXPERTS), axis=0, keepdims=True)
        sel = eidx == idx
        lt = jnp.where(sel, -jnp.inf, lt)
        vals.append(m)
        idxs.append(idx)
        sels.append(sel)
    ex = [jnp.exp(vk - vals[0]) for vk in vals]
    inv = 1.0 / (ex[0] + ex[1] + ex[2] + ex[3])
    zrow = jnp.zeros_like(ex[0])
    gate_ref[...] = jnp.concatenate([e_ * inv for e_ in ex] + [zrow] * (SUBLANES - TOP_K), axis=0)

    onehot = jnp.zeros(lt.shape, F32)
    for sel in sels:
        onehot = onehot + sel.astype(F32)
    before = cnt_sc[:, 0:1] + jnp.dot(onehot.astype(BF16), tri_ref[...], preferred_element_type=F32)
    ranks = [jnp.sum(jnp.where(sel, before, 0.0), axis=0, keepdims=True).astype(I32) for sel in sels]
    meta_ref[...] = jnp.concatenate(idxs + ranks, axis=0)
    cnt_sc[...] = cnt_sc[...] + jnp.sum(onehot, axis=1, keepdims=True)
    cnt_ref[...] = cnt_sc[...]


def _mid(x_p, x_s, a, f, kvc, g_attn_out, g_four_out, w_out, g_cross, w_q, w_o, g_moe, w_router, b_router):
    n_prompt, seq, d = x_p.shape
    nb = a.shape[0]
    tm = min(TM_MID, seq)
    nt = seq // tm
    n_mem = kvc.shape[1]
    pb, sb = _two_group_maps(n_prompt)
    wr = jnp.zeros((d, LANES), F32).at[:, :N_EXPERTS].set(w_router)
    wr_hi = wr.astype(BF16)
    wr_lo = (wr - wr_hi.astype(F32)).astype(BF16)
    br = jnp.zeros((1, LANES), F32).at[0, :N_EXPERTS].set(b_router)
    tri = jnp.asarray(np.triu(np.ones((tm, tm), np.float32), 1), BF16)
    full = lambda arr: pl.BlockSpec(arr.shape, lambda b, i: (0,) * arr.ndim)
    row = lambda v: v.reshape(1, -1)
    args = [x_p, x_s, a, f, kvc, row(g_attn_out), row(g_four_out), w_out.astype(BF16), row(g_cross),
            w_q.astype(BF16), w_o.astype(BF16), row(g_moe), wr_hi, wr_lo, br, tri]
    in_specs = [
        pl.BlockSpec((1, tm, d), lambda b, i: (pb(b), i, 0)),
        pl.BlockSpec((1, tm, d), lambda b, i: (sb(b), i, 0)),
        pl.BlockSpec((1, tm, ATTN_WIDTH), lambda b, i: (b, i, 0)),
        pl.BlockSpec((1, tm, FOURIER_WIDTH), lambda b, i: (b, i, 0)),
        pl.BlockSpec((1, n_mem, 2 * CROSS_WIDTH), lambda b, i: (b, 0, 0)),
    ] + [full(v) for v in args[5:]]
    tok = nb * seq
    return pl.pallas_call(
        functools.partial(_mid_kernel, n_prompt=n_prompt),
        out_shape=(jax.ShapeDtypeStruct((nb, seq, d), F32),
                   jax.ShapeDtypeStruct((tok * SUBLANES, LANES), F32),
                   jax.ShapeDtypeStruct((SUBLANES, tok), I32),
                   jax.ShapeDtypeStruct((SUBLANES, tok), F32),
                   jax.ShapeDtypeStruct((N_EXPERTS, LANES), F32)),
        grid=(nb, nt),
        in_specs=in_specs,
        out_specs=(pl.BlockSpec((1, tm, d), lambda b, i: (b, i, 0)),
                   pl.BlockSpec((tm * SUBLANES, LANES), lambda b, i: (b * nt + i, 0)),
                   pl.BlockSpec((SUBLANES, tm), lambda b, i: (0, b * nt + i)),
                   pl.BlockSpec((SUBLANES, tm), lambda b, i: (0, b * nt + i)),
                   pl.BlockSpec((N_EXPERTS, LANES), lambda b, i: (0, 0))),
        scratch_shapes=[pltpu.VMEM((N_EXPERTS, LANES), F32)],
        compiler_params=_cparams(("arbitrary", "arbitrary")),
        name="mid",
    )(*args)


def _row_copy(src_rows, dst_rows, sem):
    return pltpu.make_async_copy(src_rows, dst_rows, sem)


def _dispatch_kernel(pos_ref, ztile_ref, h3_ref, xs_ref, zbuf, sem, *, tm, te):
    i = pl.program_id(0)

    @pl.when(i == 0)
    def _():
        zbuf[...] = jnp.zeros_like(zbuf)
        for e in range(N_EXPERTS):
            r0 = pl.multiple_of(ztile_ref[e] * (te * SUBLANES), te * SUBLANES)
            _row_copy(zbuf, xs_ref.at[pl.ds(r0, te * SUBLANES), :], sem).start()
        for e in range(N_EXPERTS):
            _row_copy(zbuf, xs_ref.at[pl.ds(0, te * SUBLANES), :], sem).wait()

    def issue(t, carry):
        src = h3_ref.at[pl.ds(pl.multiple_of(t * SUBLANES, SUBLANES), SUBLANES), :]
        for k in range(TOP_K):
            r0 = pl.multiple_of(pos_ref[0, 0, k * tm + t] * SUBLANES, SUBLANES)
            _row_copy(src, xs_ref.at[pl.ds(r0, SUBLANES), :], sem).start(priority=k % 2)
        return carry

    lax.fori_loop(0, tm, issue, 0)

    def drain(k, carry):
        _row_copy(h3_ref, xs_ref.at[pl.ds(0, tm * SUBLANES), :], sem).wait()
        return carry

    lax.fori_loop(0, TOP_K, drain, 0)


def _dispatch(h3, pos3, ztile, rows_pad, te):
    ntile, _, four_tm = pos3.shape
    tm = four_tm // TOP_K
    return pl.pallas_call(
        functools.partial(_dispatch_kernel, tm=tm, te=te),
        out_shape=jax.ShapeDtypeStruct((rows_pad * SUBLANES, LANES), F32),
        grid_spec=pltpu.PrefetchScalarGridSpec(
            num_scalar_prefetch=0,
            grid=(ntile,),
            in_specs=[
                pl.BlockSpec((1, 1, four_tm), lambda i: (i, 0, 0), memory_space=pltpu.SMEM),
                pl.BlockSpec(memory_space=pltpu.SMEM),
                pl.BlockSpec((tm * SUBLANES, LANES), lambda i: (i, 0)),
            ],
            out_specs=pl.BlockSpec(memory_space=pl.ANY),
            scratch_shapes=[pltpu.VMEM((te * SUBLANES, LANES), F32), pltpu.SemaphoreType.DMA],
        ),
        compiler_params=_cparams(("arbitrary",)),
        name="dispatch",
    )(pos3, ztile, h3)


def _expert_kernel(te_ref, tb_ref, nu_ref, xs_ref, wgu_ref, bgu_ref, wd_ref, bd_ref, ys_ref, wgu_bf, wd_bf):
    i = pl.program_id(0)
    te = xs_ref.shape[0] // SUBLANES
    dff = wd_ref.shape[1]

    @pl.when((i == 0) | (te_ref[i] != te_ref[jnp.maximum(i - 1, 0)]))
    def _():
        wgu_bf[...] = wgu_ref[0].astype(BF16)
        wd_bf[...] = wd_ref[0].astype(BF16)

    @pl.when(i < nu_ref[0])
    def _():
        x = jnp.concatenate([xs_ref[pl.ds(sl, te, stride=SUBLANES), :] for sl in range(SUBLANES)],
                            axis=1).astype(BF16)
        gu = jnp.dot(x, wgu_bf[...], preferred_element_type=F32) + bgu_ref[0]
        gate = jnp.minimum(gu[:, :dff], SWIGLU_LIMIT)
        up = jnp.clip(gu[:, dff:], -SWIGLU_LIMIT, SWIGLU_LIMIT)
        act = (up + 1.0) * (gate * (1.0 / (1.0 + jnp.exp(-SWIGLU_ALPHA * gate))))
        y = jnp.dot(act.astype(BF16), wd_bf[...], preferred_element_type=F32) + bd_ref[0]
        for sl in range(SUBLANES):
            ys_ref[pl.ds(sl, te, stride=SUBLANES), :] = y[:, sl * LANES:(sl + 1) * LANES]


def _experts(xs, tile_expert, tile_block, n_used, w_gate_up, b_gate_up, w_down, b_down, te):
    ne, d, two_dff = w_gate_up.shape
    dff = two_dff // 2
    ntile = tile_expert.shape[0]
    return pl.pallas_call(
        _expert_kernel,
        out_shape=jax.ShapeDtypeStruct(xs.shape, F32),
        grid_spec=pltpu.PrefetchScalarGridSpec(
            num_scalar_prefetch=3,
            grid=(ntile,),
            in_specs=[
                pl.BlockSpec((te * SUBLANES, LANES), lambda i, te_r, tb_r, nu_r: (tb_r[i], 0)),
                pl.BlockSpec((1, d, two_dff), lambda i, te_r, tb_r, nu_r: (te_r[i], 0, 0)),
                pl.BlockSpec((1, 1, two_dff), lambda i, te_r, tb_r, nu_r: (te_r[i], 0, 0)),
                pl.BlockSpec((1, dff, d), lambda i, te_r, tb_r, nu_r: (te_r[i], 0, 0)),
                pl.BlockSpec((1, 1, d), lambda i, te_r, tb_r, nu_r: (te_r[i], 0, 0)),
            ],
            out_specs=pl.BlockSpec((te * SUBLANES, LANES), lambda i, te_r, tb_r, nu_r: (tb_r[i], 0)),
            scratch_shapes=[pltpu.VMEM((d, two_dff), BF16), pltpu.VMEM((dff, d), BF16)],
        ),
        compiler_params=_cparams(("arbitrary",)),
        name="experts",
    )(tile_expert, tile_block, n_used, xs, w_gate_up, b_gate_up.reshape(ne, 1, two_dff),
      w_down, b_down.reshape(ne, 1, d))


def _combine_kernel(pos_ref, posn_ref, x2_ref, gate_ref, g_ref, ys_ref, o_ref, buf, sem, *, tm):
    step = pl.program_id(0) * pl.num_programs(1) + pl.program_id(1)
    nsteps = pl.num_programs(0) * pl.num_programs(1)
    slot = step % 2
    slot_rows = TOP_K * tm * SUBLANES

    def gather_rows(p_ref, to_slot):
        def issue(t, carry):
            for k in range(TOP_K):
                r0 = pl.multiple_of(p_ref[0, 0, k * tm + t] * SUBLANES, SUBLANES)
                d0 = pl.multiple_of(to_slot * slot_rows + (k * tm + t) * SUBLANES, SUBLANES)
                _row_copy(ys_ref.at[pl.ds(r0, SUBLANES), :], buf.at[pl.ds(d0, SUBLANES), :],
                          sem.at[to_slot]).start(priority=k % 2)
            return carry

        lax.fori_loop(0, tm, issue, 0)

    @pl.when(step == 0)
    def _():
        gather_rows(pos_ref, 0)

    @pl.when(step + 1 < nsteps)
    def _():
        gather_rows(posn_ref, 1 - slot)

    base = pl.multiple_of(slot * slot_rows, SUBLANES)
    _row_copy(ys_ref.at[pl.ds(0, slot_rows), :], buf.at[pl.ds(base, slot_rows), :], sem.at[slot]).wait()

    x = x2_ref[0]
    g = gate_ref[...]
    for k in range(TOP_K):
        yk = jnp.concatenate(
            [buf[pl.ds(base + (k * tm * SUBLANES + sl), tm, stride=SUBLANES), :] for sl in range(SUBLANES)],
            axis=1)
        x = x + g[:, k:k + 1] * yk
    o_ref[0] = _rms(x, g_ref[...])


def _combine(x2, gate_t, pos3, ys, g_final, b0, nbatch):
    _, seq, d = x2.shape
    tm = pos3.shape[2] // TOP_K
    nt = seq // tm
    last = (b0 + nbatch) * nt - 1
    return pl.pallas_call(
        functools.partial(_combine_kernel, tm=tm),
        out_shape=jax.ShapeDtypeStruct((nbatch, seq, d), F32),
        grid_spec=pltpu.PrefetchScalarGridSpec(
            num_scalar_prefetch=0,
            grid=(nbatch, nt),
            in_specs=[
                pl.BlockSpec((1, 1, TOP_K * tm), lambda b, i: ((b0 + b) * nt + i, 0, 0),
                             memory_space=pltpu.SMEM),
                pl.BlockSpec((1, 1, TOP_K * tm), lambda b, i: (jnp.minimum((b0 + b) * nt + i + 1, last), 0, 0),
                             memory_space=pltpu.SMEM),
                pl.BlockSpec((1, tm, d), lambda b, i: (b0 + b, i, 0)),
                pl.BlockSpec((tm, SUBLANES), lambda b, i: ((b0 + b) * nt + i, 0)),
                pl.BlockSpec((1, d), lambda b, i: (0, 0)),
                pl.BlockSpec(memory_space=pl.ANY),
            ],
            out_specs=pl.BlockSpec((1, tm, d), lambda b, i: (b, i, 0)),
            scratch_shapes=[pltpu.VMEM((2 * TOP_K * tm * SUBLANES, LANES), F32),
                            pltpu.SemaphoreType.DMA((2,))],
        ),
        compiler_params=_cparams(("arbitrary", "arbitrary")),
        name="combine",
    )(pos3, pos3, x2, gate_t, g_final.reshape(1, d), ys)


def _routing_plan(meta, cnt, te, tm_rows):
    tok = meta.shape[1]
    idx, rank = meta[:TOP_K], meta[TOP_K:]
    counts = cnt[:, 0].astype(I32)
    tiles = (counts + te - 1) // te
    tile_end = jnp.cumsum(tiles)
    tile_start = tile_end - tiles
    offsets = tile_start * te
    n_used = tile_end[-1]
    pos = rank + jnp.sum(jnp.where(idx[None] == jnp.arange(N_EXPERTS, dtype=I32)[:, None, None],
                                   offsets[:, None, None], 0), axis=0)
    ntile_max = (tok * TOP_K) // te + N_EXPERTS
    t_ids = jnp.minimum(jnp.arange(ntile_max, dtype=I32), n_used - 1)
    tile_expert = jnp.sum((t_ids[:, None] >= tile_end[None, :]).astype(I32), axis=1)
    ztile = jnp.where(tiles > 0, tile_end - 1, 0).astype(I32)
    ntok_tiles = tok // tm_rows
    pos3 = pos.reshape(TOP_K, ntok_tiles, tm_rows).transpose(1, 0, 2).reshape(ntok_tiles, 1, TOP_K * tm_rows)
    return pos3, tile_expert, t_ids, n_used.reshape(1), ztile, ntile_max * te


def kernel(x_prompt, x_sample, mem_prompt, mem_sample, g_mix, w_in, sinks, g_attn_out, g_four_out, w_out,
           g_cross, g_mem, w_q_cross, w_kv_cross, w_o_cross, g_moe, w_router, b_router, w_gate_up,
           b_gate_up, w_down, b_down, g_final):
    n_prompt, seq, _ = x_prompt.shape
    n_sample = x_sample.shape[0]
    l = 0
    kvc = _memkv(mem_prompt, mem_sample, g_mem[l], w_kv_cross[l])
    q, kv, ab = _inproj(x_prompt, x_sample, g_mix[l], w_in[l])
    a = _swa(q, kv, sinks[l])
    f = _seqdft(ab)
    x2, h3, meta, gates, cnt = _mid(x_prompt, x_sample, a, f, kvc, g_attn_out[l], g_four_out[l], w_out[l],
                                    g_cross[l], w_q_cross[l], w_o_cross[l], g_moe[l], w_router[l], b_router[l])
    tm_rows = min(TM_ROWS, seq)
    te = TM_EXPERT
    pos3, tile_expert, tile_block, n_used, ztile, rows_pad = _routing_plan(meta, cnt, te, tm_rows)
    xs = _dispatch(h3, pos3, ztile, rows_pad, te)
    ys = _experts(xs, tile_expert, tile_block, n_used, w_gate_up[l], b_gate_up[l], w_down[l], b_down[l], te)
    gate_t = gates.T
    y_p = _combine(x2, gate_t, pos3, ys, g_final, 0, n_prompt)
    y_s = _combine(x2, gate_t, pos3, ys, g_final, n_prompt, n_sample)
    return (y_p, y_s)
```

```python
import functools

import numpy as np
import jax
import jax.numpy as jnp
from jax import lax
from jax.experimental import pallas as pl
from jax.experimental.pallas import tpu as pltpu

F32 = jnp.float32
BF16 = jnp.bfloat16
I32 = jnp.int32

HEAD_DIM = 64
N_Q_HEADS = 8
N_KV_HEADS = 2
ATTN_WIDTH = N_Q_HEADS * HEAD_DIM
KV_WIDTH = N_KV_HEADS * HEAD_DIM
FOURIER_WIDTH = 512
FOURIER_GROUP = 64
WINDOW = 128
ROPE_DIM = 16
ROPE_THETA = 500000.0
N_CROSS_HEADS = 4
CROSS_HEAD_DIM = 128
CROSS_WIDTH = N_CROSS_HEADS * CROSS_HEAD_DIM
N_EXPERTS = 32
TOP_K = 4
SWIGLU_LIMIT = 7.0
SWIGLU_ALPHA = 1.702
EPS = 1e-5
NEG_INF = -1e30

LANES = 128
SUBLANES = 8
VMEM_LIMIT_BYTES = 56 * 1024 * 1024

TS_INPROJ = 512
TQ_ATTN = 512
TR_DFT = 512
TM_MID = 512
TM_ROWS = 256
TM_EXPERT = 512
ISSUE_UNROLL = 4


def _cparams(sem):
    return pltpu.CompilerParams(dimension_semantics=sem, vmem_limit_bytes=VMEM_LIMIT_BYTES)


def _rms(x, g):
    return x * lax.rsqrt(jnp.mean(x * x, axis=-1, keepdims=True) + EPS) * g


def _two_group_maps(n_prompt):
    def prompt_b(b):
        return jnp.minimum(b, n_prompt - 1)

    def sample_b(b):
        return jnp.maximum(b - n_prompt, 0)
    return prompt_b, sample_b


def _memkv_kernel(mp_ref, ms_ref, g_ref, w_ref, o_ref, *, n_prompt):
    b = pl.program_id(0)
    m = jnp.where(b < n_prompt, mp_ref[0], ms_ref[0])
    mn = _rms(m, g_ref[...]).astype(BF16)
    o_ref[0] = jnp.dot(mn, w_ref[...], preferred_element_type=F32).astype(BF16)


def _memkv(mem_p, mem_s, g_mem, w_kv):
    n_prompt, n_mem, d = mem_p.shape
    nb = n_prompt + mem_s.shape[0]
    pb, sb = _two_group_maps(n_prompt)
    return pl.pallas_call(
        functools.partial(_memkv_kernel, n_prompt=n_prompt),
        out_shape=jax.ShapeDtypeStruct((nb, n_mem, w_kv.shape[1]), BF16),
        grid=(nb,),
        in_specs=[
            pl.BlockSpec((1, n_mem, d), lambda b: (pb(b), 0, 0)),
            pl.BlockSpec((1, n_mem, d), lambda b: (sb(b), 0, 0)),
            pl.BlockSpec((1, d), lambda b: (0, 0)),
            pl.BlockSpec(w_kv.shape, lambda b: (0, 0)),
        ],
        out_specs=pl.BlockSpec((1, n_mem, w_kv.shape[1]), lambda b: (b, 0, 0)),
        compiler_params=_cparams(("arbitrary",)),
        name="memkv",
    )(mem_p, mem_s, g_mem.reshape(1, d), w_kv.astype(BF16))


_ROT_W = ATTN_WIDTH + 2 * KV_WIDTH


def _inproj_kernel(xp_ref, xs_ref, g_ref, w_ref, cos_ref, sin_ref, bd_ref,
                   q_ref, kv_ref, ab_ref, *, n_prompt):
    b = pl.program_id(1)
    x = jnp.where(b < n_prompt, xp_ref[0], xs_ref[0])
    h = _rms(x, g_ref[...]).astype(BF16)
    z = jnp.dot(h, w_ref[...], preferred_element_type=F32)
    cos = cos_ref[...]
    sin = sin_ref[...]
    lane = lax.broadcasted_iota(I32, cos.shape, 1) & (HEAD_DIM - 1)
    first_half = lane < ROPE_DIM // 2
    rot = []
    for c in range(_ROT_W // LANES):
        zc = z[:, c * LANES:(c + 1) * LANES]
        partner = jnp.where(first_half,
                            pltpu.roll(zc, LANES - ROPE_DIM // 2, axis=1),
                            pltpu.roll(zc, ROPE_DIM // 2, axis=1))
        rot.append(zc * cos + partner * sin)
    nq = ATTN_WIDTH // LANES
    q_ref[0] = (jnp.concatenate(rot[:nq], axis=1) * (HEAD_DIM ** -0.5)).astype(BF16)
    v_off = _ROT_W
    u_off = _ROT_W + 2 * KV_WIDTH
    kv_ref[0] = jnp.concatenate(rot[nq:] + [z[:, v_off:u_off]], axis=1).astype(BF16)
    half = FOURIER_WIDTH // 2
    u = z[:, u_off:].astype(BF16)
    r0 = jnp.dot(u[:, :half], bd_ref[...], preferred_element_type=F32)
    r1 = jnp.dot(u[:, half:], bd_ref[...], preferred_element_type=F32)
    ab_ref[0] = jnp.concatenate([r0[:, :half], r1[:, :half], r0[:, half:], r1[:, half:]],
                                axis=1).astype(BF16)


def _rope_tables(seq):
    half = ROPE_DIM // 2
    inv_freq = ROPE_THETA ** (-(jnp.arange(half, dtype=F32) * 2.0) / ROPE_DIM)
    ang = jnp.arange(seq).astype(F32)[:, None] * inv_freq[None, :]
    cos, sin = jnp.cos(ang), jnp.sin(ang)
    rest = HEAD_DIM - ROPE_DIM
    cos_h = jnp.concatenate([cos, cos, jnp.ones((seq, rest), F32)], axis=1)
    sin_h = jnp.concatenate([-sin, sin, jnp.zeros((seq, rest), F32)], axis=1)
    reps = LANES // HEAD_DIM
    return jnp.tile(cos_h, (1, reps)), jnp.tile(sin_h, (1, reps))


def _channel_dft_tables():
    n = FOURIER_GROUP
    idx = np.arange(n)
    ang = 2.0 * np.pi * ((idx[:, None] * idx[None, :]) % n) / n
    eye = np.eye(FOURIER_WIDTH // (2 * n))
    return jnp.asarray(np.concatenate([np.kron(eye, np.cos(ang)), np.kron(eye, np.sin(ang))], axis=1), BF16)


def _inproj(x_p, x_s, g_mix, w_in):
    n_prompt, seq, d = x_p.shape
    nb = n_prompt + x_s.shape[0]
    ts = min(TS_INPROJ, seq)
    o1, o2, o3 = ATTN_WIDTH, ATTN_WIDTH + KV_WIDTH, ATTN_WIDTH + 2 * KV_WIDTH
    swap = lambda w: jnp.concatenate([w[:, HEAD_DIM:], w[:, :HEAD_DIM]], axis=1)
    wk, wv = w_in[:, o1:o2], w_in[:, o2:o3]
    w_ext = jnp.concatenate([w_in[:, :o1], wk, swap(wk), wv, swap(wv), w_in[:, o3:]], axis=1).astype(BF16)
    cos_t, sin_t = _rope_tables(seq)
    bd = _channel_dft_tables()
    pb, sb = _two_group_maps(n_prompt)
    kvw = 4 * KV_WIDTH
    return pl.pallas_call(
        functools.partial(_inproj_kernel, n_prompt=n_prompt),
        out_shape=(jax.ShapeDtypeStruct((nb, seq, ATTN_WIDTH), BF16),
                   jax.ShapeDtypeStruct((nb, seq, kvw), BF16),
                   jax.ShapeDtypeStruct((nb, seq, 2 * FOURIER_WIDTH), BF16)),
        grid=(seq // ts, nb),
        in_specs=[
            pl.BlockSpec((1, ts, d), lambda i, b: (pb(b), i, 0)),
            pl.BlockSpec((1, ts, d), lambda i, b: (sb(b), i, 0)),
            pl.BlockSpec((1, d), lambda i, b: (0, 0)),
            pl.BlockSpec(w_ext.shape, lambda i, b: (0, 0)),
            pl.BlockSpec((ts, LANES), lambda i, b: (i, 0)),
            pl.BlockSpec((ts, LANES), lambda i, b: (i, 0)),
            pl.BlockSpec(bd.shape, lambda i, b: (0, 0)),
        ],
        out_specs=(pl.BlockSpec((1, ts, ATTN_WIDTH), lambda i, b: (b, i, 0)),
                   pl.BlockSpec((1, ts, kvw), lambda i, b: (b, i, 0)),
                   pl.BlockSpec((1, ts, 2 * FOURIER_WIDTH), lambda i, b: (b, i, 0))),
        compiler_params=_cparams(("arbitrary", "arbitrary")),
        name="inproj",
    )(x_p, x_s, g_mix.reshape(1, d), w_ext, cos_t, sin_t, bd)


def _swa_kernel(sink_ref, q_ref, kvm_ref, kvp_ref, kvn_ref, bias_ref, o_ref, *, nblk):
    i = pl.program_id(1)
    last = pl.num_programs(1) - 1
    kvw = jnp.concatenate([kvp_ref[0], kvm_ref[0], kvn_ref[0]], axis=0)
    k, ksw, v, vsw = [kvw[:, c * LANES:(c + 1) * LANES] for c in range(4)]
    lo = lax.broadcasted_iota(I32, k.shape, 1) < HEAD_DIM
    zero = jnp.zeros_like(k)
    k_lo = (jnp.where(lo, k, zero), jnp.where(lo, ksw, zero))
    k_hi = (jnp.where(lo, zero, ksw), jnp.where(lo, zero, k))
    v_lo = (jnp.where(lo, v, zero), jnp.where(lo, vsw, zero))
    v_hi = (jnp.where(lo, zero, vsw), jnp.where(lo, zero, v))
    row = lax.broadcasted_iota(I32, (2 * WINDOW, 1), 0)
    for j in range(nblk):
        bias = bias_ref[1]
        if j == 0:
            bias = jnp.where(i == 0, bias_ref[0], bias)
        if j == nblk - 1:
            bias = jnp.where(i == last, bias_ref[2], bias)
        bias2 = jnp.concatenate([bias, bias], axis=0)
        rows = slice(j * WINDOW, (j + 1) * WINDOW)
        win = slice(j * WINDOW, (j + 3) * WINDOW)
        for kvh in range(N_KV_HEADS):
            c0 = 2 * kvh
            q2 = jnp.concatenate([q_ref[0, rows, c0 * LANES:(c0 + 1) * LANES],
                                  q_ref[0, rows, (c0 + 1) * LANES:(c0 + 2) * LANES]], axis=0)
            acc = None
            for half, (kk, vv) in enumerate(((k_lo[kvh], v_lo[kvh]), (k_hi[kvh], v_hi[kvh]))):
                s = lax.dot_general(q2, kk[win], (((1,), (1,)), ((), ())),
                                    preferred_element_type=F32) + bias2
                sink = jnp.where(row < WINDOW, sink_ref[4 * kvh + half], sink_ref[4 * kvh + 2 + half])
                m = jnp.maximum(jnp.max(s, axis=-1, keepdims=True), sink)
                e = jnp.exp(s - m)
                den = jnp.sum(e, axis=-1, keepdims=True) + jnp.exp(sink - m)
                pv = jnp.dot(e.astype(BF16), vv[win], preferred_element_type=F32)
                contrib = pv * (1.0 / den)
                acc = contrib if acc is None else acc + contrib
            o_ref[0, rows, c0 * LANES:(c0 + 1) * LANES] = acc[:WINDOW].astype(o_ref.dtype)
            o_ref[0, rows, (c0 + 1) * LANES:(c0 + 2) * LANES] = acc[WINDOW:].astype(o_ref.dtype)


def _swa_bias():
    qi = np.arange(WINDOW)[:, None]
    c = np.arange(3 * WINDOW)[None, :]
    band = np.abs(c - WINDOW - qi) <= WINDOW
    first = band & (c >= WINDOW)
    lastb = band & (c < 2 * WINDOW)
    tab = np.stack([first, band, lastb]).astype(np.float32)
    return jnp.asarray((1.0 - tab) * NEG_INF, F32)


def _swa(q, kv, sinks):
    nb, seq, _ = q.shape
    tq = min(TQ_ATTN, seq)
    nblk = tq // WINDOW
    nseq_blk = seq // WINDOW
    kvw = kv.shape[-1]
    assert nblk >= 2 and seq % tq == 0, "first/last window masks are applied to distinct query blocks"
    bias = _swa_bias()
    return pl.pallas_call(
        functools.partial(_swa_kernel, nblk=nblk),
        out_shape=jax.ShapeDtypeStruct((nb, seq, ATTN_WIDTH), BF16),
        grid=(nb, seq // tq),
        in_specs=[
            pl.BlockSpec(memory_space=pltpu.SMEM),
            pl.BlockSpec((1, tq, ATTN_WIDTH), lambda b, i: (b, i, 0)),
            pl.BlockSpec((1, tq, kvw), lambda b, i: (b, i, 0)),
            pl.BlockSpec((1, WINDOW, kvw), lambda b, i: (b, jnp.maximum(i * nblk - 1, 0), 0)),
            pl.BlockSpec((1, WINDOW, kvw), lambda b, i: (b, jnp.minimum((i + 1) * nblk, nseq_blk - 1), 0)),
            pl.BlockSpec(bias.shape, lambda b, i: (0, 0, 0)),
        ],
        out_specs=pl.BlockSpec((1, tq, ATTN_WIDTH), lambda b, i: (b, i, 0)),
        compiler_params=_cparams(("arbitrary", "arbitrary")),
        name="swa",
    )(sinks.astype(F32), q, kv, kv, kv, bias)


_DFT_PAD_ROWS = 16


def _seqdft_kernel(ab_ref, ct_hbm, st_hbm, perm_ref, o_ref, ct, st, aebo, hbuf, sem, *, scale, tk):
    n = ab_ref.shape[1]
    m = n // 2
    fw = FOURIER_WIDTH
    nblk = m // LANES

    @pl.when(pl.program_id(0) == 0)
    def _():
        copies = [pltpu.make_async_copy(ct_hbm, ct, sem.at[0]), pltpu.make_async_copy(st_hbm, st, sem.at[1])]
        for c in copies:
            c.start()
        for c in copies:
            c.wait()

    perm = perm_ref[...]
    for blk in range(nblk):
        mirrored = [ab_ref[0, n - LANES * (blk + 1):n - LANES * blk, :]]
        if blk > 0:
            mirrored.append(ab_ref[0, n - LANES * blk:n - LANES * (blk - 1), :])
        else:
            mirrored.append(jnp.zeros((LANES, 2 * fw), BF16))
        r = jnp.dot(perm, jnp.concatenate(mirrored, axis=0), preferred_element_type=F32)
        cur = ab_ref[0, LANES * blk:LANES * (blk + 1), :].astype(F32)
        aebo[LANES * blk:LANES * (blk + 1), :] = jnp.concatenate(
            [cur[:, :fw] + r[:, :fw], cur[:, fw:] - r[:, fw:]], axis=1).astype(BF16)

    a_mid = ab_ref[0, m:m + _DFT_PAD_ROWS, :fw][0:1, :].astype(F32)

    def pq(rows, nrows):
        p = jnp.dot(ct[rows, :], aebo[:, :fw], preferred_element_type=F32)
        q = jnp.dot(st[rows, :], aebo[:, fw:], preferred_element_type=F32)
        odd = (lax.broadcasted_iota(I32, (nrows, fw), 0) & 1) == 1
        return p + jnp.where(odd, -a_mid, a_mid), q

    for kt in range(m // tk):
        rows = slice(kt * tk, (kt + 1) * tk)
        p, q = pq(rows, tk)
        o_ref[0, rows, :] = ((p - q) * scale).astype(o_ref.dtype)
        hbuf[rows, :] = p + q
    p, q = pq(slice(m, m + _DFT_PAD_ROWS), _DFT_PAD_ROWS)
    hbuf[m:m + _DFT_PAD_ROWS, :] = p + q
    hbuf[m + _DFT_PAD_ROWS:, :] = jnp.zeros((LANES - _DFT_PAD_ROWS, fw), F32)
    for c in range(nblk):
        win = hbuf[LANES * c:LANES * (c + 2), :].astype(BF16)
        blk_out = jnp.dot(perm, win, preferred_element_type=F32)
        o_ref[0, n - LANES * (c + 1):n - LANES * c, :] = (blk_out * scale).astype(o_ref.dtype)


def _seq_dft_tables(seq):
    m = seq // 2
    n2 = 64
    n1 = m // n2
    k = np.arange(m + _DFT_PAD_ROWS)[:, None]
    a1 = 2.0 * np.pi * ((k * n2 * np.arange(n1)[None, :]) % seq) / seq
    a2 = 2.0 * np.pi * ((k * np.arange(n2)[None, :]) % seq) / seq
    c1, s1 = jnp.asarray(np.cos(a1), F32), jnp.asarray(np.sin(a1), F32)
    c2, s2 = jnp.asarray(np.cos(a2), F32), jnp.asarray(np.sin(a2), F32)
    cs = c1[:, :, None] * c2[:, None, :] - s1[:, :, None] * s2[:, None, :]
    ss = s1[:, :, None] * c2[:, None, :] + c1[:, :, None] * s2[:, None, :]
    return cs.reshape(-1, m).astype(BF16), ss.reshape(-1, m).astype(BF16)


def _seqdft(ab):
    nb, seq, _ = ab.shape
    m = seq // 2
    tk = min(TR_DFT, m)
    assert m % LANES == 0 and m % tk == 0 and m % 64 == 0
    ct, st = _seq_dft_tables(seq)
    perm = np.zeros((LANES, 2 * LANES), np.float32)
    perm[np.arange(LANES), LANES - np.arange(LANES)] = 1.0
    scale = float(1.0 / np.sqrt(seq * FOURIER_GROUP))
    return pl.pallas_call(
        functools.partial(_seqdft_kernel, scale=scale, tk=tk),
        out_shape=jax.ShapeDtypeStruct((nb, seq, FOURIER_WIDTH), BF16),
        grid=(nb,),
        in_specs=[
            pl.BlockSpec((1, seq, 2 * FOURIER_WIDTH), lambda b: (b, 0, 0)),
            pl.BlockSpec(memory_space=pl.ANY),
            pl.BlockSpec(memory_space=pl.ANY),
            pl.BlockSpec(perm.shape, lambda b: (0, 0)),
        ],
        out_specs=pl.BlockSpec((1, seq, FOURIER_WIDTH), lambda b: (b, 0, 0)),
        scratch_shapes=[pltpu.VMEM(ct.shape, BF16), pltpu.VMEM(st.shape, BF16),
                        pltpu.VMEM((m, 2 * FOURIER_WIDTH), BF16),
                        pltpu.VMEM((m + LANES, FOURIER_WIDTH), F32),
                        pltpu.SemaphoreType.DMA((2,))],
        compiler_params=_cparams(("arbitrary",)),
        name="seqdft",
    )(ab, ct, st, jnp.asarray(perm, BF16))


def _mid_kernel(xp_ref, xs_ref, a_ref, f_ref, kv_ref, gao_ref, gfo_ref, wout_ref, gc_ref, wq_ref,
                wo_ref, gm_ref, wr_ref, br_ref, tri_ref,
                x2_ref, h3_ref, meta_ref, gate_ref, cnt_ref, cnt_sc, *, n_prompt):
    b = pl.program_id(0)
    i = pl.program_id(1)
    tm = a_ref.shape[1]

    @pl.when((b == 0) & (i == 0))
    def _():
        cnt_sc[...] = jnp.zeros_like(cnt_sc)

    x = jnp.where(b < n_prompt, xp_ref[0], xs_ref[0])
    an = _rms(a_ref[0].astype(F32), gao_ref[...]).astype(BF16)
    fn = _rms(f_ref[0].astype(F32), gfo_ref[...]).astype(BF16)
    x1 = (x + jnp.dot(an, wout_ref[:ATTN_WIDTH, :], preferred_element_type=F32)
          + jnp.dot(fn, wout_ref[ATTN_WIDTH:, :], preferred_element_type=F32))

    h2 = _rms(x1, gc_ref[...]).astype(BF16)
    qc = (jnp.dot(h2, wq_ref[...], preferred_element_type=F32) * (CROSS_HEAD_DIM ** -0.5)).astype(BF16)
    heads = []
    for hd in range(N_CROSS_HEADS):
        cols = slice(hd * CROSS_HEAD_DIM, (hd + 1) * CROSS_HEAD_DIM)
        vcols = slice(CROSS_WIDTH + hd * CROSS_HEAD_DIM, CROSS_WIDTH + (hd + 1) * CROSS_HEAD_DIM)
        s = lax.dot_general(qc[:, cols], kv_ref[0, :, cols], (((1,), (1,)), ((), ())),
                            preferred_element_type=F32)
        e = jnp.exp(s - jnp.max(s, axis=-1, keepdims=True))
        den = jnp.sum(e, axis=-1, keepdims=True)
        heads.append(jnp.dot(e.astype(BF16), kv_ref[0, :, vcols], preferred_element_type=F32) * (1.0 / den))
    oc = jnp.concatenate(heads, axis=1).astype(BF16)
    x2 = x1 + jnp.dot(oc, wo_ref[...], preferred_element_type=F32)
    x2_ref[0] = x2

    h3 = _rms(x2, gm_ref[...])
    for sl in range(SUBLANES):
        h3_ref[pl.ds(sl, tm, stride=SUBLANES), :] = h3[:, sl * LANES:(sl + 1) * LANES]

    hi = h3.astype(BF16)
    lo = (h3 - hi.astype(F32)).astype(BF16)
    hw = jnp.dot(hi, wr_ref[...], preferred_element_type=F32)
    logits = (hw[:, :LANES] + hw[:, LANES:]
              + jnp.dot(lo, wr_ref[:, :LANES], preferred_element_type=F32) + br_ref[...])
    lt = logits.T[:N_EXPERTS, :]
    eidx = lax.broadcasted_iota(I32, lt.shape, 0)
    vals, idxs, sels = [], [], []
    for _ in range(TOP_K):
        m = jnp.max(lt, axis=0, keepdims=True)
        idx = jnp.min(jnp.where(lt == m, eidx, N_EXPERTS), axis=0, keepdims=True)
        sel = eidx == idx
        lt = jnp.where(sel, -jnp.inf, lt)
        vals.append(m)
        idxs.append(idx)
        sels.append(sel)
    ex = [jnp.exp(vk - vals[0]) for vk in vals]
    inv = 1.0 / (ex[0] + ex[1] + ex[2] + ex[3])
    zrow = jnp.zeros_like(ex[0])
    gate_ref[...] = jnp.concatenate([e_ * inv for e_ in ex] + [zrow] * (SUBLANES - TOP_K), axis=0)

    onehot = jnp.zeros(lt.shape, F32)
    for sel in sels:
        onehot = onehot + sel.astype(F32)
    before = cnt_sc[:, 0:1] + jnp.dot(onehot.astype(BF16), tri_ref[...], preferred_element_type=F32)
    ranks = [jnp.sum(jnp.where(sel, before, 0.0), axis=0, keepdims=True).astype(I32) for sel in sels]
    meta_ref[...] = jnp.concatenate(idxs + ranks, axis=0)
    cnt_sc[...] = cnt_sc[...] + jnp.sum(onehot, axis=1, keepdims=True)
    cnt_ref[...] = cnt_sc[...]


def _mid(x_p, x_s, a, f, kvc, g_attn_out, g_four_out, w_out, g_cross, w_q, w_o, g_moe, w_router, b_router):
    n_prompt, seq, d = x_p.shape
    nb = a.shape[0]
    tm = min(TM_MID, seq)
    nt = seq // tm
    n_mem = kvc.shape[1]
    pb, sb = _two_group_maps(n_prompt)
    wr = jnp.zeros((d, LANES), F32).at[:, :N_EXPERTS].set(w_router)
    wr_hi = wr.astype(BF16)
    wr_lo = (wr - wr_hi.astype(F32)).astype(BF16)
    br = jnp.zeros((1, LANES), F32).at[0, :N_EXPERTS].set(b_router)
    tri = jnp.asarray(np.triu(np.ones((tm, tm), np.float32), 1), BF16)
    full = lambda arr: pl.BlockSpec(arr.shape, lambda b, i: (0,) * arr.ndim)
    row = lambda v: v.reshape(1, -1)
    args = [x_p, x_s, a, f, kvc, row(g_attn_out), row(g_four_out), w_out.astype(BF16), row(g_cross),
            w_q.astype(BF16), w_o.astype(BF16), row(g_moe), jnp.concatenate([wr_hi, wr_lo], axis=1), br, tri]
    in_specs = [
        pl.BlockSpec((1, tm, d), lambda b, i: (pb(b), i, 0)),
        pl.BlockSpec((1, tm, d), lambda b, i: (sb(b), i, 0)),
        pl.BlockSpec((1, tm, ATTN_WIDTH), lambda b, i: (b, i, 0)),
        pl.BlockSpec((1, tm, FOURIER_WIDTH), lambda b, i: (b, i, 0)),
        pl.BlockSpec((1, n_mem, 2 * CROSS_WIDTH), lambda b, i: (b, 0, 0)),
    ] + [full(v) for v in args[5:]]
    tok = nb * seq
    return pl.pallas_call(
        functools.partial(_mid_kernel, n_prompt=n_prompt),
        out_shape=(jax.ShapeDtypeStruct((nb, seq, d), F32),
                   jax.ShapeDtypeStruct((tok * SUBLANES, LANES), F32),
                   jax.ShapeDtypeStruct((SUBLANES, tok), I32),
                   jax.ShapeDtypeStruct((SUBLANES, tok), F32),
                   jax.ShapeDtypeStruct((N_EXPERTS, LANES), F32)),
        grid=(nb, nt),
        in_specs=in_specs,
        out_specs=(pl.BlockSpec((1, tm, d), lambda b, i: (b, i, 0)),
                   pl.BlockSpec((tm * SUBLANES, LANES), lambda b, i: (b * nt + i, 0)),
                   pl.BlockSpec((SUBLANES, tm), lambda b, i: (0, b * nt + i)),
                   pl.BlockSpec((SUBLANES, tm), lambda b, i: (0, b * nt + i)),
                   pl.BlockSpec((N_EXPERTS, LANES), lambda b, i: (0, 0))),
        scratch_shapes=[pltpu.VMEM((N_EXPERTS, LANES), F32)],
        compiler_params=_cparams(("arbitrary", "arbitrary")),
        name="mid",
    )(*args)


def _row_copy(src_rows, dst_rows, sem):
    return pltpu.make_async_copy(src_rows, dst_rows, sem)


def _dispatch_kernel(pos_ref, ztile_ref, h3_ref, xs_ref, zbuf, sem, *, tm, te):
    i = pl.program_id(0)

    @pl.when(i == 0)
    def _():
        zbuf[...] = jnp.zeros_like(zbuf)
        for e in range(N_EXPERTS):
            r0 = pl.multiple_of(ztile_ref[e] * (te * SUBLANES), te * SUBLANES)
            _row_copy(zbuf, xs_ref.at[pl.ds(r0, te * SUBLANES), :], sem).start()
        for e in range(N_EXPERTS):
            _row_copy(zbuf, xs_ref.at[pl.ds(0, te * SUBLANES), :], sem).wait()

    def issue(tb, carry):
        for u in range(ISSUE_UNROLL):
            t = tb * ISSUE_UNROLL + u
            src = h3_ref.at[pl.ds(pl.multiple_of(t * SUBLANES, SUBLANES), SUBLANES), :]
            for k in range(TOP_K):
                r0 = pl.multiple_of(pos_ref[0, 0, k * tm + t] * SUBLANES, SUBLANES)
                _row_copy(src, xs_ref.at[pl.ds(r0, SUBLANES), :], sem).start(priority=k % 2)
        return carry

    lax.fori_loop(0, tm // ISSUE_UNROLL, issue, 0)

    def drain(k, carry):
        _row_copy(h3_ref, xs_ref.at[pl.ds(0, tm * SUBLANES), :], sem).wait()
        return carry

    lax.fori_loop(0, TOP_K, drain, 0)


def _dispatch(h3, pos3, ztile, rows_pad, te):
    ntile, _, four_tm = pos3.shape
    tm = four_tm // TOP_K
    return pl.pallas_call(
        functools.partial(_dispatch_kernel, tm=tm, te=te),
        out_shape=jax.ShapeDtypeStruct((rows_pad * SUBLANES, LANES), F32),
        grid_spec=pltpu.PrefetchScalarGridSpec(
            num_scalar_prefetch=0,
            grid=(ntile,),
            in_specs=[
                pl.BlockSpec((1, 1, four_tm), lambda i: (i, 0, 0), memory_space=pltpu.SMEM),
                pl.BlockSpec(memory_space=pltpu.SMEM),
                pl.BlockSpec((tm * SUBLANES, LANES), lambda i: (i, 0)),
            ],
            out_specs=pl.BlockSpec(memory_space=pl.ANY),
            scratch_shapes=[pltpu.VMEM((te * SUBLANES, LANES), F32), pltpu.SemaphoreType.DMA],
        ),
        compiler_params=_cparams(("arbitrary",)),
        name="dispatch",
    )(pos3, ztile, h3)


def _expert_kernel(te_ref, tb_ref, nu_ref, xs_ref, wgu_ref, bgu_ref, wd_ref, bd_ref, ys_ref, wgu_bf, wd_bf):
    i = pl.program_id(0)
    te = xs_ref.shape[0] // SUBLANES
    dff = wd_ref.shape[1]

    @pl.when((i == 0) | (te_ref[i] != te_ref[jnp.maximum(i - 1, 0)]))
    def _():
        wgu_bf[...] = wgu_ref[0].astype(BF16)
        wd_bf[...] = wd_ref[0].astype(BF16)

    @pl.when(i < nu_ref[0])
    def _():
        x = jnp.concatenate([xs_ref[pl.ds(sl, te, stride=SUBLANES), :] for sl in range(SUBLANES)],
                            axis=1).astype(BF16)
        gu = jnp.dot(x, wgu_bf[...], preferred_element_type=F32) + bgu_ref[0]
        gate = jnp.minimum(gu[:, :dff], SWIGLU_LIMIT)
        up = jnp.clip(gu[:, dff:], -SWIGLU_LIMIT, SWIGLU_LIMIT)
        act = (up + 1.0) * (gate * (1.0 / (1.0 + jnp.exp(-SWIGLU_ALPHA * gate))))
        y = jnp.dot(act.astype(BF16), wd_bf[...], preferred_element_type=F32) + bd_ref[0]
        for sl in range(SUBLANES):
            ys_ref[pl.ds(sl, te, stride=SUBLANES), :] = y[:, sl * LANES:(sl + 1) * LANES]


def _experts(xs, tile_expert, tile_block, n_used, w_gate_up, b_gate_up, w_down, b_down, te):
    ne, d, two_dff = w_gate_up.shape
    dff = two_dff // 2
    ntile = tile_expert.shape[0]
    return pl.pallas_call(
        _expert_kernel,
        out_shape=jax.ShapeDtypeStruct(xs.shape, F32),
        grid_spec=pltpu.PrefetchScalarGridSpec(
            num_scalar_prefetch=3,
            grid=(ntile,),
            in_specs=[
                pl.BlockSpec((te * SUBLANES, LANES), lambda i, te_r, tb_r, nu_r: (tb_r[i], 0)),
                pl.BlockSpec((1, d, two_dff), lambda i, te_r, tb_r, nu_r: (te_r[i], 0, 0)),
                pl.BlockSpec((1, 1, two_dff), lambda i, te_r, tb_r, nu_r: (te_r[i], 0, 0)),
                pl.BlockSpec((1, dff, d), lambda i, te_r, tb_r, nu_r: (te_r[i], 0, 0)),
                pl.BlockSpec((1, 1, d), lambda i, te_r, tb_r, nu_r: (te_r[i], 0, 0)),
            ],
            out_specs=pl.BlockSpec((te * SUBLANES, LANES), lambda i, te_r, tb_r, nu_r: (tb_r[i], 0)),
            scratch_shapes=[pltpu.VMEM((d, two_dff), BF16), pltpu.VMEM((dff, d), BF16)],
        ),
        compiler_params=_cparams(("arbitrary",)),
        name="experts",
    )(tile_expert, tile_block, n_used, xs, w_gate_up, b_gate_up.reshape(ne, 1, two_dff),
      w_down, b_down.reshape(ne, 1, d))


def _combine_kernel(pos_ref, posn_ref, x2_ref, gate_ref, g_ref, ys_ref, o_ref, buf, sem, *, tm):
    step = pl.program_id(0) * pl.num_programs(1) + pl.program_id(1)
    nsteps = pl.num_programs(0) * pl.num_programs(1)
    slot = step % 2
    slot_rows = TOP_K * tm * SUBLANES

    def gather_rows(p_ref, to_slot):
        def issue(tb, carry):
            for u in range(ISSUE_UNROLL):
                t = tb * ISSUE_UNROLL + u
                for k in range(TOP_K):
                    r0 = pl.multiple_of(p_ref[0, 0, k * tm + t] * SUBLANES, SUBLANES)
                    d0 = pl.multiple_of(to_slot * slot_rows + (k * tm + t) * SUBLANES, SUBLANES)
                    _row_copy(ys_ref.at[pl.ds(r0, SUBLANES), :], buf.at[pl.ds(d0, SUBLANES), :],
                              sem.at[to_slot]).start(priority=k % 2)
            return carry

        lax.fori_loop(0, tm // ISSUE_UNROLL, issue, 0)

    @pl.when(step == 0)
    def _():
        gather_rows(pos_ref, 0)

    @pl.when(step + 1 < nsteps)
    def _():
        gather_rows(posn_ref, 1 - slot)

    base = pl.multiple_of(slot * slot_rows, SUBLANES)
    _row_copy(ys_ref.at[pl.ds(0, slot_rows), :], buf.at[pl.ds(base, slot_rows), :], sem.at[slot]).wait()

    x = x2_ref[0]
    g = gate_ref[...]
    for k in range(TOP_K):
        yk = jnp.concatenate(
            [buf[pl.ds(base + (k * tm * SUBLANES + sl), tm, stride=SUBLANES), :] for sl in range(SUBLANES)],
            axis=1)
        x = x + g[:, k:k + 1] * yk
    o_ref[0] = _rms(x, g_ref[...])


def _combine(x2, gate_t, pos3, ys, g_final, b0, nbatch):
    _, seq, d = x2.shape
    tm = pos3.shape[2] // TOP_K
    nt = seq // tm
    last = (b0 + nbatch) * nt - 1
    return pl.pallas_call(
        functools.partial(_combine_kernel, tm=tm),
        out_shape=jax.ShapeDtypeStruct((nbatch, seq, d), F32),
        grid_spec=pltpu.PrefetchScalarGridSpec(
            num_scalar_prefetch=0,
            grid=(nbatch, nt),
            in_specs=[
                pl.BlockSpec((1, 1, TOP_K * tm), lambda b, i: ((b0 + b) * nt + i, 0, 0),
                             memory_space=pltpu.SMEM),
                pl.BlockSpec((1, 1, TOP_K * tm), lambda b, i: (jnp.minimum((b0 + b) * nt + i + 1, last), 0, 0),
                             memory_space=pltpu.SMEM),
                pl.BlockSpec((1, tm, d), lambda b, i: (b0 + b, i, 0)),
                pl.BlockSpec((tm, SUBLANES), lambda b, i: ((b0 + b) * nt + i, 0)),
                pl.BlockSpec((1, d), lambda b, i: (0, 0)),
                pl.BlockSpec(memory_space=pl.ANY),
            ],
            out_specs=pl.BlockSpec((1, tm, d), lambda b, i: (b, i, 0)),
            scratch_shapes=[pltpu.VMEM((2 * TOP_K * tm * SUBLANES, LANES), F32),
                            pltpu.SemaphoreType.DMA((2,))],
        ),
        compiler_params=_cparams(("arbitrary", "arbitrary")),
        name="combine",
    )(pos3, pos3, x2, gate_t, g_final.reshape(1, d), ys)


def _routing_plan(meta, cnt, te, tm_rows):
    tok = meta.shape[1]
    idx, rank = meta[:TOP_K], meta[TOP_K:]
    counts = cnt[:, 0].astype(I32)
    tiles = (counts + te - 1) // te
    tile_end = jnp.cumsum(tiles)
    tile_start = tile_end - tiles
    offsets = tile_start * te
    n_used = tile_end[-1]
    pos = rank + jnp.sum(jnp.where(idx[None] == jnp.arange(N_EXPERTS, dtype=I32)[:, None, None],
                                   offsets[:, None, None], 0), axis=0)
    ntile_max = (tok * TOP_K) // te + N_EXPERTS
    t_ids = jnp.minimum(jnp.arange(ntile_max, dtype=I32), n_used - 1)
    tile_expert = jnp.sum((t_ids[:, None] >= tile_end[None, :]).astype(I32), axis=1)
    ztile = jnp.where(tiles > 0, tile_end - 1, 0).astype(I32)
    ntok_tiles = tok // tm_rows
    pos3 = pos.reshape(TOP_K, ntok_tiles, tm_rows).transpose(1, 0, 2).reshape(ntok_tiles, 1, TOP_K * tm_rows)
    return pos3, tile_expert, t_ids, n_used.reshape(1), ztile, ntile_max * te


def kernel(x_prompt, x_sample, mem_prompt, mem_sample, g_mix, w_in, sinks, g_attn_out, g_four_out, w_out,
           g_cross, g_mem, w_q_cross, w_kv_cross, w_o_cross, g_moe, w_router, b_router, w_gate_up,
           b_gate_up, w_down, b_down, g_final):
    n_prompt, seq, _ = x_prompt.shape
    n_sample = x_sample.shape[0]
    l = 0
    kvc = _memkv(mem_prompt, mem_sample, g_mem[l], w_kv_cross[l])
    q, kv, ab = _inproj(x_prompt, x_sample, g_mix[l], w_in[l])
    a = _swa(q, kv, sinks[l])
    f = _seqdft(ab)
    x2, h3, meta, gates, cnt = _mid(x_prompt, x_sample, a, f, kvc, g_attn_out[l], g_four_out[l], w_out[l],
                                    g_cross[l], w_q_cross[l], w_o_cross[l], g_moe[l], w_router[l], b_router[l])
    tm_rows = min(TM_ROWS, seq)
    te = TM_EXPERT
    pos3, tile_expert, tile_block, n_used, ztile, rows_pad = _routing_plan(meta, cnt, te, tm_rows)
    xs = _dispatch(h3, pos3, ztile, rows_pad, te)
    ys = _experts(xs, tile_expert, tile_block, n_used, w_gate_up[l], b_gate_up[l], w_down[l], b_down[l], te)
    gate_t = gates.T
    y_p = _combine(x2, gate_t, pos3, ys, g_final, 0, n_prompt)
    y_s = _combine(x2, gate_t, pos3, ys, g_final, n_prompt, n_sample)
    return (y_p, y_s)
```

```python
import functools

import numpy as np
import jax
import jax.numpy as jnp
from jax import lax
from jax.experimental import pallas as pl
from jax.experimental.pallas import tpu as pltpu

F32 = jnp.float32
BF16 = jnp.bfloat16
I32 = jnp.int32

HEAD_DIM = 64
N_Q_HEADS = 8
N_KV_HEADS = 2
ATTN_WIDTH = N_Q_HEADS * HEAD_DIM
KV_WIDTH = N_KV_HEADS * HEAD_DIM
FOURIER_WIDTH = 512
FOURIER_GROUP = 64
WINDOW = 128
ROPE_DIM = 16
ROPE_THETA = 500000.0
N_CROSS_HEADS = 4
CROSS_HEAD_DIM = 128
CROSS_WIDTH = N_CROSS_HEADS * CROSS_HEAD_DIM
N_EXPERTS = 32
TOP_K = 4
SWIGLU_LIMIT = 7.0
SWIGLU_ALPHA = 1.702
EPS = 1e-5
NEG_INF = -1e30
LOG2E = 1.4426950408889634

LANES = 128
SUBLANES = 8
VMEM_LIMIT_BYTES = 56 * 1024 * 1024

TS_INPROJ = 512
TQ_ATTN = 512
TR_DFT = 512
TM_MID = 512
TM_ROWS = 512
TM_EXPERT = 512
ISSUE_UNROLL = 4
MID_CHAINS = 1


def _cparams(sem):
    return pltpu.CompilerParams(dimension_semantics=sem, vmem_limit_bytes=VMEM_LIMIT_BYTES)


def _rms(x, g):
    return x * lax.rsqrt(jnp.mean(x * x, axis=-1, keepdims=True) + EPS) * g


def _two_group_maps(n_prompt):
    def prompt_b(b):
        return jnp.minimum(b, n_prompt - 1)

    def sample_b(b):
        return jnp.maximum(b - n_prompt, 0)
    return prompt_b, sample_b


def _memkv_kernel(mp_ref, ms_ref, g_ref, w_ref, o_ref, *, n_prompt):
    b = pl.program_id(0)
    m = jnp.where(b < n_prompt, mp_ref[0], ms_ref[0])
    mn = _rms(m, g_ref[...]).astype(BF16)
    o_ref[0] = jnp.dot(mn, w_ref[...], preferred_element_type=F32).astype(BF16)


def _memkv(mem_p, mem_s, g_mem, w_kv):
    n_prompt, n_mem, d = mem_p.shape
    nb = n_prompt + mem_s.shape[0]
    pb, sb = _two_group_maps(n_prompt)
    return pl.pallas_call(
        functools.partial(_memkv_kernel, n_prompt=n_prompt),
        out_shape=jax.ShapeDtypeStruct((nb, n_mem, w_kv.shape[1]), BF16),
        grid=(nb,),
        in_specs=[
            pl.BlockSpec((1, n_mem, d), lambda b: (pb(b), 0, 0)),
            pl.BlockSpec((1, n_mem, d), lambda b: (sb(b), 0, 0)),
            pl.BlockSpec((1, d), lambda b: (0, 0)),
            pl.BlockSpec(w_kv.shape, lambda b: (0, 0)),
        ],
        out_specs=pl.BlockSpec((1, n_mem, w_kv.shape[1]), lambda b: (b, 0, 0)),
        compiler_params=_cparams(("arbitrary",)),
        name="memkv",
    )(mem_p, mem_s, g_mem.reshape(1, d), w_kv.astype(BF16))


_ROT_W = ATTN_WIDTH + 2 * KV_WIDTH


def _inproj_kernel(xp_ref, xs_ref, g_ref, w_ref, cos_ref, sin_ref, bd_ref,
                   q_ref, kv_ref, ab_ref, *, n_prompt):
    b = pl.program_id(1)
    x = jnp.where(b < n_prompt, xp_ref[0], xs_ref[0])
    h = _rms(x, g_ref[...]).astype(BF16)
    z = jnp.dot(h, w_ref[...], preferred_element_type=F32)
    cos = cos_ref[...]
    sin = sin_ref[...]
    lane = lax.broadcasted_iota(I32, cos.shape, 1) & (HEAD_DIM - 1)
    first_half = lane < ROPE_DIM // 2
    rot = []
    for c in range(_ROT_W // LANES):
        zc = z[:, c * LANES:(c + 1) * LANES]
        partner = jnp.where(first_half,
                            pltpu.roll(zc, LANES - ROPE_DIM // 2, axis=1),
                            pltpu.roll(zc, ROPE_DIM // 2, axis=1))
        rot.append(zc * cos + partner * sin)
    nq = ATTN_WIDTH // LANES
    q_ref[0] = (jnp.concatenate(rot[:nq], axis=1) * (LOG2E * HEAD_DIM ** -0.5)).astype(BF16)
    v_off = _ROT_W
    u_off = _ROT_W + 2 * KV_WIDTH
    kv_ref[0] = jnp.concatenate(rot[nq:] + [z[:, v_off:u_off]], axis=1).astype(BF16)
    half = FOURIER_WIDTH // 2
    u = z[:, u_off:].astype(BF16)
    r0 = jnp.dot(u[:, :half], bd_ref[...], preferred_element_type=F32)
    r1 = jnp.dot(u[:, half:], bd_ref[...], preferred_element_type=F32)
    ab_ref[0] = jnp.concatenate([r0[:, :half], r1[:, :half], r0[:, half:], r1[:, half:]],
                                axis=1).astype(BF16)


def _rope_tables(seq):
    half = ROPE_DIM // 2
    inv_freq = ROPE_THETA ** (-(jnp.arange(half, dtype=F32) * 2.0) / ROPE_DIM)
    ang = jnp.arange(seq).astype(F32)[:, None] * inv_freq[None, :]
    cos, sin = jnp.cos(ang), jnp.sin(ang)
    rest = HEAD_DIM - ROPE_DIM
    cos_h = jnp.concatenate([cos, cos, jnp.ones((seq, rest), F32)], axis=1)
    sin_h = jnp.concatenate([-sin, sin, jnp.zeros((seq, rest), F32)], axis=1)
    reps = LANES // HEAD_DIM
    return jnp.tile(cos_h, (1, reps)), jnp.tile(sin_h, (1, reps))


def _channel_dft_tables():
    n = FOURIER_GROUP
    idx = np.arange(n)
    ang = 2.0 * np.pi * ((idx[:, None] * idx[None, :]) % n) / n
    eye = np.eye(FOURIER_WIDTH // (2 * n))
    return jnp.asarray(np.concatenate([np.kron(eye, np.cos(ang)), np.kron(eye, np.sin(ang))], axis=1), BF16)


def _inproj(x_p, x_s, g_mix, w_in):
    n_prompt, seq, d = x_p.shape
    nb = n_prompt + x_s.shape[0]
    ts = min(TS_INPROJ, seq)
    o1, o2, o3 = ATTN_WIDTH, ATTN_WIDTH + KV_WIDTH, ATTN_WIDTH + 2 * KV_WIDTH
    swap = lambda w: jnp.concatenate([w[:, HEAD_DIM:], w[:, :HEAD_DIM]], axis=1)
    wk, wv = w_in[:, o1:o2], w_in[:, o2:o3]
    w_ext = jnp.concatenate([w_in[:, :o1], wk, swap(wk), wv, swap(wv), w_in[:, o3:]], axis=1).astype(BF16)
    cos_t, sin_t = _rope_tables(seq)
    bd = _channel_dft_tables()
    pb, sb = _two_group_maps(n_prompt)
    kvw = 4 * KV_WIDTH
    return pl.pallas_call(
        functools.partial(_inproj_kernel, n_prompt=n_prompt),
        out_shape=(jax.ShapeDtypeStruct((nb, seq, ATTN_WIDTH), BF16),
                   jax.ShapeDtypeStruct((nb, seq, kvw), BF16),
                   jax.ShapeDtypeStruct((nb, seq, 2 * FOURIER_WIDTH), BF16)),
        grid=(seq // ts, nb),
        in_specs=[
            pl.BlockSpec((1, ts, d), lambda i, b: (pb(b), i, 0)),
            pl.BlockSpec((1, ts, d), lambda i, b: (sb(b), i, 0)),
            pl.BlockSpec((1, d), lambda i, b: (0, 0)),
            pl.BlockSpec(w_ext.shape, lambda i, b: (0, 0)),
            pl.BlockSpec((ts, LANES), lambda i, b: (i, 0)),
            pl.BlockSpec((ts, LANES), lambda i, b: (i, 0)),
            pl.BlockSpec(bd.shape, lambda i, b: (0, 0)),
        ],
        out_specs=(pl.BlockSpec((1, ts, ATTN_WIDTH), lambda i, b: (b, i, 0)),
                   pl.BlockSpec((1, ts, kvw), lambda i, b: (b, i, 0)),
                   pl.BlockSpec((1, ts, 2 * FOURIER_WIDTH), lambda i, b: (b, i, 0))),
        compiler_params=_cparams(("arbitrary", "arbitrary")),
        name="inproj",
    )(x_p, x_s, g_mix.reshape(1, d), w_ext, cos_t, sin_t, bd)


def _swa_kernel(sink_ref, q_ref, kvm_ref, kvp_ref, kvn_ref, bias_ref, o_ref, *, nblk):
    i = pl.program_id(1)
    last = pl.num_programs(1) - 1
    kvw = jnp.concatenate([kvp_ref[0], kvm_ref[0], kvn_ref[0]], axis=0)
    k, ksw, v, vsw = [kvw[:, c * LANES:(c + 1) * LANES] for c in range(4)]
    lo_kv = lax.broadcasted_iota(I32, k.shape, 1) < HEAD_DIM
    k_dup = (jnp.where(lo_kv, k, ksw), jnp.where(lo_kv, ksw, k))
    v_dup = (jnp.where(lo_kv, v, vsw), jnp.where(lo_kv, vsw, v))
    lo_q = lax.broadcasted_iota(I32, (WINDOW, LANES), 1) < HEAD_DIM
    row = lax.broadcasted_iota(I32, (4 * WINDOW, 1), 0)
    for j in range(nblk):
        bias = bias_ref[1]
        if j == 0:
            bias = jnp.where(i == 0, bias_ref[0], bias)
        if j == nblk - 1:
            bias = jnp.where(i == last, bias_ref[2], bias)
        bias4 = jnp.concatenate([bias] * 4, axis=0)
        rows = slice(j * WINDOW, (j + 1) * WINDOW)
        win = slice(j * WINDOW, (j + 3) * WINDOW)
        for kvh in range(N_KV_HEADS):
            c0 = 2 * kvh
            qa = q_ref[0, rows, c0 * LANES:(c0 + 1) * LANES]
            qb = q_ref[0, rows, (c0 + 1) * LANES:(c0 + 2) * LANES]
            zero = jnp.zeros_like(qa)
            q4 = jnp.concatenate([jnp.where(lo_q, qa, zero), jnp.where(lo_q, zero, qa),
                                  jnp.where(lo_q, qb, zero), jnp.where(lo_q, zero, qb)], axis=0)
            s = lax.dot_general(q4, k_dup[kvh][win], (((1,), (1,)), ((), ())),
                                preferred_element_type=F32) + bias4
            h0 = 4 * kvh
            sink = jnp.where(row < WINDOW, sink_ref[h0],
                             jnp.where(row < 2 * WINDOW, sink_ref[h0 + 1],
                                       jnp.where(row < 3 * WINDOW, sink_ref[h0 + 2], sink_ref[h0 + 3])))
            m = jnp.maximum(jnp.max(s, axis=-1, keepdims=True), sink)
            e = jnp.exp2(s - m)
            den = jnp.sum(e, axis=-1, keepdims=True) + jnp.exp2(sink - m)
            o = jnp.dot(e.astype(BF16), v_dup[kvh][win], preferred_element_type=F32) * (1.0 / den)
            o_ref[0, rows, c0 * LANES:(c0 + 1) * LANES] = jnp.where(
                lo_q, o[:WINDOW], o[WINDOW:2 * WINDOW]).astype(o_ref.dtype)
            o_ref[0, rows, (c0 + 1) * LANES:(c0 + 2) * LANES] = jnp.where(
                lo_q, o[2 * WINDOW:3 * WINDOW], o[3 * WINDOW:]).astype(o_ref.dtype)


def _swa_bias():
    qi = np.arange(WINDOW)[:, None]
    c = np.arange(3 * WINDOW)[None, :]
    band = np.abs(c - WINDOW - qi) <= WINDOW
    first = band & (c >= WINDOW)
    lastb = band & (c < 2 * WINDOW)
    tab = np.stack([first, band, lastb]).astype(np.float32)
    return jnp.asarray((1.0 - tab) * NEG_INF, F32)


def _swa(q, kv, sinks):
    nb, seq, _ = q.shape
    tq = min(TQ_ATTN, seq)
    nblk = tq // WINDOW
    nseq_blk = seq // WINDOW
    kvw = kv.shape[-1]
    assert nblk >= 2 and seq % tq == 0, "first/last window masks are applied to distinct query blocks"
    bias = _swa_bias()
    return pl.pallas_call(
        functools.partial(_swa_kernel, nblk=nblk),
        out_shape=jax.ShapeDtypeStruct((nb, seq, ATTN_WIDTH), BF16),
        grid=(nb, seq // tq),
        in_specs=[
            pl.BlockSpec(memory_space=pltpu.SMEM),
            pl.BlockSpec((1, tq, ATTN_WIDTH), lambda b, i: (b, i, 0)),
            pl.BlockSpec((1, tq, kvw), lambda b, i: (b, i, 0)),
            pl.BlockSpec((1, WINDOW, kvw), lambda b, i: (b, jnp.maximum(i * nblk - 1, 0), 0)),
            pl.BlockSpec((1, WINDOW, kvw), lambda b, i: (b, jnp.minimum((i + 1) * nblk, nseq_blk - 1), 0)),
            pl.BlockSpec(bias.shape, lambda b, i: (0, 0, 0)),
        ],
        out_specs=pl.BlockSpec((1, tq, ATTN_WIDTH), lambda b, i: (b, i, 0)),
        compiler_params=_cparams(("arbitrary", "arbitrary")),
        name="swa",
    )(sinks.astype(F32) * LOG2E, q, kv, kv, kv, bias)


_DFT_PAD_ROWS = 16


def _seqdft_kernel(ab_ref, ct_hbm, st_hbm, perm_ref, o_ref, ct, st, aebo, hbuf, sem, *, scale, tk):
    n = ab_ref.shape[1]
    m = n // 2
    fw = FOURIER_WIDTH
    nblk = m // LANES

    @pl.when(pl.program_id(0) == 0)
    def _():
        copies = [pltpu.make_async_copy(ct_hbm, ct, sem.at[0]), pltpu.make_async_copy(st_hbm, st, sem.at[1])]
        for c in copies:
            c.start()
        for c in copies:
            c.wait()

    perm = perm_ref[...]
    for blk in range(nblk):
        mirrored = [ab_ref[0, n - LANES * (blk + 1):n - LANES * blk, :]]
        if blk > 0:
            mirrored.append(ab_ref[0, n - LANES * blk:n - LANES * (blk - 1), :])
        else:
            mirrored.append(jnp.zeros((LANES, 2 * fw), BF16))
        r = jnp.dot(perm, jnp.concatenate(mirrored, axis=0), preferred_element_type=F32)
        cur = ab_ref[0, LANES * blk:LANES * (blk + 1), :].astype(F32)
        aebo[LANES * blk:LANES * (blk + 1), :] = jnp.concatenate(
            [cur[:, :fw] + r[:, :fw], cur[:, fw:] - r[:, fw:]], axis=1).astype(BF16)

    a_mid = ab_ref[0, m:m + _DFT_PAD_ROWS, :fw][0:1, :].astype(F32)

    def pq(rows, nrows):
        p = jnp.dot(ct[rows, :], aebo[:, :fw], preferred_element_type=F32)
        q = jnp.dot(st[rows, :], aebo[:, fw:], preferred_element_type=F32)
        odd = (lax.broadcasted_iota(I32, (nrows, fw), 0) & 1) == 1
        return p + jnp.where(odd, -a_mid, a_mid), q

    for kt in range(m // tk):
        rows = slice(kt * tk, (kt + 1) * tk)
        p, q = pq(rows, tk)
        o_ref[0, rows, :] = ((p - q) * scale).astype(o_ref.dtype)
        hbuf[rows, :] = p + q
    p, q = pq(slice(m, m + _DFT_PAD_ROWS), _DFT_PAD_ROWS)
    hbuf[m:m + _DFT_PAD_ROWS, :] = p + q
    hbuf[m + _DFT_PAD_ROWS:, :] = jnp.zeros((LANES - _DFT_PAD_ROWS, fw), F32)
    for c in range(nblk):
        win = hbuf[LANES * c:LANES * (c + 2), :].astype(BF16)
        blk_out = jnp.dot(perm, win, preferred_element_type=F32)
        o_ref[0, n - LANES * (c + 1):n - LANES * c, :] = (blk_out * scale).astype(o_ref.dtype)


def _seq_dft_tables(seq):
    m = seq // 2
    n2 = 64
    n1 = m // n2
    k = np.arange(m + _DFT_PAD_ROWS)[:, None]
    a1 = 2.0 * np.pi * ((k * n2 * np.arange(n1)[None, :]) % seq) / seq
    a2 = 2.0 * np.pi * ((k * np.arange(n2)[None, :]) % seq) / seq
    c1, s1 = jnp.asarray(np.cos(a1), F32), jnp.asarray(np.sin(a1), F32)
    c2, s2 = jnp.asarray(np.cos(a2), F32), jnp.asarray(np.sin(a2), F32)
    cs = c1[:, :, None] * c2[:, None, :] - s1[:, :, None] * s2[:, None, :]
    ss = s1[:, :, None] * c2[:, None, :] + c1[:, :, None] * s2[:, None, :]
    return cs.reshape(-1, m).astype(BF16), ss.reshape(-1, m).astype(BF16)


def _seqdft(ab):
    nb, seq, _ = ab.shape
    m = seq // 2
    tk = min(TR_DFT, m)
    assert m % LANES == 0 and m % tk == 0 and m % 64 == 0
    ct, st = _seq_dft_tables(seq)
    perm = np.zeros((LANES, 2 * LANES), np.float32)
    perm[np.arange(LANES), LANES - np.arange(LANES)] = 1.0
    scale = float(1.0 / np.sqrt(seq * FOURIER_GROUP))
    return pl.pallas_call(
        functools.partial(_seqdft_kernel, scale=scale, tk=tk),
        out_shape=jax.ShapeDtypeStruct((nb, seq, FOURIER_WIDTH), BF16),
        grid=(nb,),
        in_specs=[
            pl.BlockSpec((1, seq, 2 * FOURIER_WIDTH), lambda b: (b, 0, 0)),
            pl.BlockSpec(memory_space=pl.ANY),
            pl.BlockSpec(memory_space=pl.ANY),
            pl.BlockSpec(perm.shape, lambda b: (0, 0)),
        ],
        out_specs=pl.BlockSpec((1, seq, FOURIER_WIDTH), lambda b: (b, 0, 0)),
        scratch_shapes=[pltpu.VMEM(ct.shape, BF16), pltpu.VMEM(st.shape, BF16),
                        pltpu.VMEM((m, 2 * FOURIER_WIDTH), BF16),
                        pltpu.VMEM((m + LANES, FOURIER_WIDTH), F32),
                        pltpu.SemaphoreType.DMA((2,))],
        compiler_params=_cparams(("arbitrary",)),
        name="seqdft",
    )(ab, ct, st, jnp.asarray(perm, BF16))


def _mid_kernel(xp_ref, xs_ref, a_ref, f_ref, kv_ref, gao_ref, gfo_ref, wout_ref, gc_ref, wq_ref,
                wo_ref, gm_ref, wr_ref, br_ref, tri_ref,
                x2_ref, h3_ref, meta_ref, gate_ref, cnt_ref, cnt_sc, *, n_prompt):
    b = pl.program_id(0)
    i = pl.program_id(1)
    tm = a_ref.shape[1]

    @pl.when((b == 0) & (i == 0))
    def _():
        cnt_sc[...] = jnp.zeros_like(cnt_sc)

    def token_chain(r0, nr):
        rows = slice(r0, r0 + nr)
        x = jnp.where(b < n_prompt, xp_ref[0, rows, :], xs_ref[0, rows, :])
        an = _rms(a_ref[0, rows, :].astype(F32), gao_ref[...]).astype(BF16)
        fn = _rms(f_ref[0, rows, :].astype(F32), gfo_ref[...]).astype(BF16)
        x1 = (x + jnp.dot(an, wout_ref[:ATTN_WIDTH, :], preferred_element_type=F32)
              + jnp.dot(fn, wout_ref[ATTN_WIDTH:, :], preferred_element_type=F32))

        h2 = _rms(x1, gc_ref[...]).astype(BF16)
        qc = (jnp.dot(h2, wq_ref[...], preferred_element_type=F32)
              * (LOG2E * CROSS_HEAD_DIM ** -0.5)).astype(BF16)
        heads = []
        for hd in range(N_CROSS_HEADS):
            cols = slice(hd * CROSS_HEAD_DIM, (hd + 1) * CROSS_HEAD_DIM)
            vcols = slice(CROSS_WIDTH + hd * CROSS_HEAD_DIM, CROSS_WIDTH + (hd + 1) * CROSS_HEAD_DIM)
            s = lax.dot_general(qc[:, cols], kv_ref[0, :, cols], (((1,), (1,)), ((), ())),
                                preferred_element_type=F32)
            e = jnp.exp2(s - jnp.max(s, axis=-1, keepdims=True))
            den = jnp.sum(e, axis=-1, keepdims=True)
            heads.append(jnp.dot(e.astype(BF16), kv_ref[0, :, vcols], preferred_element_type=F32) * (1.0 / den))
        oc = jnp.concatenate(heads, axis=1).astype(BF16)
        x2 = x1 + jnp.dot(oc, wo_ref[...], preferred_element_type=F32)
        x2_ref[0, rows, :] = x2

        h3 = _rms(x2, gm_ref[...])
        for sl in range(SUBLANES):
            h3_ref[pl.ds(r0 * SUBLANES + sl, nr, stride=SUBLANES), :] = h3[:, sl * LANES:(sl + 1) * LANES]

        hi = h3.astype(BF16)
        lo = (h3 - hi.astype(F32)).astype(BF16)
        hw = jnp.dot(hi, wr_ref[...], preferred_element_type=F32)
        return (hw[:, :LANES] + hw[:, LANES:]
                + jnp.dot(lo, wr_ref[:, :LANES], preferred_element_type=F32) + br_ref[...])

    nr = tm // MID_CHAINS
    logits = jnp.concatenate([token_chain(c * nr, nr) for c in range(MID_CHAINS)], axis=0)
    lt = logits.T[:N_EXPERTS, :]
    eidx = lax.broadcasted_iota(I32, lt.shape, 0)
    vals, idxs, sels = [], [], []
    for _ in range(TOP_K):
        m = jnp.max(lt, axis=0, keepdims=True)
        idx = jnp.min(jnp.where(lt == m, eidx, N_EXPERTS), axis=0, keepdims=True)
        sel = eidx == idx
        lt = jnp.where(sel, -jnp.inf, lt)
        vals.append(m)
        idxs.append(idx)
        sels.append(sel)
    ex = [jnp.exp(vk - vals[0]) for vk in vals]
    inv = 1.0 / (ex[0] + ex[1] + ex[2] + ex[3])
    zrow = jnp.zeros_like(ex[0])
    gate_ref[...] = jnp.concatenate([e_ * inv for e_ in ex] + [zrow] * (SUBLANES - TOP_K), axis=0)

    onehot = jnp.zeros(lt.shape, F32)
    for sel in sels:
        onehot = onehot + sel.astype(F32)
    before = cnt_sc[:, 0:1] + jnp.dot(onehot.astype(BF16), tri_ref[...], preferred_element_type=F32)
    ranks = [jnp.sum(jnp.where(sel, before, 0.0), axis=0, keepdims=True).astype(I32) for sel in sels]
    meta_ref[...] = jnp.concatenate(idxs + ranks, axis=0)
    cnt_sc[...] = cnt_sc[...] + jnp.sum(onehot, axis=1, keepdims=True)
    cnt_ref[...] = cnt_sc[...]


def _mid(x_p, x_s, a, f, kvc, g_attn_out, g_four_out, w_out, g_cross, w_q, w_o, g_moe, w_router, b_router):
    n_prompt, seq, d = x_p.shape
    nb = a.shape[0]
    tm = min(TM_MID, seq)
    nt = seq // tm
    n_mem = kvc.shape[1]
    pb, sb = _two_group_maps(n_prompt)
    wr = jnp.zeros((d, LANES), F32).at[:, :N_EXPERTS].set(w_router)
    wr_hi = wr.astype(BF16)
    wr_lo = (wr - wr_hi.astype(F32)).astype(BF16)
    br = jnp.zeros((1, LANES), F32).at[0, :N_EXPERTS].set(b_router)
    tri = jnp.asarray(np.triu(np.ones((tm, tm), np.float32), 1), BF16)
    full = lambda arr: pl.BlockSpec(arr.shape, lambda b, i: (0,) * arr.ndim)
    row = lambda v: v.reshape(1, -1)
    args = [x_p, x_s, a, f, kvc, row(g_attn_out), row(g_four_out), w_out.astype(BF16), row(g_cross),
            w_q.astype(BF16), w_o.astype(BF16), row(g_moe), jnp.concatenate([wr_hi, wr_lo], axis=1), br, tri]
    in_specs = [
        pl.BlockSpec((1, tm, d), lambda b, i: (pb(b), i, 0)),
        pl.BlockSpec((1, tm, d), lambda b, i: (sb(b), i, 0)),
        pl.BlockSpec((1, tm, ATTN_WIDTH), lambda b, i: (b, i, 0)),
        pl.BlockSpec((1, tm, FOURIER_WIDTH), lambda b, i: (b, i, 0)),
        pl.BlockSpec((1, n_mem, 2 * CROSS_WIDTH), lambda b, i: (b, 0, 0)),
    ] + [full(v) for v in args[5:]]
    tok = nb * seq
    return pl.pallas_call(
        functools.partial(_mid_kernel, n_prompt=n_prompt),
        out_shape=(jax.ShapeDtypeStruct((nb, seq, d), F32),
                   jax.ShapeDtypeStruct((tok * SUBLANES, LANES), F32),
                   jax.ShapeDtypeStruct((SUBLANES, tok), I32),
                   jax.ShapeDtypeStruct((SUBLANES, tok), F32),
                   jax.ShapeDtypeStruct((N_EXPERTS, LANES), F32)),
        grid=(nb, nt),
        in_specs=in_specs,
        out_specs=(pl.BlockSpec((1, tm, d), lambda b, i: (b, i, 0)),
                   pl.BlockSpec((tm * SUBLANES, LANES), lambda b, i: (b * nt + i, 0)),
                   pl.BlockSpec((SUBLANES, tm), lambda b, i: (0, b * nt + i)),
                   pl.BlockSpec((SUBLANES, tm), lambda b, i: (0, b * nt + i)),
                   pl.BlockSpec((N_EXPERTS, LANES), lambda b, i: (0, 0))),
        scratch_shapes=[pltpu.VMEM((N_EXPERTS, LANES), F32)],
        compiler_params=_cparams(("arbitrary", "arbitrary")),
        name="mid",
    )(*args)


def _row_copy(src_rows, dst_rows, sem):
    return pltpu.make_async_copy(src_rows, dst_rows, sem)


def _dispatch_kernel(pos_ref, ztile_ref, h3_ref, xs_ref, zbuf, sem, zsem, *, tm, te):
    i = pl.program_id(0)
    last = pl.num_programs(0) - 1
    slot = i % 2

    @pl.when(i == 0)
    def _():
        zbuf[...] = jnp.zeros_like(zbuf)
        for e in range(N_EXPERTS):
            r0 = pl.multiple_of(ztile_ref[e] * (te * SUBLANES), te * SUBLANES)
            _row_copy(zbuf, xs_ref.at[pl.ds(r0, te * SUBLANES), :], zsem).start()
        for e in range(N_EXPERTS):
            _row_copy(zbuf, xs_ref.at[pl.ds(0, te * SUBLANES), :], zsem).wait()

    def issue(tb, carry):
        for u in range(ISSUE_UNROLL):
            t = tb * ISSUE_UNROLL + u
            src = h3_ref.at[pl.ds(pl.multiple_of((i * tm + t) * SUBLANES, SUBLANES), SUBLANES), :]
            for k in range(TOP_K):
                r0 = pl.multiple_of(pos_ref[0, 0, k * tm + t] * SUBLANES, SUBLANES)
                _row_copy(src, xs_ref.at[pl.ds(r0, SUBLANES), :], sem.at[slot]).start(priority=k % 2)
        return carry

    lax.fori_loop(0, tm // ISSUE_UNROLL, issue, 0)

    def drain(s):
        for _ in range(TOP_K):
            _row_copy(h3_ref.at[pl.ds(0, tm * SUBLANES), :], xs_ref.at[pl.ds(0, tm * SUBLANES), :],
                      sem.at[s]).wait()

    @pl.when(i > 0)
    def _():
        drain(1 - slot)

    @pl.when(i == last)
    def _():
        drain(slot)


def _dispatch(h3, pos3, ztile, rows_pad, te):
    ntile, _, four_tm = pos3.shape
    tm = four_tm // TOP_K
    return pl.pallas_call(
        functools.partial(_dispatch_kernel, tm=tm, te=te),
        out_shape=jax.ShapeDtypeStruct((rows_pad * SUBLANES, LANES), F32),
        grid_spec=pltpu.PrefetchScalarGridSpec(
            num_scalar_prefetch=0,
            grid=(ntile,),
            in_specs=[
                pl.BlockSpec((1, 1, four_tm), lambda i: (i, 0, 0), memory_space=pltpu.SMEM),
                pl.BlockSpec(memory_space=pltpu.SMEM),
                pl.BlockSpec(memory_space=pl.ANY),
            ],
            out_specs=pl.BlockSpec(memory_space=pl.ANY),
            scratch_shapes=[pltpu.VMEM((te * SUBLANES, LANES), F32), pltpu.SemaphoreType.DMA((2,)),
                            pltpu.SemaphoreType.DMA],
        ),
        compiler_params=_cparams(("arbitrary",)),
        name="dispatch",
    )(pos3, ztile, h3)


def _expert_kernel(te_ref, tb_ref, nu_ref, xs_ref, wgu_ref, bgu_ref, wd_ref, bd_ref, ys_ref, wgu_bf, wd_bf):
    i = pl.program_id(0)
    te = xs_ref.shape[0] // SUBLANES
    dff = wd_ref.shape[1]

    @pl.when((i == 0) | (te_ref[i] != te_ref[jnp.maximum(i - 1, 0)]))
    def _():
        wgu_bf[...] = wgu_ref[0].astype(BF16)
        wd_bf[...] = wd_ref[0].astype(BF16)

    @pl.when(i < nu_ref[0])
    def _():
        x = jnp.concatenate([xs_ref[pl.ds(sl, te, stride=SUBLANES), :] for sl in range(SUBLANES)],
                            axis=1).astype(BF16)
        gu = jnp.dot(x, wgu_bf[...], preferred_element_type=F32) + bgu_ref[0]
        gate = jnp.minimum(gu[:, :dff], SWIGLU_LIMIT)
        up = jnp.clip(gu[:, dff:], -SWIGLU_LIMIT, SWIGLU_LIMIT)
        act = (up + 1.0) * (gate * (1.0 / (1.0 + jnp.exp(-SWIGLU_ALPHA * gate))))
        y = jnp.dot(act.astype(BF16), wd_bf[...], preferred_element_type=F32) + bd_ref[0]
        for sl in range(SUBLANES):
            ys_ref[pl.ds(sl, te, stride=SUBLANES), :] = y[:, sl * LANES:(sl + 1) * LANES]


def _experts(xs, tile_expert, tile_block, n_used, w_gate_up, b_gate_up, w_down, b_down, te):
    ne, d, two_dff = w_gate_up.shape
    dff = two_dff // 2
    ntile = tile_expert.shape[0]
    return pl.pallas_call(
        _expert_kernel,
        out_shape=jax.ShapeDtypeStruct(xs.shape, F32),
        grid_spec=pltpu.PrefetchScalarGridSpec(
            num_scalar_prefetch=3,
            grid=(ntile,),
            in_specs=[
                pl.BlockSpec((te * SUBLANES, LANES), lambda i, te_r, tb_r, nu_r: (tb_r[i], 0)),
                pl.BlockSpec((1, d, two_dff), lambda i, te_r, tb_r, nu_r: (te_r[i], 0, 0)),
                pl.BlockSpec((1, 1, two_dff), lambda i, te_r, tb_r, nu_r: (te_r[i], 0, 0)),
                pl.BlockSpec((1, dff, d), lambda i, te_r, tb_r, nu_r: (te_r[i], 0, 0)),
                pl.BlockSpec((1, 1, d), lambda i, te_r, tb_r, nu_r: (te_r[i], 0, 0)),
            ],
            out_specs=pl.BlockSpec((te * SUBLANES, LANES), lambda i, te_r, tb_r, nu_r: (tb_r[i], 0)),
            scratch_shapes=[pltpu.VMEM((d, two_dff), BF16), pltpu.VMEM((dff, d), BF16)],
        ),
        compiler_params=_cparams(("arbitrary",)),
        name="experts",
    )(tile_expert, tile_block, n_used, xs, w_gate_up, b_gate_up.reshape(ne, 1, two_dff),
      w_down, b_down.reshape(ne, 1, d))


def _combine_kernel(pos_ref, posn_ref, x2_ref, gate_ref, g_ref, ys_ref, o_ref, buf, sem, *, tm):
    step = pl.program_id(0) * pl.num_programs(1) + pl.program_id(1)
    nsteps = pl.num_programs(0) * pl.num_programs(1)
    slot = step % 2
    slot_rows = TOP_K * tm * SUBLANES

    def gather_rows(p_ref, to_slot):
        def issue(tb, carry):
            for u in range(ISSUE_UNROLL):
                t = tb * ISSUE_UNROLL + u
                for k in range(TOP_K):
                    r0 = pl.multiple_of(p_ref[0, 0, k * tm + t] * SUBLANES, SUBLANES)
                    d0 = pl.multiple_of(to_slot * slot_rows + (k * tm + t) * SUBLANES, SUBLANES)
                    _row_copy(ys_ref.at[pl.ds(r0, SUBLANES), :], buf.at[pl.ds(d0, SUBLANES), :],
                              sem.at[to_slot]).start(priority=k % 2)
            return carry

        lax.fori_loop(0, tm // ISSUE_UNROLL, issue, 0)

    @pl.when(step == 0)
    def _():
        gather_rows(pos_ref, 0)

    @pl.when(step + 1 < nsteps)
    def _():
        gather_rows(posn_ref, 1 - slot)

    base = pl.multiple_of(slot * slot_rows, SUBLANES)
    _row_copy(ys_ref.at[pl.ds(0, slot_rows), :], buf.at[pl.ds(base, slot_rows), :], sem.at[slot]).wait()

    x = x2_ref[0]
    g = gate_ref[...]
    for k in range(TOP_K):
        yk = jnp.concatenate(
            [buf[pl.ds(base + (k * tm * SUBLANES + sl), tm, stride=SUBLANES), :] for sl in range(SUBLANES)],
            axis=1)
        x = x + g[:, k:k + 1] * yk
    o_ref[0] = _rms(x, g_ref[...])


def _combine(x2, gate_t, pos3, ys, g_final, b0, nbatch):
    _, seq, d = x2.shape
    tm = pos3.shape[2] // TOP_K
    nt = seq // tm
    last = (b0 + nbatch) * nt - 1
    return pl.pallas_call(
        functools.partial(_combine_kernel, tm=tm),
        out_shape=jax.ShapeDtypeStruct((nbatch, seq, d), F32),
        grid_spec=pltpu.PrefetchScalarGridSpec(
            num_scalar_prefetch=0,
            grid=(nbatch, nt),
            in_specs=[
                pl.BlockSpec((1, 1, TOP_K * tm), lambda b, i: ((b0 + b) * nt + i, 0, 0),
                             memory_space=pltpu.SMEM),
                pl.BlockSpec((1, 1, TOP_K * tm), lambda b, i: (jnp.minimum((b0 + b) * nt + i + 1, last), 0, 0),
                             memory_space=pltpu.SMEM),
                pl.BlockSpec((1, tm, d), lambda b, i: (b0 + b, i, 0)),
                pl.BlockSpec((tm, SUBLANES), lambda b, i: ((b0 + b) * nt + i, 0)),
                pl.BlockSpec((1, d), lambda b, i: (0, 0)),
                pl.BlockSpec(memory_space=pl.ANY),
            ],
            out_specs=pl.BlockSpec((1, tm, d), lambda b, i: (b, i, 0)),
            scratch_shapes=[pltpu.VMEM((2 * TOP_K * tm * SUBLANES, LANES), F32),
                            pltpu.SemaphoreType.DMA((2,))],
        ),
        compiler_params=_cparams(("arbitrary", "arbitrary")),
        name="combine",
    )(pos3, pos3, x2, gate_t, g_final.reshape(1, d), ys)


def _routing_plan(meta, cnt, te, tm_rows):
    tok = meta.shape[1]
    idx, rank = meta[:TOP_K], meta[TOP_K:]
    counts = cnt[:, 0].astype(I32)
    tiles = (counts + te - 1) // te
    tile_end = jnp.cumsum(tiles)
    tile_start = tile_end - tiles
    offsets = tile_start * te
    n_used = tile_end[-1]
    pos = rank + jnp.sum(jnp.where(idx[None] == jnp.arange(N_EXPERTS, dtype=I32)[:, None, None],
                                   offsets[:, None, None], 0), axis=0)
    ntile_max = (tok * TOP_K) // te + N_EXPERTS
    t_ids = jnp.minimum(jnp.arange(ntile_max, dtype=I32), n_used - 1)
    tile_expert = jnp.sum((t_ids[:, None] >= tile_end[None, :]).astype(I32), axis=1)
    ztile = jnp.where(tiles > 0, tile_end - 1, 0).astype(I32)
    ntok_tiles = tok // tm_rows
    pos3 = pos.reshape(TOP_K, ntok_tiles, tm_rows).transpose(1, 0, 2).reshape(ntok_tiles, 1, TOP_K * tm_rows)
    return pos3, tile_expert, t_ids, n_used.reshape(1), ztile, ntile_max * te


def kernel(x_prompt, x_sample, mem_prompt, mem_sample, g_mix, w_in, sinks, g_attn_out, g_four_out, w_out,
           g_cross, g_mem, w_q_cross, w_kv_cross, w_o_cross, g_moe, w_router, b_router, w_gate_up,
           b_gate_up, w_down, b_down, g_final):
    n_prompt, seq, _ = x_prompt.shape
    n_sample = x_sample.shape[0]
    l = 0
    kvc = _memkv(mem_prompt, mem_sample, g_mem[l], w_kv_cross[l])
    q, kv, ab = _inproj(x_prompt, x_sample, g_mix[l], w_in[l])
    a = _swa(q, kv, sinks[l])
    f = _seqdft(ab)
    x2, h3, meta, gates, cnt = _mid(x_prompt, x_sample, a, f, kvc, g_attn_out[l], g_four_out[l], w_out[l],
                                    g_cross[l], w_q_cross[l], w_o_cross[l], g_moe[l], w_router[l], b_router[l])
    tm_rows = min(TM_ROWS, seq)
    te = TM_EXPERT
    pos3, tile_expert, tile_block, n_used, ztile, rows_pad = _routing_plan(meta, cnt, te, tm_rows)
    xs = _dispatch(h3, pos3, ztile, rows_pad, te)
    ys = _experts(xs, tile_expert, tile_block, n_used, w_gate_up[l], b_gate_up[l], w_down[l], b_down[l], te)
    gate_t = gates.T
    y_p = _combine(x2, gate_t, pos3, ys, g_final, 0, n_prompt)
    y_s = _combine(x2, gate_t, pos3, ys, g_final, n_prompt, n_sample)
    return (y_p, y_s)
```

```python
import functools

import numpy as np
import jax
import jax.numpy as jnp
from jax import lax
from jax.experimental import pallas as pl
from jax.experimental.pallas import tpu as pltpu

F32 = jnp.float32
BF16 = jnp.bfloat16
I32 = jnp.int32

HEAD_DIM = 64
N_Q_HEADS = 8
N_KV_HEADS = 2
ATTN_WIDTH = N_Q_HEADS * HEAD_DIM
KV_WIDTH = N_KV_HEADS * HEAD_DIM
FOURIER_WIDTH = 512
FOURIER_GROUP = 64
WINDOW = 128
ROPE_DIM = 16
ROPE_THETA = 500000.0
N_CROSS_HEADS = 4
CROSS_HEAD_DIM = 128
CROSS_WIDTH = N_CROSS_HEADS * CROSS_HEAD_DIM
N_EXPERTS = 32
TOP_K = 4
SWIGLU_LIMIT = 7.0
SWIGLU_ALPHA = 1.702
EPS = 1e-5
NEG_INF = -1e30
LOG2E = 1.4426950408889634

LANES = 128
SUBLANES = 8
VMEM_LIMIT_BYTES = 56 * 1024 * 1024

TS_INPROJ = 512
TQ_ATTN = 512
TR_DFT = 512
TM_MID = 512
TM_ROWS = 512
TM_EXPERT = 512
ISSUE_UNROLL = 4
MID_CHAINS = 1


def _cparams(sem):
    return pltpu.CompilerParams(dimension_semantics=sem, vmem_limit_bytes=VMEM_LIMIT_BYTES)


def _rms(x, g):
    return x * lax.rsqrt(jnp.mean(x * x, axis=-1, keepdims=True) + EPS) * g


def _two_group_maps(n_prompt):
    def prompt_b(b):
        return jnp.minimum(b, n_prompt - 1)

    def sample_b(b):
        return jnp.maximum(b - n_prompt, 0)
    return prompt_b, sample_b


def _memkv_kernel(mp_ref, ms_ref, g_ref, w_ref, o_ref, *, n_prompt):
    b = pl.program_id(0)
    m = jnp.where(b < n_prompt, mp_ref[0], ms_ref[0])
    mn = _rms(m, g_ref[...]).astype(BF16)
    o_ref[0] = jnp.dot(mn, w_ref[...], preferred_element_type=F32).astype(BF16)


def _memkv(mem_p, mem_s, g_mem, w_kv):
    n_prompt, n_mem, d = mem_p.shape
    nb = n_prompt + mem_s.shape[0]
    pb, sb = _two_group_maps(n_prompt)
    return pl.pallas_call(
        functools.partial(_memkv_kernel, n_prompt=n_prompt),
        out_shape=jax.ShapeDtypeStruct((nb, n_mem, w_kv.shape[1]), BF16),
        grid=(nb,),
        in_specs=[
            pl.BlockSpec((1, n_mem, d), lambda b: (pb(b), 0, 0)),
            pl.BlockSpec((1, n_mem, d), lambda b: (sb(b), 0, 0)),
            pl.BlockSpec((1, d), lambda b: (0, 0)),
            pl.BlockSpec(w_kv.shape, lambda b: (0, 0)),
        ],
        out_specs=pl.BlockSpec((1, n_mem, w_kv.shape[1]), lambda b: (b, 0, 0)),
        compiler_params=_cparams(("arbitrary",)),
        name="memkv",
    )(mem_p, mem_s, g_mem.reshape(1, d), w_kv.astype(BF16))


_ROT_W = ATTN_WIDTH + 2 * KV_WIDTH


def _inproj_kernel(xp_ref, xs_ref, g_ref, w_ref, cos_ref, sin_ref, bd_ref,
                   q_ref, kv_ref, ab_ref, *, n_prompt):
    b = pl.program_id(1)
    x = jnp.where(b < n_prompt, xp_ref[0], xs_ref[0])
    h = _rms(x, g_ref[...]).astype(BF16)
    z = jnp.dot(h, w_ref[...], preferred_element_type=F32)
    cos = cos_ref[...]
    sin = sin_ref[...]
    lane = lax.broadcasted_iota(I32, cos.shape, 1) & (HEAD_DIM - 1)
    first_half = lane < ROPE_DIM // 2
    rot = []
    for c in range(_ROT_W // LANES):
        zc = z[:, c * LANES:(c + 1) * LANES]
        partner = jnp.where(first_half,
                            pltpu.roll(zc, LANES - ROPE_DIM // 2, axis=1),
                            pltpu.roll(zc, ROPE_DIM // 2, axis=1))
        rot.append(zc * cos + partner * sin)
    nq = ATTN_WIDTH // LANES
    q_ref[0] = (jnp.concatenate(rot[:nq], axis=1) * (LOG2E * HEAD_DIM ** -0.5)).astype(BF16)
    v_off = _ROT_W
    u_off = _ROT_W + 2 * KV_WIDTH
    kv_ref[0] = jnp.concatenate(rot[nq:] + [z[:, v_off:u_off]], axis=1).astype(BF16)
    half = FOURIER_WIDTH // 2
    u = z[:, u_off:].astype(BF16)
    r0 = jnp.dot(u[:, :half], bd_ref[...], preferred_element_type=F32)
    r1 = jnp.dot(u[:, half:], bd_ref[...], preferred_element_type=F32)
    ab_ref[0] = jnp.concatenate([r0[:, :half], r1[:, :half], r0[:, half:], r1[:, half:]],
                                axis=1).astype(BF16)


def _rope_tables(seq):
    half = ROPE_DIM // 2
    inv_freq = ROPE_THETA ** (-(jnp.arange(half, dtype=F32) * 2.0) / ROPE_DIM)
    ang = jnp.arange(seq).astype(F32)[:, None] * inv_freq[None, :]
    cos, sin = jnp.cos(ang), jnp.sin(ang)
    rest = HEAD_DIM - ROPE_DIM
    cos_h = jnp.concatenate([cos, cos, jnp.ones((seq, rest), F32)], axis=1)
    sin_h = jnp.concatenate([-sin, sin, jnp.zeros((seq, rest), F32)], axis=1)
    reps = LANES // HEAD_DIM
    return jnp.tile(cos_h, (1, reps)), jnp.tile(sin_h, (1, reps))


def _channel_dft_tables():
    n = FOURIER_GROUP
    idx = np.arange(n)
    ang = 2.0 * np.pi * ((idx[:, None] * idx[None, :]) % n) / n
    eye = np.eye(FOURIER_WIDTH // (2 * n))
    return jnp.asarray(np.concatenate([np.kron(eye, np.cos(ang)), np.kron(eye, np.sin(ang))], axis=1), BF16)


def _inproj(x_p, x_s, g_mix, w_in):
    n_prompt, seq, d = x_p.shape
    nb = n_prompt + x_s.shape[0]
    ts = min(TS_INPROJ, seq)
    o1, o2, o3 = ATTN_WIDTH, ATTN_WIDTH + KV_WIDTH, ATTN_WIDTH + 2 * KV_WIDTH
    swap = lambda w: jnp.concatenate([w[:, HEAD_DIM:], w[:, :HEAD_DIM]], axis=1)
    wk, wv = w_in[:, o1:o2], w_in[:, o2:o3]
    w_ext = jnp.concatenate([w_in[:, :o1], wk, swap(wk), wv, swap(wv), w_in[:, o3:]], axis=1).astype(BF16)
    cos_t, sin_t = _rope_tables(seq)
    bd = _channel_dft_tables()
    pb, sb = _two_group_maps(n_prompt)
    kvw = 4 * KV_WIDTH
    return pl.pallas_call(
        functools.partial(_inproj_kernel, n_prompt=n_prompt),
        out_shape=(jax.ShapeDtypeStruct((nb, seq, ATTN_WIDTH), BF16),
                   jax.ShapeDtypeStruct((nb, seq, kvw), BF16),
                   jax.ShapeDtypeStruct((nb, seq, 2 * FOURIER_WIDTH), BF16)),
        grid=(seq // ts, nb),
        in_specs=[
            pl.BlockSpec((1, ts, d), lambda i, b: (pb(b), i, 0)),
            pl.BlockSpec((1, ts, d), lambda i, b: (sb(b), i, 0)),
            pl.BlockSpec((1, d), lambda i, b: (0, 0)),
            pl.BlockSpec(w_ext.shape, lambda i, b: (0, 0)),
            pl.BlockSpec((ts, LANES), lambda i, b: (i, 0)),
            pl.BlockSpec((ts, LANES), lambda i, b: (i, 0)),
            pl.BlockSpec(bd.shape, lambda i, b: (0, 0)),
        ],
        out_specs=(pl.BlockSpec((1, ts, ATTN_WIDTH), lambda i, b: (b, i, 0)),
                   pl.BlockSpec((1, ts, kvw), lambda i, b: (b, i, 0)),
                   pl.BlockSpec((1, ts, 2 * FOURIER_WIDTH), lambda i, b: (b, i, 0))),
        compiler_params=_cparams(("arbitrary", "arbitrary")),
        name="inproj",
    )(x_p, x_s, g_mix.reshape(1, d), w_ext, cos_t, sin_t, bd)


def _swa_kernel(sink_ref, q_ref, kvm_ref, kvp_ref, kvn_ref, bias_ref, o_ref, *, nblk):
    i = pl.program_id(1)
    last = pl.num_programs(1) - 1
    kvw = jnp.concatenate([kvp_ref[0], kvm_ref[0], kvn_ref[0]], axis=0)
    k, ksw, v, vsw = [kvw[:, c * LANES:(c + 1) * LANES] for c in range(4)]
    lo_kv = lax.broadcasted_iota(I32, k.shape, 1) < HEAD_DIM
    k_dup = (jnp.where(lo_kv, k, ksw), jnp.where(lo_kv, ksw, k))
    v_dup = (jnp.where(lo_kv, v, vsw), jnp.where(lo_kv, vsw, v))
    lo_q = lax.broadcasted_iota(I32, (WINDOW, LANES), 1) < HEAD_DIM
    row = lax.broadcasted_iota(I32, (4 * WINDOW, 1), 0)
    for j in range(nblk):
        bias = bias_ref[1]
        if j == 0:
            bias = jnp.where(i == 0, bias_ref[0], bias)
        if j == nblk - 1:
            bias = jnp.where(i == last, bias_ref[2], bias)
        bias4 = jnp.concatenate([bias] * 4, axis=0)
        rows = slice(j * WINDOW, (j + 1) * WINDOW)
        win = slice(j * WINDOW, (j + 3) * WINDOW)
        for kvh in range(N_KV_HEADS):
            c0 = 2 * kvh
            qa = q_ref[0, rows, c0 * LANES:(c0 + 1) * LANES]
            qb = q_ref[0, rows, (c0 + 1) * LANES:(c0 + 2) * LANES]
            zero = jnp.zeros_like(qa)
            q4 = jnp.concatenate([jnp.where(lo_q, qa, zero), jnp.where(lo_q, zero, qa),
                                  jnp.where(lo_q, qb, zero), jnp.where(lo_q, zero, qb)], axis=0)
            s = lax.dot_general(q4, k_dup[kvh][win], (((1,), (1,)), ((), ())),
                                preferred_element_type=F32) + bias4
            h0 = 4 * kvh
            sink = jnp.where(row < WINDOW, sink_ref[h0],
                             jnp.where(row < 2 * WINDOW, sink_ref[h0 + 1],
                                       jnp.where(row < 3 * WINDOW, sink_ref[h0 + 2], sink_ref[h0 + 3])))
            m = jnp.maximum(jnp.max(s, axis=-1, keepdims=True), sink)
            e = jnp.exp2(s - m)
            den = jnp.sum(e, axis=-1, keepdims=True) + jnp.exp2(sink - m)
            o = jnp.dot(e.astype(BF16), v_dup[kvh][win], preferred_element_type=F32) * (1.0 / den)
            o_ref[0, rows, c0 * LANES:(c0 + 1) * LANES] = jnp.where(
                lo_q, o[:WINDOW], o[WINDOW:2 * WINDOW]).astype(o_ref.dtype)
            o_ref[0, rows, (c0 + 1) * LANES:(c0 + 2) * LANES] = jnp.where(
                lo_q, o[2 * WINDOW:3 * WINDOW], o[3 * WINDOW:]).astype(o_ref.dtype)


def _swa_bias():
    qi = np.arange(WINDOW)[:, None]
    c = np.arange(3 * WINDOW)[None, :]
    band = np.abs(c - WINDOW - qi) <= WINDOW
    first = band & (c >= WINDOW)
    lastb = band & (c < 2 * WINDOW)
    tab = np.stack([first, band, lastb]).astype(np.float32)
    return jnp.asarray((1.0 - tab) * NEG_INF, F32)


def _swa(q, kv, sinks):
    nb, seq, _ = q.shape
    tq = min(TQ_ATTN, seq)
    nblk = tq // WINDOW
    nseq_blk = seq // WINDOW
    kvw = kv.shape[-1]
    assert nblk >= 2 and seq % tq == 0, "first/last window masks are applied to distinct query blocks"
    bias = _swa_bias()
    return pl.pallas_call(
        functools.partial(_swa_kernel, nblk=nblk),
        out_shape=jax.ShapeDtypeStruct((nb, seq, ATTN_WIDTH), BF16),
        grid=(nb, seq // tq),
        in_specs=[
            pl.BlockSpec(memory_space=pltpu.SMEM),
            pl.BlockSpec((1, tq, ATTN_WIDTH), lambda b, i: (b, i, 0)),
            pl.BlockSpec((1, tq, kvw), lambda b, i: (b, i, 0)),
            pl.BlockSpec((1, WINDOW, kvw), lambda b, i: (b, jnp.maximum(i * nblk - 1, 0), 0)),
            pl.BlockSpec((1, WINDOW, kvw), lambda b, i: (b, jnp.minimum((i + 1) * nblk, nseq_blk - 1), 0)),
            pl.BlockSpec(bias.shape, lambda b, i: (0, 0, 0)),
        ],
        out_specs=pl.BlockSpec((1, tq, ATTN_WIDTH), lambda b, i: (b, i, 0)),
        compiler_params=_cparams(("arbitrary", "arbitrary")),
        name="swa",
    )(sinks.astype(F32) * LOG2E, q, kv, kv, kv, bias)


_DFT_PAD_ROWS = 16


def _seqdft_kernel(ab_ref, ct_hbm, st_hbm, perm_ref, o_ref, ct, st, aebo, hbuf, sem, *, scale, tk):
    n = ab_ref.shape[1]
    m = n // 2
    fw = FOURIER_WIDTH
    nblk = m // LANES

    @pl.when(pl.program_id(0) == 0)
    def _():
        copies = [pltpu.make_async_copy(ct_hbm, ct, sem.at[0]), pltpu.make_async_copy(st_hbm, st, sem.at[1])]
        for c in copies:
            c.start()
        for c in copies:
            c.wait()

    perm = perm_ref[...]
    for blk in range(nblk):
        mirrored = [ab_ref[0, n - LANES * (blk + 1):n - LANES * blk, :]]
        if blk > 0:
            mirrored.append(ab_ref[0, n - LANES * blk:n - LANES * (blk - 1), :])
        else:
            mirrored.append(jnp.zeros((LANES, 2 * fw), BF16))
        r = jnp.dot(perm, jnp.concatenate(mirrored, axis=0), preferred_element_type=F32)
        cur = ab_ref[0, LANES * blk:LANES * (blk + 1), :].astype(F32)
        aebo[LANES * blk:LANES * (blk + 1), :] = jnp.concatenate(
            [cur[:, :fw] + r[:, :fw], cur[:, fw:] - r[:, fw:]], axis=1).astype(BF16)

    a_mid = ab_ref[0, m:m + _DFT_PAD_ROWS, :fw][0:1, :].astype(F32)

    def pq(rows, nrows):
        p = jnp.dot(ct[rows, :], aebo[:, :fw], preferred_element_type=F32)
        q = jnp.dot(st[rows, :], aebo[:, fw:], preferred_element_type=F32)
        odd = (lax.broadcasted_iota(I32, (nrows, fw), 0) & 1) == 1
        return p + jnp.where(odd, -a_mid, a_mid), q

    for kt in range(m // tk):
        rows = slice(kt * tk, (kt + 1) * tk)
        p, q = pq(rows, tk)
        o_ref[0, rows, :] = ((p - q) * scale).astype(o_ref.dtype)
        hbuf[rows, :] = p + q
    p, q = pq(slice(m, m + _DFT_PAD_ROWS), _DFT_PAD_ROWS)
    hbuf[m:m + _DFT_PAD_ROWS, :] = p + q
    hbuf[m + _DFT_PAD_ROWS:, :] = jnp.zeros((LANES - _DFT_PAD_ROWS, fw), F32)
    for c in range(nblk):
        win = hbuf[LANES * c:LANES * (c + 2), :].astype(BF16)
        blk_out = jnp.dot(perm, win, preferred_element_type=F32)
        o_ref[0, n - LANES * (c + 1):n - LANES * c, :] = (blk_out * scale).astype(o_ref.dtype)


def _seq_dft_tables(seq):
    m = seq // 2
    n2 = 64
    n1 = m // n2
    k = np.arange(m + _DFT_PAD_ROWS)[:, None]
    a1 = 2.0 * np.pi * ((k * n2 * np.arange(n1)[None, :]) % seq) / seq
    a2 = 2.0 * np.pi * ((k * np.arange(n2)[None, :]) % seq) / seq
    c1, s1 = jnp.asarray(np.cos(a1), F32), jnp.asarray(np.sin(a1), F32)
    c2, s2 = jnp.asarray(np.cos(a2), F32), jnp.asarray(np.sin(a2), F32)
    cs = c1[:, :, None] * c2[:, None, :] - s1[:, :, None] * s2[:, None, :]
    ss = s1[:, :, None] * c2[:, None, :] + c1[:, :, None] * s2[:, None, :]
    return cs.reshape(-1, m).astype(BF16), ss.reshape(-1, m).astype(BF16)


def _seqdft(ab):
    nb, seq, _ = ab.shape
    m = seq // 2
    tk = min(TR_DFT, m)
    assert m % LANES == 0 and m % tk == 0 and m % 64 == 0
    ct, st = _seq_dft_tables(seq)
    perm = np.zeros((LANES, 2 * LANES), np.float32)
    perm[np.arange(LANES), LANES - np.arange(LANES)] = 1.0
    scale = float(1.0 / np.sqrt(seq * FOURIER_GROUP))
    return pl.pallas_call(
        functools.partial(_seqdft_kernel, scale=scale, tk=tk),
        out_shape=jax.ShapeDtypeStruct((nb, seq, FOURIER_WIDTH), BF16),
        grid=(nb,),
        in_specs=[
            pl.BlockSpec((1, seq, 2 * FOURIER_WIDTH), lambda b: (b, 0, 0)),
            pl.BlockSpec(memory_space=pl.ANY),
            pl.BlockSpec(memory_space=pl.ANY),
            pl.BlockSpec(perm.shape, lambda b: (0, 0)),
        ],
        out_specs=pl.BlockSpec((1, seq, FOURIER_WIDTH), lambda b: (b, 0, 0)),
        scratch_shapes=[pltpu.VMEM(ct.shape, BF16), pltpu.VMEM(st.shape, BF16),
                        pltpu.VMEM((m, 2 * FOURIER_WIDTH), BF16),
                        pltpu.VMEM((m + LANES, FOURIER_WIDTH), F32),
                        pltpu.SemaphoreType.DMA((2,))],
        compiler_params=_cparams(("arbitrary",)),
        name="seqdft",
    )(ab, ct, st, jnp.asarray(perm, BF16))


def _mid_kernel(xp_ref, xs_ref, a_ref, f_ref, kv_ref, gao_ref, gfo_ref, wout_ref, gc_ref, wq_ref,
                wo_ref, gm_ref, wr_ref, br_ref, tri_ref,
                x2_ref, h3_ref, meta_ref, gate_ref, cnt_ref, cnt_sc, *, n_prompt):
    b = pl.program_id(0)
    i = pl.program_id(1)
    tm = a_ref.shape[1]

    @pl.when((b == 0) & (i == 0))
    def _():
        cnt_sc[...] = jnp.zeros_like(cnt_sc)

    def token_chain(r0, nr):
        rows = slice(r0, r0 + nr)
        x = jnp.where(b < n_prompt, xp_ref[0, rows, :], xs_ref[0, rows, :])
        an = _rms(a_ref[0, rows, :].astype(F32), gao_ref[...]).astype(BF16)
        fn = _rms(f_ref[0, rows, :].astype(F32), gfo_ref[...]).astype(BF16)
        x1 = (x + jnp.dot(an, wout_ref[:ATTN_WIDTH, :], preferred_element_type=F32)
              + jnp.dot(fn, wout_ref[ATTN_WIDTH:, :], preferred_element_type=F32))

        h2 = _rms(x1, gc_ref[...]).astype(BF16)
        qc = (jnp.dot(h2, wq_ref[...], preferred_element_type=F32)
              * (LOG2E * CROSS_HEAD_DIM ** -0.5)).astype(BF16)
        heads = []
        for hd in range(N_CROSS_HEADS):
            cols = slice(hd * CROSS_HEAD_DIM, (hd + 1) * CROSS_HEAD_DIM)
            vcols = slice(CROSS_WIDTH + hd * CROSS_HEAD_DIM, CROSS_WIDTH + (hd + 1) * CROSS_HEAD_DIM)
            s = lax.dot_general(qc[:, cols], kv_ref[0, :, cols], (((1,), (1,)), ((), ())),
                                preferred_element_type=F32)
            e = jnp.exp2(s - jnp.max(s, axis=-1, keepdims=True))
            den = jnp.sum(e, axis=-1, keepdims=True)
            heads.append(jnp.dot(e.astype(BF16), kv_ref[0, :, vcols], preferred_element_type=F32) * (1.0 / den))
        oc = jnp.concatenate(heads, axis=1).astype(BF16)
        x2 = x1 + jnp.dot(oc, wo_ref[...], preferred_element_type=F32)
        x2_ref[0, rows, :] = x2

        h3 = _rms(x2, gm_ref[...])
        for sl in range(SUBLANES):
            h3_ref[pl.ds(r0 * SUBLANES + sl, nr, stride=SUBLANES), :] = h3[:, sl * LANES:(sl + 1) * LANES]

        hi = h3.astype(BF16)
        lo = (h3 - hi.astype(F32)).astype(BF16)
        hw = jnp.dot(hi, wr_ref[...], preferred_element_type=F32)
        return (hw[:, :LANES] + hw[:, LANES:]
                + jnp.dot(lo, wr_ref[:, :LANES], preferred_element_type=F32) + br_ref[...])

    nr = tm // MID_CHAINS
    logits = jnp.concatenate([token_chain(c * nr, nr) for c in range(MID_CHAINS)], axis=0)
    lt = logits.T[:N_EXPERTS, :]
    eidx = lax.broadcasted_iota(I32, lt.shape, 0)
    vals, idxs, sels = [], [], []
    for _ in range(TOP_K):
        m = jnp.max(lt, axis=0, keepdims=True)
        idx = jnp.min(jnp.where(lt == m, eidx, N_EXPERTS), axis=0, keepdims=True)
        sel = eidx == idx
        lt = jnp.where(sel, -jnp.inf, lt)
        vals.append(m)
        idxs.append(idx)
        sels.append(sel)
    ex = [jnp.exp(vk - vals[0]) for vk in vals]
    inv = 1.0 / (ex[0] + ex[1] + ex[2] + ex[3])
    zrow = jnp.zeros_like(ex[0])
    gate_ref[...] = jnp.concatenate([e_ * inv for e_ in ex] + [zrow] * (SUBLANES - TOP_K), axis=0)

    onehot = jnp.zeros(lt.shape, F32)
    for sel in sels:
        onehot = onehot + sel.astype(F32)
    before = cnt_sc[:, 0:1] + jnp.dot(onehot.astype(BF16), tri_ref[...], preferred_element_type=F32)
    ranks = [jnp.sum(jnp.where(sel, before, 0.0), axis=0, keepdims=True).astype(I32) for sel in sels]
    meta_ref[...] = jnp.concatenate(idxs + ranks, axis=0)
    cnt_sc[...] = cnt_sc[...] + jnp.sum(onehot, axis=1, keepdims=True)
    cnt_ref[...] = cnt_sc[...]


def _mid(x_p, x_s, a, f, kvc, g_attn_out, g_four_out, w_out, g_cross, w_q, w_o, g_moe, w_router, b_router):
    n_prompt, seq, d = x_p.shape
    nb = a.shape[0]
    tm = min(TM_MID, seq)
    nt = seq // tm
    n_mem = kvc.shape[1]
    pb, sb = _two_group_maps(n_prompt)
    wr = jnp.zeros((d, LANES), F32).at[:, :N_EXPERTS].set(w_router)
    wr_hi = wr.astype(BF16)
    wr_lo = (wr - wr_hi.astype(F32)).astype(BF16)
    br = jnp.zeros((1, LANES), F32).at[0, :N_EXPERTS].set(b_router)
    tri = jnp.asarray(np.triu(np.ones((tm, tm), np.float32), 1), BF16)
    full = lambda arr: pl.BlockSpec(arr.shape, lambda b, i: (0,) * arr.ndim)
    row = lambda v: v.reshape(1, -1)
    args = [x_p, x_s, a, f, kvc, row(g_attn_out), row(g_four_out), w_out.astype(BF16), row(g_cross),
            w_q.astype(BF16), w_o.astype(BF16), row(g_moe), jnp.concatenate([wr_hi, wr_lo], axis=1), br, tri]
    in_specs = [
        pl.BlockSpec((1, tm, d), lambda b, i: (pb(b), i, 0)),
        pl.BlockSpec((1, tm, d), lambda b, i: (sb(b), i, 0)),
        pl.BlockSpec((1, tm, ATTN_WIDTH), lambda b, i: (b, i, 0)),
        pl.BlockSpec((1, tm, FOURIER_WIDTH), lambda b, i: (b, i, 0)),
        pl.BlockSpec((1, n_mem, 2 * CROSS_WIDTH), lambda b, i: (b, 0, 0)),
    ] + [full(v) for v in args[5:]]
    tok = nb * seq
    return pl.pallas_call(
        functools.partial(_mid_kernel, n_prompt=n_prompt),
        out_shape=(jax.ShapeDtypeStruct((nb, seq, d), F32),
                   jax.ShapeDtypeStruct((tok * SUBLANES, LANES), F32),
                   jax.ShapeDtypeStruct((SUBLANES, tok), I32),
                   jax.ShapeDtypeStruct((SUBLANES, tok), F32),
                   jax.ShapeDtypeStruct((N_EXPERTS, LANES), F32)),
        grid=(nb, nt),
        in_specs=in_specs,
        out_specs=(pl.BlockSpec((1, tm, d), lambda b, i: (b, i, 0)),
                   pl.BlockSpec((tm * SUBLANES, LANES), lambda b, i: (b * nt + i, 0)),
                   pl.BlockSpec((SUBLANES, tm), lambda b, i: (0, b * nt + i)),
                   pl.BlockSpec((SUBLANES, tm), lambda b, i: (0, b * nt + i)),
                   pl.BlockSpec((N_EXPERTS, LANES), lambda b, i: (0, 0))),
        scratch_shapes=[pltpu.VMEM((N_EXPERTS, LANES), F32)],
        compiler_params=_cparams(("arbitrary", "arbitrary")),
        name="mid",
    )(*args)


def _row_copy(src_rows, dst_rows, sem):
    return pltpu.make_async_copy(src_rows, dst_rows, sem)


def _dispatch_kernel(pos_ref, ztile_ref, h3_ref, xs_ref, zbuf, stage, fsem, rsem, zsem, *, tm, te):
    i = pl.program_id(0)
    last = pl.num_programs(0) - 1
    tile_rows = tm * SUBLANES

    def fetch(tile, slot):
        r0 = pl.multiple_of(tile * tile_rows, tile_rows)
        return pltpu.make_async_copy(h3_ref.at[pl.ds(r0, tile_rows), :], stage.at[slot], fsem.at[slot])

    @pl.when(i == 0)
    def _():
        fetch(0, 0).start()
        zbuf[...] = jnp.zeros_like(zbuf)
        for e in range(N_EXPERTS):
            r0 = pl.multiple_of(ztile_ref[e] * (te * SUBLANES), te * SUBLANES)
            _row_copy(zbuf, xs_ref.at[pl.ds(r0, te * SUBLANES), :], zsem).start()
        for e in range(N_EXPERTS):
            _row_copy(zbuf, xs_ref.at[pl.ds(0, te * SUBLANES), :], zsem).wait()

    @pl.when(i < last)
    def _():
        fetch(i + 1, (i + 1) % 3).start()

    slot = i % 3
    fetch(i, slot).wait()

    def issue(tb, carry):
        for u in range(ISSUE_UNROLL):
            t = tb * ISSUE_UNROLL + u
            src = stage.at[slot, pl.ds(pl.multiple_of(t * SUBLANES, SUBLANES), SUBLANES), :]
            for k in range(TOP_K):
                r0 = pl.multiple_of(pos_ref[0, 0, k * tm + t] * SUBLANES, SUBLANES)
                _row_copy(src, xs_ref.at[pl.ds(r0, SUBLANES), :], rsem.at[i % 2]).start(priority=k % 2)
        return carry

    lax.fori_loop(0, tm // ISSUE_UNROLL, issue, 0)

    def drain(s):
        for _ in range(TOP_K):
            _row_copy(stage.at[0], xs_ref.at[pl.ds(0, tile_rows), :], rsem.at[s]).wait()

    @pl.when(i > 0)
    def _():
        drain((i + 1) % 2)

    @pl.when(i == last)
    def _():
        drain(i % 2)


def _dispatch(h3, pos3, ztile, rows_pad, te):
    ntile, _, four_tm = pos3.shape
    tm = four_tm // TOP_K
    return pl.pallas_call(
        functools.partial(_dispatch_kernel, tm=tm, te=te),
        out_shape=jax.ShapeDtypeStruct((rows_pad * SUBLANES, LANES), F32),
        grid_spec=pltpu.PrefetchScalarGridSpec(
            num_scalar_prefetch=0,
            grid=(ntile,),
            in_specs=[
                pl.BlockSpec((1, 1, four_tm), lambda i: (i, 0, 0), memory_space=pltpu.SMEM),
                pl.BlockSpec(memory_space=pltpu.SMEM),
                pl.BlockSpec(memory_space=pl.ANY),
            ],
            out_specs=pl.BlockSpec(memory_space=pl.ANY),
            scratch_shapes=[pltpu.VMEM((te * SUBLANES, LANES), F32),
                            pltpu.VMEM((3, tm * SUBLANES, LANES), F32),
                            pltpu.SemaphoreType.DMA((3,)), pltpu.SemaphoreType.DMA((2,)),
                            pltpu.SemaphoreType.DMA],
        ),
        compiler_params=_cparams(("arbitrary",)),
        name="dispatch",
    )(pos3, ztile, h3)


def _expert_kernel(te_ref, tb_ref, nu_ref, xs_ref, wgu_ref, bgu_ref, wd_ref, bd_ref, ys_ref, wgu_bf, wd_bf):
    i = pl.program_id(0)
    te = xs_ref.shape[0] // SUBLANES
    dff = wd_ref.shape[1]

    @pl.when((i == 0) | (te_ref[i] != te_ref[jnp.maximum(i - 1, 0)]))
    def _():
        wgu_bf[...] = wgu_ref[0].astype(BF16)
        wd_bf[...] = wd_ref[0].astype(BF16)

    @pl.when(i < nu_ref[0])
    def _():
        x = jnp.concatenate([xs_ref[pl.ds(sl, te, stride=SUBLANES), :] for sl in range(SUBLANES)],
                            axis=1).astype(BF16)
        gu = jnp.dot(x, wgu_bf[...], preferred_element_type=F32) + bgu_ref[0]
        gate = jnp.minimum(gu[:, :dff], SWIGLU_LIMIT)
        up = jnp.clip(gu[:, dff:], -SWIGLU_LIMIT, SWIGLU_LIMIT)
        act = (up + 1.0) * (gate * (1.0 / (1.0 + jnp.exp(-SWIGLU_ALPHA * gate))))
        y = jnp.dot(act.astype(BF16), wd_bf[...], preferred_element_type=F32) + bd_ref[0]
        for sl in range(SUBLANES):
            ys_ref[pl.ds(sl, te, stride=SUBLANES), :] = y[:, sl * LANES:(sl + 1) * LANES]


def _experts(xs, tile_expert, tile_block, n_used, w_gate_up, b_gate_up, w_down, b_down, te):
    ne, d, two_dff = w_gate_up.shape
    dff = two_dff // 2
    ntile = tile_expert.shape[0]
    return pl.pallas_call(
        _expert_kernel,
        out_shape=jax.ShapeDtypeStruct(xs.shape, F32),
        grid_spec=pltpu.PrefetchScalarGridSpec(
            num_scalar_prefetch=3,
            grid=(ntile,),
            in_specs=[
                pl.BlockSpec((te * SUBLANES, LANES), lambda i, te_r, tb_r, nu_r: (tb_r[i], 0)),
                pl.BlockSpec((1, d, two_dff), lambda i, te_r, tb_r, nu_r: (te_r[i], 0, 0)),
                pl.BlockSpec((1, 1, two_dff), lambda i, te_r, tb_r, nu_r: (te_r[i], 0, 0)),
                pl.BlockSpec((1, dff, d), lambda i, te_r, tb_r, nu_r: (te_r[i], 0, 0)),
                pl.BlockSpec((1, 1, d), lambda i, te_r, tb_r, nu_r: (te_r[i], 0, 0)),
            ],
            out_specs=pl.BlockSpec((te * SUBLANES, LANES), lambda i, te_r, tb_r, nu_r: (tb_r[i], 0)),
            scratch_shapes=[pltpu.VMEM((d, two_dff), BF16), pltpu.VMEM((dff, d), BF16)],
        ),
        compiler_params=_cparams(("arbitrary",)),
        name="experts",
    )(tile_expert, tile_block, n_used, xs, w_gate_up, b_gate_up.reshape(ne, 1, two_dff),
      w_down, b_down.reshape(ne, 1, d))


def _combine_kernel(pos_a, pos_b, pos_n, x2_ref, gate_ref, g_ref, ys_ref, o_ref, buf_a, buf_b, sem, *, tm):
    step = pl.program_id(0) * pl.num_programs(1) + pl.program_id(1)
    nsteps = pl.num_programs(0) * pl.num_programs(1)
    slot_rows = TOP_K * tm * SUBLANES

    def wait_rows(buf, s):
        _row_copy(ys_ref.at[pl.ds(0, slot_rows), :], buf, sem.at[s]).wait()

    @pl.when(step == 0)
    def _():
        def issue(tb, carry):
            for u in range(ISSUE_UNROLL):
                t = tb * ISSUE_UNROLL + u
                for k in range(TOP_K):
                    r0 = pl.multiple_of(pos_a[0, 0, k * tm + t] * SUBLANES, SUBLANES)
                    d0 = pl.multiple_of((k * tm + t) * SUBLANES, SUBLANES)
                    _row_copy(ys_ref.at[pl.ds(r0, SUBLANES), :], buf_a.at[pl.ds(d0, SUBLANES), :],
                              sem.at[0]).start(priority=k % 2)
            return carry

        lax.fori_loop(0, tm // ISSUE_UNROLL, issue, 0)

    def combine_tile(rows, buf, req_pos, req_buf, req_sem):
        x = x2_ref[0, rows, :]
        g = gate_ref[rows, :]
        for k in range(TOP_K):
            for t in range(tm):
                r0 = pl.multiple_of(req_pos[0, 0, k * tm + t] * SUBLANES, SUBLANES)
                _row_copy(ys_ref.at[pl.ds(r0, SUBLANES), :],
                          req_buf.at[pl.ds((k * tm + t) * SUBLANES, SUBLANES), :],
                          sem.at[req_sem]).start(priority=t % 2)
            yk = jnp.concatenate(
                [buf[pl.ds(k * tm * SUBLANES + sl, tm, stride=SUBLANES), :] for sl in range(SUBLANES)], axis=1)
            x = x + g[:, k:k + 1] * yk
        o_ref[0, rows, :] = _rms(x, g_ref[...])

    wait_rows(buf_a, 0)
    combine_tile(slice(0, tm), buf_a, pos_b, buf_b, 1)
    wait_rows(buf_b, 1)
    combine_tile(slice(tm, 2 * tm), buf_b, pos_n, buf_a, 0)

    @pl.when(step == nsteps - 1)
    def _():
        wait_rows(buf_a, 0)


def _combine(x2, gate_t, pos3, ys, g_final, b0, nbatch):
    _, seq, d = x2.shape
    tm = pos3.shape[2] // TOP_K
    nt = seq // (2 * tm)
    assert seq % (2 * tm) == 0
    last = 2 * (b0 + nbatch) * nt - 2
    tile_a = lambda b, i: 2 * ((b0 + b) * nt + i)
    smem_tile = lambda fn: pl.BlockSpec((1, 1, TOP_K * tm), lambda b, i: (fn(b, i), 0, 0),
                                        memory_space=pltpu.SMEM)
    return pl.pallas_call(
        functools.partial(_combine_kernel, tm=tm),
        out_shape=jax.ShapeDtypeStruct((nbatch, seq, d), F32),
        grid_spec=pltpu.PrefetchScalarGridSpec(
            num_scalar_prefetch=0,
            grid=(nbatch, nt),
            in_specs=[
                smem_tile(tile_a),
                smem_tile(lambda b, i: tile_a(b, i) + 1),
                smem_tile(lambda b, i: jnp.minimum(tile_a(b, i) + 2, last)),
                pl.BlockSpec((1, 2 * tm, d), lambda b, i: (b0 + b, i, 0)),
                pl.BlockSpec((2 * tm, SUBLANES), lambda b, i: ((b0 + b) * nt + i, 0)),
                pl.BlockSpec((1, d), lambda b, i: (0, 0)),
                pl.BlockSpec(memory_space=pl.ANY),
            ],
            out_specs=pl.BlockSpec((1, 2 * tm, d), lambda b, i: (b, i, 0)),
            scratch_shapes=[pltpu.VMEM((TOP_K * tm * SUBLANES, LANES), F32),
                            pltpu.VMEM((TOP_K * tm * SUBLANES, LANES), F32),
                            pltpu.SemaphoreType.DMA((2,))],
        ),
        compiler_params=_cparams(("arbitrary", "arbitrary")),
        name="combine",
    )(pos3, pos3, pos3, x2, gate_t, g_final.reshape(1, d), ys)


def _routing_plan(meta, cnt, te, tm_rows):
    tok = meta.shape[1]
    idx, rank = meta[:TOP_K], meta[TOP_K:]
    counts = cnt[:, 0].astype(I32)
    tiles = (counts + te - 1) // te
    tile_end = jnp.cumsum(tiles)
    tile_start = tile_end - tiles
    offsets = tile_start * te
    n_used = tile_end[-1]
    pos = rank + jnp.sum(jnp.where(idx[None] == jnp.arange(N_EXPERTS, dtype=I32)[:, None, None],
                                   offsets[:, None, None], 0), axis=0)
    ntile_max = (tok * TOP_K) // te + N_EXPERTS
    t_ids = jnp.minimum(jnp.arange(ntile_max, dtype=I32), n_used - 1)
    tile_expert = jnp.sum((t_ids[:, None] >= tile_end[None, :]).astype(I32), axis=1)
    ztile = jnp.where(tiles > 0, tile_end - 1, 0).astype(I32)
    ntok_tiles = tok // tm_rows
    pos3 = pos.reshape(TOP_K, ntok_tiles, tm_rows).transpose(1, 0, 2).reshape(ntok_tiles, 1, TOP_K * tm_rows)
    return pos3, tile_expert, t_ids, n_used.reshape(1), ztile, ntile_max * te


def kernel(x_prompt, x_sample, mem_prompt, mem_sample, g_mix, w_in, sinks, g_attn_out, g_four_out, w_out,
           g_cross, g_mem, w_q_cross, w_kv_cross, w_o_cross, g_moe, w_router, b_router, w_gate_up,
           b_gate_up, w_down, b_down, g_final):
    n_prompt, seq, _ = x_prompt.shape
    n_sample = x_sample.shape[0]
    l = 0
    kvc = _memkv(mem_prompt, mem_sample, g_mem[l], w_kv_cross[l])
    q, kv, ab = _inproj(x_prompt, x_sample, g_mix[l], w_in[l])
    a = _swa(q, kv, sinks[l])
    f = _seqdft(ab)
    x2, h3, meta, gates, cnt = _mid(x_prompt, x_sample, a, f, kvc, g_attn_out[l], g_four_out[l], w_out[l],
                                    g_cross[l], w_q_cross[l], w_o_cross[l], g_moe[l], w_router[l], b_router[l])
    tm_rows = min(TM_ROWS, seq // 2)
    te = TM_EXPERT
    pos3, tile_expert, tile_block, n_used, ztile, rows_pad = _routing_plan(meta, cnt, te, tm_rows)
    xs = _dispatch(h3, pos3, ztile, rows_pad, te)
    ys = _experts(xs, tile_expert, tile_block, n_used, w_gate_up[l], b_gate_up[l], w_down[l], b_down[l], te)
    gate_t = gates.T
    y_p = _combine(x2, gate_t, pos3, ys, g_final, 0, n_prompt)
    y_s = _combine(x2, gate_t, pos3, ys, g_final, n_prompt, n_sample)
    return (y_p, y_s)
```

```python
import functools

import numpy as np
import jax
import jax.numpy as jnp
from jax import lax
from jax.experimental import pallas as pl
from jax.experimental.pallas import tpu as pltpu

F32 = jnp.float32
BF16 = jnp.bfloat16
I32 = jnp.int32

HEAD_DIM = 64
N_Q_HEADS = 8
N_KV_HEADS = 2
ATTN_WIDTH = N_Q_HEADS * HEAD_DIM
KV_WIDTH = N_KV_HEADS * HEAD_DIM
FOURIER_WIDTH = 512
FOURIER_GROUP = 64
WINDOW = 128
ROPE_DIM = 16
ROPE_THETA = 500000.0
N_CROSS_HEADS = 4
CROSS_HEAD_DIM = 128
CROSS_WIDTH = N_CROSS_HEADS * CROSS_HEAD_DIM
N_EXPERTS = 32
TOP_K = 4
SWIGLU_LIMIT = 7.0
SWIGLU_ALPHA = 1.702
EPS = 1e-5
NEG_INF = -1e30
LOG2E = 1.4426950408889634

LANES = 128
SUBLANES = 8
VMEM_LIMIT_BYTES = 56 * 1024 * 1024

TS_INPROJ = 512
TQ_ATTN = 512
TR_DFT = 512
TM_MID = 512
TM_ROWS = 512
TM_EXPERT = 512
ISSUE_UNROLL = 4
MID_CHAINS = 1


def _cparams(sem):
    return pltpu.CompilerParams(dimension_semantics=sem, vmem_limit_bytes=VMEM_LIMIT_BYTES)


def _rms(x, g):
    return x * lax.rsqrt(jnp.mean(x * x, axis=-1, keepdims=True) + EPS) * g


def _row_copy(src_rows, dst_rows, sem):
    return pltpu.make_async_copy(src_rows, dst_rows, sem)


def _two_group_maps(n_prompt):
    def prompt_b(b):
        return jnp.minimum(b, n_prompt - 1)

    def sample_b(b):
        return jnp.maximum(b - n_prompt, 0)
    return prompt_b, sample_b


def _memkv_kernel(mp_ref, ms_ref, g_ref, w_ref, o_ref, *, n_prompt):
    b = pl.program_id(0)
    m = jnp.where(b < n_prompt, mp_ref[0], ms_ref[0])
    mn = _rms(m, g_ref[...]).astype(BF16)
    o_ref[0] = jnp.dot(mn, w_ref[...], preferred_element_type=F32).astype(BF16)


def _memkv(mem_p, mem_s, g_mem, w_kv):
    n_prompt, n_mem, d = mem_p.shape
    nb = n_prompt + mem_s.shape[0]
    pb, sb = _two_group_maps(n_prompt)
    return pl.pallas_call(
        functools.partial(_memkv_kernel, n_prompt=n_prompt),
        out_shape=jax.ShapeDtypeStruct((nb, n_mem, w_kv.shape[1]), BF16),
        grid=(nb,),
        in_specs=[
            pl.BlockSpec((1, n_mem, d), lambda b: (pb(b), 0, 0)),
            pl.BlockSpec((1, n_mem, d), lambda b: (sb(b), 0, 0)),
            pl.BlockSpec((1, d), lambda b: (0, 0)),
            pl.BlockSpec(w_kv.shape, lambda b: (0, 0)),
        ],
        out_specs=pl.BlockSpec((1, n_mem, w_kv.shape[1]), lambda b: (b, 0, 0)),
        compiler_params=_cparams(("arbitrary",)),
        name="memkv",
    )(mem_p, mem_s, g_mem.reshape(1, d), w_kv.astype(BF16))


_ROT_W = ATTN_WIDTH + 2 * KV_WIDTH


def _inproj_kernel(xp_ref, xs_ref, g_ref, w_ref, cos_ref, sin_ref, bd_ref,
                   q_ref, kv_ref, ab_ref, *, n_prompt):
    b = pl.program_id(1)
    x = jnp.where(b < n_prompt, xp_ref[0], xs_ref[0])
    h = _rms(x, g_ref[...]).astype(BF16)
    z = jnp.dot(h, w_ref[...], preferred_element_type=F32)
    cos = cos_ref[...]
    sin = sin_ref[...]
    lane = lax.broadcasted_iota(I32, cos.shape, 1) & (HEAD_DIM - 1)
    first_half = lane < ROPE_DIM // 2
    rot = []
    for c in range(_ROT_W // LANES):
        zc = z[:, c * LANES:(c + 1) * LANES]
        partner = jnp.where(first_half,
                            pltpu.roll(zc, LANES - ROPE_DIM // 2, axis=1),
                            pltpu.roll(zc, ROPE_DIM // 2, axis=1))
        rot.append(zc * cos + partner * sin)
    nq = ATTN_WIDTH // LANES
    q_ref[0] = (jnp.concatenate(rot[:nq], axis=1) * (LOG2E * HEAD_DIM ** -0.5)).astype(BF16)
    v_off = _ROT_W
    u_off = _ROT_W + 2 * KV_WIDTH
    kv_ref[0] = jnp.concatenate(rot[nq:] + [z[:, v_off:u_off]], axis=1).astype(BF16)
    half = FOURIER_WIDTH // 2
    u = z[:, u_off:].astype(BF16)
    r0 = jnp.dot(u[:, :half], bd_ref[...], preferred_element_type=F32)
    r1 = jnp.dot(u[:, half:], bd_ref[...], preferred_element_type=F32)
    ab_ref[0] = jnp.concatenate([r0[:, :half], r1[:, :half], r0[:, half:], r1[:, half:]],
                                axis=1).astype(BF16)


def _rope_tables(seq):
    half = ROPE_DIM // 2
    inv_freq = ROPE_THETA ** (-(jnp.arange(half, dtype=F32) * 2.0) / ROPE_DIM)
    ang = jnp.arange(seq).astype(F32)[:, None] * inv_freq[None, :]
    cos, sin = jnp.cos(ang), jnp.sin(ang)
    rest = HEAD_DIM - ROPE_DIM
    cos_h = jnp.concatenate([cos, cos, jnp.ones((seq, rest), F32)], axis=1)
    sin_h = jnp.concatenate([-sin, sin, jnp.zeros((seq, rest), F32)], axis=1)
    reps = LANES // HEAD_DIM
    return jnp.tile(cos_h, (1, reps)), jnp.tile(sin_h, (1, reps))


def _channel_dft_tables():
    n = FOURIER_GROUP
    idx = np.arange(n)
    ang = 2.0 * np.pi * ((idx[:, None] * idx[None, :]) % n) / n
    eye = np.eye(FOURIER_WIDTH // (2 * n))
    return jnp.asarray(np.concatenate([np.kron(eye, np.cos(ang)), np.kron(eye, np.sin(ang))], axis=1), BF16)


def _inproj(x_p, x_s, g_mix, w_in):
    n_prompt, seq, d = x_p.shape
    nb = n_prompt + x_s.shape[0]
    ts = min(TS_INPROJ, seq)
    o1, o2, o3 = ATTN_WIDTH, ATTN_WIDTH + KV_WIDTH, ATTN_WIDTH + 2 * KV_WIDTH
    swap = lambda w: jnp.concatenate([w[:, HEAD_DIM:], w[:, :HEAD_DIM]], axis=1)
    wk, wv = w_in[:, o1:o2], w_in[:, o2:o3]
    w_ext = jnp.concatenate([w_in[:, :o1], wk, swap(wk), wv, swap(wv), w_in[:, o3:]], axis=1).astype(BF16)
    cos_t, sin_t = _rope_tables(seq)
    bd = _channel_dft_tables()
    pb, sb = _two_group_maps(n_prompt)
    kvw = 4 * KV_WIDTH
    return pl.pallas_call(
        functools.partial(_inproj_kernel, n_prompt=n_prompt),
        out_shape=(jax.ShapeDtypeStruct((nb, seq, ATTN_WIDTH), BF16),
                   jax.ShapeDtypeStruct((nb, seq, kvw), BF16),
                   jax.ShapeDtypeStruct((nb, seq, 2 * FOURIER_WIDTH), BF16)),
        grid=(seq // ts, nb),
        in_specs=[
            pl.BlockSpec((1, ts, d), lambda i, b: (pb(b), i, 0)),
            pl.BlockSpec((1, ts, d), lambda i, b: (sb(b), i, 0)),
            pl.BlockSpec((1, d), lambda i, b: (0, 0)),
            pl.BlockSpec(w_ext.shape, lambda i, b: (0, 0)),
            pl.BlockSpec((ts, LANES), lambda i, b: (i, 0)),
            pl.BlockSpec((ts, LANES), lambda i, b: (i, 0)),
            pl.BlockSpec(bd.shape, lambda i, b: (0, 0)),
        ],
        out_specs=(pl.BlockSpec((1, ts, ATTN_WIDTH), lambda i, b: (b, i, 0)),
                   pl.BlockSpec((1, ts, kvw), lambda i, b: (b, i, 0)),
                   pl.BlockSpec((1, ts, 2 * FOURIER_WIDTH), lambda i, b: (b, i, 0))),
        compiler_params=_cparams(("arbitrary", "arbitrary")),
        name="inproj",
    )(x_p, x_s, g_mix.reshape(1, d), w_ext, cos_t, sin_t, bd)


def _swa_kernel(sink_ref, q_ref, kvm_ref, kvp_ref, kvn_ref, bias_ref, o_ref, *, nblk):
    i = pl.program_id(1)
    last = pl.num_programs(1) - 1
    kvw = jnp.concatenate([kvp_ref[0], kvm_ref[0], kvn_ref[0]], axis=0)
    k, ksw, v, vsw = [kvw[:, c * LANES:(c + 1) * LANES] for c in range(4)]
    lo_kv = lax.broadcasted_iota(I32, k.shape, 1) < HEAD_DIM
    k_dup = (jnp.where(lo_kv, k, ksw), jnp.where(lo_kv, ksw, k))
    v_dup = (jnp.where(lo_kv, v, vsw), jnp.where(lo_kv, vsw, v))
    lo_q = lax.broadcasted_iota(I32, (WINDOW, LANES), 1) < HEAD_DIM
    row = lax.broadcasted_iota(I32, (4 * WINDOW, 1), 0)
    for j in range(nblk):
        bias = bias_ref[1]
        if j == 0:
            bias = jnp.where(i == 0, bias_ref[0], bias)
        if j == nblk - 1:
            bias = jnp.where(i == last, bias_ref[2], bias)
        bias4 = jnp.concatenate([bias] * 4, axis=0)
        rows = slice(j * WINDOW, (j + 1) * WINDOW)
        win = slice(j * WINDOW, (j + 3) * WINDOW)
        for kvh in range(N_KV_HEADS):
            c0 = 2 * kvh
            qa = q_ref[0, rows, c0 * LANES:(c0 + 1) * LANES]
            qb = q_ref[0, rows, (c0 + 1) * LANES:(c0 + 2) * LANES]
            zero = jnp.zeros_like(qa)
            q4 = jnp.concatenate([jnp.where(lo_q, qa, zero), jnp.where(lo_q, zero, qa),
                                  jnp.where(lo_q, qb, zero), jnp.where(lo_q, zero, qb)], axis=0)
            s = lax.dot_general(q4, k_dup[kvh][win], (((1,), (1,)), ((), ())),
                                preferred_element_type=F32) + bias4
            h0 = 4 * kvh
            sink = jnp.where(row < WINDOW, sink_ref[h0],
                             jnp.where(row < 2 * WINDOW, sink_ref[h0 + 1],
                                       jnp.where(row < 3 * WINDOW, sink_ref[h0 + 2], sink_ref[h0 + 3])))
            m = jnp.maximum(jnp.max(s, axis=-1, keepdims=True), sink)
            e = jnp.exp2(s - m)
            den = jnp.sum(e, axis=-1, keepdims=True) + jnp.exp2(sink - m)
            o = jnp.dot(e.astype(BF16), v_dup[kvh][win], preferred_element_type=F32) * (1.0 / den)
            o_ref[0, rows, c0 * LANES:(c0 + 1) * LANES] = jnp.where(
                lo_q, o[:WINDOW], o[WINDOW:2 * WINDOW]).astype(o_ref.dtype)
            o_ref[0, rows, (c0 + 1) * LANES:(c0 + 2) * LANES] = jnp.where(
                lo_q, o[2 * WINDOW:3 * WINDOW], o[3 * WINDOW:]).astype(o_ref.dtype)


def _swa_bias():
    qi = np.arange(WINDOW)[:, None]
    c = np.arange(3 * WINDOW)[None, :]
    band = np.abs(c - WINDOW - qi) <= WINDOW
    first = band & (c >= WINDOW)
    lastb = band & (c < 2 * WINDOW)
    tab = np.stack([first, band, lastb]).astype(np.float32)
    return jnp.asarray((1.0 - tab) * NEG_INF, F32)


def _swa(q, kv, sinks):
    nb, seq, _ = q.shape
    tq = min(TQ_ATTN, seq)
    nblk = tq // WINDOW
    nseq_blk = seq // WINDOW
    kvw = kv.shape[-1]
    assert nblk >= 2 and seq % tq == 0, "first/last window masks are applied to distinct query blocks"
    bias = _swa_bias()
    return pl.pallas_call(
        functools.partial(_swa_kernel, nblk=nblk),
        out_shape=jax.ShapeDtypeStruct((nb, seq, ATTN_WIDTH), BF16),
        grid=(nb, seq // tq),
        in_specs=[
            pl.BlockSpec(memory_space=pltpu.SMEM),
            pl.BlockSpec((1, tq, ATTN_WIDTH), lambda b, i: (b, i, 0)),
            pl.BlockSpec((1, tq, kvw), lambda b, i: (b, i, 0)),
            pl.BlockSpec((1, WINDOW, kvw), lambda b, i: (b, jnp.maximum(i * nblk - 1, 0), 0)),
            pl.BlockSpec((1, WINDOW, kvw), lambda b, i: (b, jnp.minimum((i + 1) * nblk, nseq_blk - 1), 0)),
            pl.BlockSpec(bias.shape, lambda b, i: (0, 0, 0)),
        ],
        out_specs=pl.BlockSpec((1, tq, ATTN_WIDTH), lambda b, i: (b, i, 0)),
        compiler_params=_cparams(("arbitrary", "arbitrary")),
        name="swa",
    )(sinks.astype(F32) * LOG2E, q, kv, kv, kv, bias)


_DFT_PAD_ROWS = 16


def _seqdft_kernel(ab_ref, ct_hbm, st_hbm, perm_ref, o_ref, ct, st, aebo, hbuf, sem, *, scale, tk):
    n = ab_ref.shape[1]
    m = n // 2
    fw = FOURIER_WIDTH
    nblk = m // LANES

    @pl.when(pl.program_id(0) == 0)
    def _():
        copies = [pltpu.make_async_copy(ct_hbm, ct, sem.at[0]), pltpu.make_async_copy(st_hbm, st, sem.at[1])]
        for c in copies:
            c.start()
        for c in copies:
            c.wait()

    perm = perm_ref[...]
    for blk in range(nblk):
        mirrored = [ab_ref[0, n - LANES * (blk + 1):n - LANES * blk, :]]
        if blk > 0:
            mirrored.append(ab_ref[0, n - LANES * blk:n - LANES * (blk - 1), :])
        else:
            mirrored.append(jnp.zeros((LANES, 2 * fw), BF16))
        r = jnp.dot(perm, jnp.concatenate(mirrored, axis=0), preferred_element_type=F32)
        cur = ab_ref[0, LANES * blk:LANES * (blk + 1), :].astype(F32)
        aebo[LANES * blk:LANES * (blk + 1), :] = jnp.concatenate(
            [cur[:, :fw] + r[:, :fw], cur[:, fw:] - r[:, fw:]], axis=1).astype(BF16)

    a_mid = ab_ref[0, m:m + _DFT_PAD_ROWS, :fw][0:1, :].astype(F32)

    def pq(rows, nrows):
        p = jnp.dot(ct[rows, :], aebo[:, :fw], preferred_element_type=F32)
        q = jnp.dot(st[rows, :], aebo[:, fw:], preferred_element_type=F32)
        odd = (lax.broadcasted_iota(I32, (nrows, fw), 0) & 1) == 1
        return p + jnp.where(odd, -a_mid, a_mid), q

    for kt in range(m // tk):
        rows = slice(kt * tk, (kt + 1) * tk)
        p, q = pq(rows, tk)
        o_ref[0, rows, :] = ((p - q) * scale).astype(o_ref.dtype)
        hbuf[rows, :] = p + q
    p, q = pq(slice(m, m + _DFT_PAD_ROWS), _DFT_PAD_ROWS)
    hbuf[m:m + _DFT_PAD_ROWS, :] = p + q
    hbuf[m + _DFT_PAD_ROWS:, :] = jnp.zeros((LANES - _DFT_PAD_ROWS, fw), F32)
    for c in range(nblk):
        win = hbuf[LANES * c:LANES * (c + 2), :].astype(BF16)
        blk_out = jnp.dot(perm, win, preferred_element_type=F32)
        o_ref[0, n - LANES * (c + 1):n - LANES * c, :] = (blk_out * scale).astype(o_ref.dtype)


def _seq_dft_tables(seq):
    m = seq // 2
    n2 = 64
    n1 = m // n2
    k = np.arange(m + _DFT_PAD_ROWS)[:, None]
    a1 = 2.0 * np.pi * ((k * n2 * np.arange(n1)[None, :]) % seq) / seq
    a2 = 2.0 * np.pi * ((k * np.arange(n2)[None, :]) % seq) / seq
    c1, s1 = jnp.asarray(np.cos(a1), F32), jnp.asarray(np.sin(a1), F32)
    c2, s2 = jnp.asarray(np.cos(a2), F32), jnp.asarray(np.sin(a2), F32)
    cs = c1[:, :, None] * c2[:, None, :] - s1[:, :, None] * s2[:, None, :]
    ss = s1[:, :, None] * c2[:, None, :] + c1[:, :, None] * s2[:, None, :]
    return cs.reshape(-1, m).astype(BF16), ss.reshape(-1, m).astype(BF16)


def _seqdft(ab):
    nb, seq, _ = ab.shape
    m = seq // 2
    tk = min(TR_DFT, m)
    assert m % LANES == 0 and m % tk == 0 and m % 64 == 0
    ct, st = _seq_dft_tables(seq)
    perm = np.zeros((LANES, 2 * LANES), np.float32)
    perm[np.arange(LANES), LANES - np.arange(LANES)] = 1.0
    scale = float(1.0 / np.sqrt(seq * FOURIER_GROUP))
    return pl.pallas_call(
        functools.partial(_seqdft_kernel, scale=scale, tk=tk),
        out_shape=jax.ShapeDtypeStruct((nb, seq, FOURIER_WIDTH), BF16),
        grid=(nb,),
        in_specs=[
            pl.BlockSpec((1, seq, 2 * FOURIER_WIDTH), lambda b: (b, 0, 0)),
            pl.BlockSpec(memory_space=pl.ANY),
            pl.BlockSpec(memory_space=pl.ANY),
            pl.BlockSpec(perm.shape, lambda b: (0, 0)),
        ],
        out_specs=pl.BlockSpec((1, seq, FOURIER_WIDTH), lambda b: (b, 0, 0)),
        scratch_shapes=[pltpu.VMEM(ct.shape, BF16), pltpu.VMEM(st.shape, BF16),
                        pltpu.VMEM((m, 2 * FOURIER_WIDTH), BF16),
                        pltpu.VMEM((m + LANES, FOURIER_WIDTH), F32),
                        pltpu.SemaphoreType.DMA((2,))],
        compiler_params=_cparams(("arbitrary",)),
        name="seqdft",
    )(ab, ct, st, jnp.asarray(perm, BF16))


def _mid_kernel(xp_ref, xs_ref, a_ref, f_ref, kv_ref, gao_ref, gfo_ref, wout_ref, gc_ref, wq_ref,
                wo_ref, gm_ref, wr_ref, br_ref, tri_ref,
                x2_ref, meta_ref, gate_ref, cnt_ref, slots_ref,
                cnt_sc, stage, posv, pos_sm, cntv, cnt_sm, zbuf, rsem, psem, zsem, *, n_prompt, cap, te):
    b = pl.program_id(0)
    i = pl.program_id(1)
    tm = a_ref.shape[1]
    step = b * pl.num_programs(1) + i
    nsteps = pl.num_programs(0) * pl.num_programs(1)
    tile_rows = tm * SUBLANES
    dump_base = N_EXPERTS * cap

    def wait_scatter(sem_idx):
        for _ in range(TOP_K):
            _row_copy(stage.at[0], slots_ref.at[pl.ds(0, tile_rows), :], rsem.at[sem_idx]).wait()

    @pl.when(step == 0)
    def _():
        cnt_sc[...] = jnp.zeros_like(cnt_sc)
        stage[2] = jnp.zeros((tile_rows, LANES), F32)

        def init(t, carry):
            for k in range(TOP_K):
                pos_sm[1, k, t] = dump_base + k * tm + t
            return carry

        lax.fori_loop(0, tm, init, 0)

    @pl.when(step >= 2)
    def _():
        wait_scatter(step % 2)

    prev_stage = (step + 2) % 3
    prev_pos = (step + 1) % 2
    for t in range(tm):
        for k in range(TOP_K):
            r0 = pl.multiple_of(pos_sm[prev_pos, k, t] * SUBLANES, SUBLANES)
            _row_copy(stage.at[prev_stage, pl.ds(t * SUBLANES, SUBLANES), :],
                      slots_ref.at[pl.ds(r0, SUBLANES), :], rsem.at[step % 2]).start(priority=k % 2)
    cur_stage = step % 3

    def token_chain(r0, nr):
        rows = slice(r0, r0 + nr)
        x = jnp.where(b < n_prompt, xp_ref[0, rows, :], xs_ref[0, rows, :])
        an = _rms(a_ref[0, rows, :].astype(F32), gao_ref[...]).astype(BF16)
        fn = _rms(f_ref[0, rows, :].astype(F32), gfo_ref[...]).astype(BF16)
        x1 = (x + jnp.dot(an, wout_ref[:ATTN_WIDTH, :], preferred_element_type=F32)
              + jnp.dot(fn, wout_ref[ATTN_WIDTH:, :], preferred_element_type=F32))

        h2 = _rms(x1, gc_ref[...]).astype(BF16)
        qc = (jnp.dot(h2, wq_ref[...], preferred_element_type=F32)
              * (LOG2E * CROSS_HEAD_DIM ** -0.5)).astype(BF16)
        heads = []
        for hd in range(N_CROSS_HEADS):
            cols = slice(hd * CROSS_HEAD_DIM, (hd + 1) * CROSS_HEAD_DIM)
            vcols = slice(CROSS_WIDTH + hd * CROSS_HEAD_DIM, CROSS_WIDTH + (hd + 1) * CROSS_HEAD_DIM)
            s = lax.dot_general(qc[:, cols], kv_ref[0, :, cols], (((1,), (1,)), ((), ())),
                                preferred_element_type=F32)
            e = jnp.exp2(s - jnp.max(s, axis=-1, keepdims=True))
            den = jnp.sum(e, axis=-1, keepdims=True)
            heads.append(jnp.dot(e.astype(BF16), kv_ref[0, :, vcols], preferred_element_type=F32) * (1.0 / den))
        oc = jnp.concatenate(heads, axis=1).astype(BF16)
        x2 = x1 + jnp.dot(oc, wo_ref[...], preferred_element_type=F32)
        x2_ref[0, rows, :] = x2

        h3 = _rms(x2, gm_ref[...])
        for sl in range(SUBLANES):
            stage[cur_stage, pl.ds(r0 * SUBLANES + sl, nr, stride=SUBLANES), :] = h3[:, sl * LANES:(sl + 1) * LANES]

        hi = h3.astype(BF16)
        lo = (h3 - hi.astype(F32)).astype(BF16)
        hw = jnp.dot(hi, wr_ref[...], preferred_element_type=F32)
        return (hw[:, :LANES] + hw[:, LANES:]
                + jnp.dot(lo, wr_ref[:, :LANES], preferred_element_type=F32) + br_ref[...])

    nr = tm // MID_CHAINS
    logits = jnp.concatenate([token_chain(c * nr, nr) for c in range(MID_CHAINS)], axis=0)
    lt = logits.T[:N_EXPERTS, :]
    eidx = lax.broadcasted_iota(I32, lt.shape, 0)
    vals, idxs, sels = [], [], []
    for _ in range(TOP_K):
        m = jnp.max(lt, axis=0, keepdims=True)
        idx = jnp.min(jnp.where(lt == m, eidx, N_EXPERTS), axis=0, keepdims=True)
        sel = eidx == idx
        lt = jnp.where(sel, -jnp.inf, lt)
        vals.append(m)
        idxs.append(idx)
        sels.append(sel)
    ex = [jnp.exp(vk - vals[0]) for vk in vals]
    inv = 1.0 / (ex[0] + ex[1] + ex[2] + ex[3])
    zrow = jnp.zeros_like(ex[0])
    gate_ref[...] = jnp.concatenate([e_ * inv for e_ in ex] + [zrow] * (SUBLANES - TOP_K), axis=0)

    onehot = jnp.zeros(lt.shape, F32)
    for sel in sels:
        onehot = onehot + sel.astype(F32)
    before = cnt_sc[:, 0:1] + jnp.dot(onehot.astype(BF16), tri_ref[...], preferred_element_type=F32)
    ranks = [jnp.sum(jnp.where(sel, before, 0.0), axis=0, keepdims=True).astype(I32) for sel in sels]
    meta_ref[...] = jnp.concatenate(idxs + ranks, axis=0)
    cnt_sc[...] = cnt_sc[...] + jnp.sum(onehot, axis=1, keepdims=True)
    cnt_ref[...] = cnt_sc[...]

    posv[...] = jnp.concatenate([ix * cap + rk for ix, rk in zip(idxs, ranks)], axis=0)
    to_smem = pltpu.make_async_copy(posv, pos_sm.at[step % 2], psem)
    to_smem.start()
    to_smem.wait()

    @pl.when(step == nsteps - 1)
    def _():
        @pl.when(step >= 1)
        def _():
            wait_scatter((step + 1) % 2)

        def issue(tb, carry):
            for u in range(ISSUE_UNROLL):
                t = tb * ISSUE_UNROLL + u
                for k in range(TOP_K):
                    r0 = pl.multiple_of(pos_sm[step % 2, k, t] * SUBLANES, SUBLANES)
                    _row_copy(stage.at[cur_stage, pl.ds(pl.multiple_of(t * SUBLANES, SUBLANES), SUBLANES), :],
                              slots_ref.at[pl.ds(r0, SUBLANES), :], rsem.at[(step + 1) % 2]).start(priority=k % 2)
            return carry

        lax.fori_loop(0, tm // ISSUE_UNROLL, issue, 0)
        wait_scatter(step % 2)
        wait_scatter((step + 1) % 2)

        cntv[...] = cnt_sc[...].astype(I32)
        counts_to_smem = pltpu.make_async_copy(cntv, cnt_sm, psem)
        counts_to_smem.start()
        counts_to_smem.wait()
        zbuf[...] = jnp.zeros_like(zbuf)
        zrows = zbuf.shape[0] // SUBLANES
        bits = [1 << j for j in range(zrows.bit_length() - 1, -1, -1)]
        assert sum(bits) == 2 * zrows - 1 and zrows * 2 == te

        def fill(start):
            for e in range(N_EXPERTS):
                cnt = cnt_sm[e, 0]
                pad = (te - cnt % te) % te
                row = e * cap + cnt
                for bit in bits:
                    @pl.when((pad & bit) != 0)
                    def _(row=row, bit=bit):
                        cp = _row_copy(zbuf.at[pl.ds(0, bit * SUBLANES), :],
                                       slots_ref.at[pl.ds(pl.multiple_of(row * SUBLANES, SUBLANES),
                                                          bit * SUBLANES), :], zsem)
                        cp.start() if start else cp.wait()
                    row = row + (pad & bit)

        fill(True)
        fill(False)


def _mid(x_p, x_s, a, f, kvc, g_attn_out, g_four_out, w_out, g_cross, w_q, w_o, g_moe, w_router, b_router,
         cap, te):
    n_prompt, seq, d = x_p.shape
    nb = a.shape[0]
    tm = min(TM_MID, seq)
    nt = seq // tm
    n_mem = kvc.shape[1]
    pb, sb = _two_group_maps(n_prompt)
    wr = jnp.zeros((d, LANES), F32).at[:, :N_EXPERTS].set(w_router)
    wr_hi = wr.astype(BF16)
    wr_lo = (wr - wr_hi.astype(F32)).astype(BF16)
    br = jnp.zeros((1, LANES), F32).at[0, :N_EXPERTS].set(b_router)
    tri = jnp.asarray(np.triu(np.ones((tm, tm), np.float32), 1), BF16)
    full = lambda arr: pl.BlockSpec(arr.shape, lambda b, i: (0,) * arr.ndim)
    row = lambda v: v.reshape(1, -1)
    args = [x_p, x_s, a, f, kvc, row(g_attn_out), row(g_four_out), w_out.astype(BF16), row(g_cross),
            w_q.astype(BF16), w_o.astype(BF16), row(g_moe), jnp.concatenate([wr_hi, wr_lo], axis=1), br, tri]
    in_specs = [
        pl.BlockSpec((1, tm, d), lambda b, i: (pb(b), i, 0)),
        pl.BlockSpec((1, tm, d), lambda b, i: (sb(b), i, 0)),
        pl.BlockSpec((1, tm, ATTN_WIDTH), lambda b, i: (b, i, 0)),
        pl.BlockSpec((1, tm, FOURIER_WIDTH), lambda b, i: (b, i, 0)),
        pl.BlockSpec((1, n_mem, 2 * CROSS_WIDTH), lambda b, i: (b, 0, 0)),
    ] + [full(v) for v in args[5:]]
    tok = nb * seq
    slot_rows = N_EXPERTS * cap + TOP_K * tm
    return pl.pallas_call(
        functools.partial(_mid_kernel, n_prompt=n_prompt, cap=cap, te=te),
        out_shape=(jax.ShapeDtypeStruct((nb, seq, d), F32),
                   jax.ShapeDtypeStruct((SUBLANES, tok), I32),
                   jax.ShapeDtypeStruct((SUBLANES, tok), F32),
                   jax.ShapeDtypeStruct((N_EXPERTS, LANES), F32),
                   jax.ShapeDtypeStruct((slot_rows * SUBLANES, LANES), F32)),
        grid=(nb, nt),
        in_specs=in_specs,
        out_specs=(pl.BlockSpec((1, tm, d), lambda b, i: (b, i, 0)),
                   pl.BlockSpec((SUBLANES, tm), lambda b, i: (0, b * nt + i)),
                   pl.BlockSpec((SUBLANES, tm), lambda b, i: (0, b * nt + i)),
                   pl.BlockSpec((N_EXPERTS, LANES), lambda b, i: (0, 0)),
                   pl.BlockSpec(memory_space=pl.ANY)),
        scratch_shapes=[pltpu.VMEM((N_EXPERTS, LANES), F32),
                        pltpu.VMEM((3, tm * SUBLANES, LANES), F32),
                        pltpu.VMEM((TOP_K, tm), I32), pltpu.SMEM((2, TOP_K, tm), I32),
                        pltpu.VMEM((N_EXPERTS, LANES), I32), pltpu.SMEM((N_EXPERTS, LANES), I32),
                        pltpu.VMEM((te // 2 * SUBLANES, LANES), F32),
                        pltpu.SemaphoreType.DMA((2,)), pltpu.SemaphoreType.DMA, pltpu.SemaphoreType.DMA],
        compiler_params=_cparams(("arbitrary", "arbitrary")),
        name="mid",
    )(*args)


def _expert_kernel(te_ref, ti_ref, to_ref, nu_ref, xs_ref, wgu_ref, bgu_ref, wd_ref, bd_ref, ys_ref, wgu_bf, wd_bf):
    i = pl.program_id(0)
    te = xs_ref.shape[0] // SUBLANES
    dff = wd_ref.shape[1]

    @pl.when((i == 0) | (te_ref[i] != te_ref[jnp.maximum(i - 1, 0)]))
    def _():
        wgu_bf[...] = wgu_ref[0].astype(BF16)
        wd_bf[...] = wd_ref[0].astype(BF16)

    @pl.when(i < nu_ref[0])
    def _():
        x = jnp.concatenate([xs_ref[pl.ds(sl, te, stride=SUBLANES), :] for sl in range(SUBLANES)],
                            axis=1).astype(BF16)
        gu = jnp.dot(x, wgu_bf[...], preferred_element_type=F32) + bgu_ref[0]
        gate = jnp.minimum(gu[:, :dff], SWIGLU_LIMIT)
        up = jnp.clip(gu[:, dff:], -SWIGLU_LIMIT, SWIGLU_LIMIT)
        act = (up + 1.0) * (gate * (1.0 / (1.0 + jnp.exp(-SWIGLU_ALPHA * gate))))
        y = jnp.dot(act.astype(BF16), wd_bf[...], preferred_element_type=F32) + bd_ref[0]
        for sl in range(SUBLANES):
            ys_ref[pl.ds(sl, te, stride=SUBLANES), :] = y[:, sl * LANES:(sl + 1) * LANES]


def _experts(xs, tile_expert, tile_in, tile_out, n_used, w_gate_up, b_gate_up, w_down, b_down, te):
    ne, d, two_dff = w_gate_up.shape
    dff = two_dff // 2
    ntile = tile_expert.shape[0]
    return pl.pallas_call(
        _expert_kernel,
        out_shape=jax.ShapeDtypeStruct((ntile * te * SUBLANES, LANES), F32),
        grid_spec=pltpu.PrefetchScalarGridSpec(
            num_scalar_prefetch=4,
            grid=(ntile,),
            in_specs=[
                pl.BlockSpec((te * SUBLANES, LANES), lambda i, te_r, ti_r, to_r, nu_r: (ti_r[i], 0)),
                pl.BlockSpec((1, d, two_dff), lambda i, te_r, ti_r, to_r, nu_r: (te_r[i], 0, 0)),
                pl.BlockSpec((1, 1, two_dff), lambda i, te_r, ti_r, to_r, nu_r: (te_r[i], 0, 0)),
                pl.BlockSpec((1, dff, d), lambda i, te_r, ti_r, to_r, nu_r: (te_r[i], 0, 0)),
                pl.BlockSpec((1, 1, d), lambda i, te_r, ti_r, to_r, nu_r: (te_r[i], 0, 0)),
            ],
            out_specs=pl.BlockSpec((te * SUBLANES, LANES), lambda i, te_r, ti_r, to_r, nu_r: (to_r[i], 0)),
            scratch_shapes=[pltpu.VMEM((d, two_dff), BF16), pltpu.VMEM((dff, d), BF16)],
        ),
        compiler_params=_cparams(("arbitrary",)),
        name="experts",
    )(tile_expert, tile_in, tile_out, n_used, xs, w_gate_up, b_gate_up.reshape(ne, 1, two_dff),
      w_down, b_down.reshape(ne, 1, d))


def _combine_kernel(pos_a, pos_b, pos_n, x2_ref, gate_ref, g_ref, ys_ref, o_ref, buf_a, buf_b, sem, *, tm):
    step = pl.program_id(0) * pl.num_programs(1) + pl.program_id(1)
    nsteps = pl.num_programs(0) * pl.num_programs(1)
    slot_rows = TOP_K * tm * SUBLANES

    def wait_rows(buf, s):
        _row_copy(ys_ref.at[pl.ds(0, slot_rows), :], buf, sem.at[s]).wait()

    @pl.when(step == 0)
    def _():
        def issue(tb, carry):
            for u in range(ISSUE_UNROLL):
                t = tb * ISSUE_UNROLL + u
                for k in range(TOP_K):
                    r0 = pl.multiple_of(pos_a[0, 0, k * tm + t] * SUBLANES, SUBLANES)
                    d0 = pl.multiple_of((k * tm + t) * SUBLANES, SUBLANES)
                    _row_copy(ys_ref.at[pl.ds(r0, SUBLANES), :], buf_a.at[pl.ds(d0, SUBLANES), :],
                              sem.at[0]).start(priority=k % 2)
            return carry

        lax.fori_loop(0, tm // ISSUE_UNROLL, issue, 0)

    def combine_tile(rows, buf, req_pos, req_buf, req_sem):
        x = x2_ref[0, rows, :]
        g = gate_ref[rows, :]
        for k in range(TOP_K):
            for t in range(tm):
                r0 = pl.multiple_of(req_pos[0, 0, k * tm + t] * SUBLANES, SUBLANES)
                _row_copy(ys_ref.at[pl.ds(r0, SUBLANES), :],
                          req_buf.at[pl.ds((k * tm + t) * SUBLANES, SUBLANES), :],
                          sem.at[req_sem]).start(priority=t % 2)
            yk = jnp.concatenate(
                [buf[pl.ds(k * tm * SUBLANES + sl, tm, stride=SUBLANES), :] for sl in range(SUBLANES)], axis=1)
            x = x + g[:, k:k + 1] * yk
        o_ref[0, rows, :] = _rms(x, g_ref[...])

    wait_rows(buf_a, 0)
    combine_tile(slice(0, tm), buf_a, pos_b, buf_b, 1)
    wait_rows(buf_b, 1)
    combine_tile(slice(tm, 2 * tm), buf_b, pos_n, buf_a, 0)

    @pl.when(step == nsteps - 1)
    def _():
        wait_rows(buf_a, 0)


def _combine(x2, gate_t, pos3, ys, g_final, b0, nbatch):
    _, seq, d = x2.shape
    tm = pos3.shape[2] // TOP_K
    nt = seq // (2 * tm)
    assert seq % (2 * tm) == 0
    last = 2 * (b0 + nbatch) * nt - 2
    tile_a = lambda b, i: 2 * ((b0 + b) * nt + i)
    smem_tile = lambda fn: pl.BlockSpec((1, 1, TOP_K * tm), lambda b, i: (fn(b, i), 0, 0),
                                        memory_space=pltpu.SMEM)
    return pl.pallas_call(
        functools.partial(_combine_kernel, tm=tm),
        out_shape=jax.ShapeDtypeStruct((nbatch, seq, d), F32),
        grid_spec=pltpu.PrefetchScalarGridSpec(
            num_scalar_prefetch=0,
            grid=(nbatch, nt),
            in_specs=[
                smem_tile(tile_a),
                smem_tile(lambda b, i: tile_a(b, i) + 1),
                smem_tile(lambda b, i: jnp.minimum(tile_a(b, i) + 2, last)),
                pl.BlockSpec((1, 2 * tm, d), lambda b, i: (b0 + b, i, 0)),
                pl.BlockSpec((2 * tm, SUBLANES), lambda b, i: ((b0 + b) * nt + i, 0)),
                pl.BlockSpec((1, d), lambda b, i: (0, 0)),
                pl.BlockSpec(memory_space=pl.ANY),
            ],
            out_specs=pl.BlockSpec((1, 2 * tm, d), lambda b, i: (b, i, 0)),
            scratch_shapes=[pltpu.VMEM((TOP_K * tm * SUBLANES, LANES), F32),
                            pltpu.VMEM((TOP_K * tm * SUBLANES, LANES), F32),
                            pltpu.SemaphoreType.DMA((2,))],
        ),
        compiler_params=_cparams(("arbitrary", "arbitrary")),
        name="combine",
    )(pos3, pos3, pos3, x2, gate_t, g_final.reshape(1, d), ys)


def _routing_plan(meta, cnt, te, tm_rows, cap):
    tok = meta.shape[1]
    idx, rank = meta[:TOP_K], meta[TOP_K:]
    counts = cnt[:, 0].astype(I32)
    tiles = (counts + te - 1) // te
    tile_end = jnp.cumsum(tiles)
    tile_start = tile_end - tiles
    offsets = tile_start * te
    n_used = tile_end[-1]
    pos = rank + jnp.sum(jnp.where(idx[None] == jnp.arange(N_EXPERTS, dtype=I32)[:, None, None],
                                   offsets[:, None, None], 0), axis=0)
    ntile_max = (tok * TOP_K) // te + N_EXPERTS
    t_ids = jnp.minimum(jnp.arange(ntile_max, dtype=I32), n_used - 1)
    tile_expert = jnp.sum((t_ids[:, None] >= tile_end[None, :]).astype(I32), axis=1)
    first = jnp.sum(jnp.where(tile_expert[:, None] == jnp.arange(N_EXPERTS, dtype=I32)[None, :],
                              tile_start[None, :], 0), axis=1)
    tile_in = tile_expert * (cap // te) + (t_ids - first)
    ntok_tiles = tok // tm_rows
    pos3 = pos.reshape(TOP_K, ntok_tiles, tm_rows).transpose(1, 0, 2).reshape(ntok_tiles, 1, TOP_K * tm_rows)
    return pos3, tile_expert, tile_in, t_ids, n_used.reshape(1)


def kernel(x_prompt, x_sample, mem_prompt, mem_sample, g_mix, w_in, sinks, g_attn_out, g_four_out, w_out,
           g_cross, g_mem, w_q_cross, w_kv_cross, w_o_cross, g_moe, w_router, b_router, w_gate_up,
           b_gate_up, w_down, b_down, g_final):
    n_prompt, seq, _ = x_prompt.shape
    n_sample = x_sample.shape[0]
    l = 0
    kvc = _memkv(mem_prompt, mem_sample, g_mem[l], w_kv_cross[l])
    q, kv, ab = _inproj(x_prompt, x_sample, g_mix[l], w_in[l])
    a = _swa(q, kv, sinks[l])
    f = _seqdft(ab)
    tok = (n_prompt + n_sample) * seq
    te = TM_EXPERT
    cap = -(-tok // te) * te
    x2, meta, gates, cnt, xs = _mid(x_prompt, x_sample, a, f, kvc, g_attn_out[l], g_four_out[l], w_out[l],
                                    g_cross[l], w_q_cross[l], w_o_cross[l], g_moe[l], w_router[l], b_router[l],
                                    cap, te)
    tm_rows = min(TM_ROWS, seq // 2)
    pos3, tile_expert, tile_in, tile_out, n_used = _routing_plan(meta, cnt, te, tm_rows, cap)
    ys = _experts(xs, tile_expert, tile_in, tile_out, n_used, w_gate_up[l], b_gate_up[l], w_down[l], b_down[l], te)
    gate_t = gates.T
    y_p = _combine(x2, gate_t, pos3, ys, g_final, 0, n_prompt)
    y_s = _combine(x2, gate_t, pos3, ys, g_final, n_prompt, n_sample)
    return (y_p, y_s)
```

```python
import functools

import numpy as np
import jax
import jax.numpy as jnp
from jax import lax
from jax.experimental import pallas as pl
from jax.experimental.pallas import tpu as pltpu

F32 = jnp.float32
BF16 = jnp.bfloat16
I32 = jnp.int32

HEAD_DIM = 64
N_Q_HEADS = 8
N_KV_HEADS = 2
ATTN_WIDTH = N_Q_HEADS * HEAD_DIM
KV_WIDTH = N_KV_HEADS * HEAD_DIM
FOURIER_WIDTH = 512
FOURIER_GROUP = 64
WINDOW = 128
ROPE_DIM = 16
ROPE_THETA = 500000.0
N_CROSS_HEADS = 4
CROSS_HEAD_DIM = 128
CROSS_WIDTH = N_CROSS_HEADS * CROSS_HEAD_DIM
N_EXPERTS = 32
TOP_K = 4
SWIGLU_LIMIT = 7.0
SWIGLU_ALPHA = 1.702
EPS = 1e-5
NEG_INF = -1e30
LOG2E = 1.4426950408889634

LANES = 128
SUBLANES = 8
VMEM_LIMIT_BYTES = 56 * 1024 * 1024

TS_INPROJ = 512
TQ_ATTN = 512
TR_DFT = 512
TM_MID = 512
TM_ROWS = 512
TM_EXPERT = 512
ISSUE_UNROLL = 4
MID_CHAINS = 1
SCATTER_CHUNKS = 4 + N_CROSS_HEADS


def _cparams(sem):
    return pltpu.CompilerParams(dimension_semantics=sem, vmem_limit_bytes=VMEM_LIMIT_BYTES)


def _rms(x, g):
    return x * lax.rsqrt(jnp.mean(x * x, axis=-1, keepdims=True) + EPS) * g


def _row_copy(src_rows, dst_rows, sem):
    return pltpu.make_async_copy(src_rows, dst_rows, sem)


def _two_group_maps(n_prompt):
    def prompt_b(b):
        return jnp.minimum(b, n_prompt - 1)

    def sample_b(b):
        return jnp.maximum(b - n_prompt, 0)
    return prompt_b, sample_b


def _memkv_kernel(mp_ref, ms_ref, g_ref, w_ref, o_ref, *, n_prompt):
    b = pl.program_id(0)
    m = jnp.where(b < n_prompt, mp_ref[0], ms_ref[0])
    mn = _rms(m, g_ref[...]).astype(BF16)
    o_ref[0] = jnp.dot(mn, w_ref[...], preferred_element_type=F32).astype(BF16)


def _memkv(mem_p, mem_s, g_mem, w_kv):
    n_prompt, n_mem, d = mem_p.shape
    nb = n_prompt + mem_s.shape[0]
    pb, sb = _two_group_maps(n_prompt)
    return pl.pallas_call(
        functools.partial(_memkv_kernel, n_prompt=n_prompt),
        out_shape=jax.ShapeDtypeStruct((nb, n_mem, w_kv.shape[1]), BF16),
        grid=(nb,),
        in_specs=[
            pl.BlockSpec((1, n_mem, d), lambda b: (pb(b), 0, 0)),
            pl.BlockSpec((1, n_mem, d), lambda b: (sb(b), 0, 0)),
            pl.BlockSpec((1, d), lambda b: (0, 0)),
            pl.BlockSpec(w_kv.shape, lambda b: (0, 0)),
        ],
        out_specs=pl.BlockSpec((1, n_mem, w_kv.shape[1]), lambda b: (b, 0, 0)),
        compiler_params=_cparams(("arbitrary",)),
        name="memkv",
    )(mem_p, mem_s, g_mem.reshape(1, d), w_kv.astype(BF16))


_ROT_W = ATTN_WIDTH + KV_WIDTH


def _inproj_kernel(xp_ref, xs_ref, g_ref, w_ref, cos_ref, sin_ref, bd_ref,
                   q_ref, kv_ref, ab_ref, *, n_prompt):
    b = pl.program_id(1)
    x = jnp.where(b < n_prompt, xp_ref[0], xs_ref[0])
    h = _rms(x, g_ref[...]).astype(BF16)
    z = jnp.dot(h, w_ref[...], preferred_element_type=F32)
    cos = cos_ref[...]
    sin = sin_ref[...]
    lane = lax.broadcasted_iota(I32, cos.shape, 1) & (HEAD_DIM - 1)
    first_half = lane < ROPE_DIM // 2
    rot = []
    for c in range(_ROT_W // LANES):
        zc = z[:, c * LANES:(c + 1) * LANES]
        partner = jnp.where(first_half,
                            pltpu.roll(zc, LANES - ROPE_DIM // 2, axis=1),
                            pltpu.roll(zc, ROPE_DIM // 2, axis=1))
        rot.append(zc * cos + partner * sin)
    nq = ATTN_WIDTH // LANES
    q_ref[0] = (jnp.concatenate(rot[:nq], axis=1) * (LOG2E * HEAD_DIM ** -0.5)).astype(BF16)
    v_off = _ROT_W
    u_off = _ROT_W + KV_WIDTH
    k, v = rot[nq], z[:, v_off:u_off]
    kv_ref[0] = jnp.concatenate([k, pltpu.roll(k, HEAD_DIM, axis=1), v, pltpu.roll(v, HEAD_DIM, axis=1)],
                                axis=1).astype(BF16)
    half = FOURIER_WIDTH // 2
    u = z[:, u_off:].astype(BF16)
    r0 = jnp.dot(u[:, :half], bd_ref[...], preferred_element_type=F32)
    r1 = jnp.dot(u[:, half:], bd_ref[...], preferred_element_type=F32)
    ab_ref[0] = jnp.concatenate([r0[:, :half], r1[:, :half], r0[:, half:], r1[:, half:]],
                                axis=1).astype(BF16)


def _rope_tables(seq):
    half = ROPE_DIM // 2
    inv_freq = ROPE_THETA ** (-(jnp.arange(half, dtype=F32) * 2.0) / ROPE_DIM)
    ang = jnp.arange(seq).astype(F32)[:, None] * inv_freq[None, :]
    cos, sin = jnp.cos(ang), jnp.sin(ang)
    rest = HEAD_DIM - ROPE_DIM
    cos_h = jnp.concatenate([cos, cos, jnp.ones((seq, rest), F32)], axis=1)
    sin_h = jnp.concatenate([-sin, sin, jnp.zeros((seq, rest), F32)], axis=1)
    reps = LANES // HEAD_DIM
    return jnp.tile(cos_h, (1, reps)), jnp.tile(sin_h, (1, reps))


def _channel_dft_tables():
    n = FOURIER_GROUP
    idx = np.arange(n)
    ang = 2.0 * np.pi * ((idx[:, None] * idx[None, :]) % n) / n
    eye = np.eye(FOURIER_WIDTH // (2 * n))
    return jnp.asarray(np.concatenate([np.kron(eye, np.cos(ang)), np.kron(eye, np.sin(ang))], axis=1), BF16)


def _inproj(x_p, x_s, g_mix, w_in):
    n_prompt, seq, d = x_p.shape
    nb = n_prompt + x_s.shape[0]
    ts = min(TS_INPROJ, seq)
    w_ext = w_in.astype(BF16)
    cos_t, sin_t = _rope_tables(seq)
    bd = _channel_dft_tables()
    pb, sb = _two_group_maps(n_prompt)
    kvw = 4 * KV_WIDTH
    return pl.pallas_call(
        functools.partial(_inproj_kernel, n_prompt=n_prompt),
        out_shape=(jax.ShapeDtypeStruct((nb, seq, ATTN_WIDTH), BF16),
                   jax.ShapeDtypeStruct((nb, seq, kvw), BF16),
                   jax.ShapeDtypeStruct((nb, seq, 2 * FOURIER_WIDTH), BF16)),
        grid=(seq // ts, nb),
        in_specs=[
            pl.BlockSpec((1, ts, d), lambda i, b: (pb(b), i, 0)),
            pl.BlockSpec((1, ts, d), lambda i, b: (sb(b), i, 0)),
            pl.BlockSpec((1, d), lambda i, b: (0, 0)),
            pl.BlockSpec(w_ext.shape, lambda i, b: (0, 0)),
            pl.BlockSpec((ts, LANES), lambda i, b: (i, 0)),
            pl.BlockSpec((ts, LANES), lambda i, b: (i, 0)),
            pl.BlockSpec(bd.shape, lambda i, b: (0, 0)),
        ],
        out_specs=(pl.BlockSpec((1, ts, ATTN_WIDTH), lambda i, b: (b, i, 0)),
                   pl.BlockSpec((1, ts, kvw), lambda i, b: (b, i, 0)),
                   pl.BlockSpec((1, ts, 2 * FOURIER_WIDTH), lambda i, b: (b, i, 0))),
        compiler_params=_cparams(("arbitrary", "arbitrary")),
        name="inproj",
    )(x_p, x_s, g_mix.reshape(1, d), w_ext, cos_t, sin_t, bd)


def _swa_kernel(sink_ref, q_ref, kvm_ref, kvp_ref, kvn_ref, bias_ref, o_ref, *, nblk):
    i = pl.program_id(1)
    last = pl.num_programs(1) - 1
    kvw = jnp.concatenate([kvp_ref[0], kvm_ref[0], kvn_ref[0]], axis=0)
    k, ksw, v, vsw = [kvw[:, c * LANES:(c + 1) * LANES] for c in range(4)]
    lo_kv = lax.broadcasted_iota(I32, k.shape, 1) < HEAD_DIM
    k_dup = (jnp.where(lo_kv, k, ksw), jnp.where(lo_kv, ksw, k))
    v_dup = (jnp.where(lo_kv, v, vsw), jnp.where(lo_kv, vsw, v))
    lo_q = lax.broadcasted_iota(I32, (WINDOW, LANES), 1) < HEAD_DIM
    row = lax.broadcasted_iota(I32, (4 * WINDOW, 1), 0)
    for j in range(nblk):
        bias = bias_ref[1]
        if j == 0:
            bias = jnp.where(i == 0, bias_ref[0], bias)
        if j == nblk - 1:
            bias = jnp.where(i == last, bias_ref[2], bias)
        bias4 = jnp.concatenate([bias] * 4, axis=0)
        rows = slice(j * WINDOW, (j + 1) * WINDOW)
        win = slice(j * WINDOW, (j + 3) * WINDOW)
        for kvh in range(N_KV_HEADS):
            c0 = 2 * kvh
            qa = q_ref[0, rows, c0 * LANES:(c0 + 1) * LANES]
            qb = q_ref[0, rows, (c0 + 1) * LANES:(c0 + 2) * LANES]
            zero = jnp.zeros_like(qa)
            q4 = jnp.concatenate([jnp.where(lo_q, qa, zero), jnp.where(lo_q, zero, qa),
                                  jnp.where(lo_q, qb, zero), jnp.where(lo_q, zero, qb)], axis=0)
            s = lax.dot_general(q4, k_dup[kvh][win], (((1,), (1,)), ((), ())),
                                preferred_element_type=F32) + bias4
            h0 = 4 * kvh
            sink = jnp.where(row < WINDOW, sink_ref[h0],
                             jnp.where(row < 2 * WINDOW, sink_ref[h0 + 1],
                                       jnp.where(row < 3 * WINDOW, sink_ref[h0 + 2], sink_ref[h0 + 3])))
            m = jnp.maximum(jnp.max(s, axis=-1, keepdims=True), sink)
            e = jnp.exp2(s - m)
            den = jnp.sum(e, axis=-1, keepdims=True) + jnp.exp2(sink - m)
            o = jnp.dot(e.astype(BF16), v_dup[kvh][win], preferred_element_type=F32) * (1.0 / den)
            o_ref[0, rows, c0 * LANES:(c0 + 1) * LANES] = jnp.where(
                lo_q, o[:WINDOW], o[WINDOW:2 * WINDOW]).astype(o_ref.dtype)
            o_ref[0, rows, (c0 + 1) * LANES:(c0 + 2) * LANES] = jnp.where(
                lo_q, o[2 * WINDOW:3 * WINDOW], o[3 * WINDOW:]).astype(o_ref.dtype)


def _swa_bias():
    qi = np.arange(WINDOW)[:, None]
    c = np.arange(3 * WINDOW)[None, :]
    band = np.abs(c - WINDOW - qi) <= WINDOW
    first = band & (c >= WINDOW)
    lastb = band & (c < 2 * WINDOW)
    tab = np.stack([first, band, lastb]).astype(np.float32)
    return jnp.asarray((1.0 - tab) * NEG_INF, F32)


def _swa(q, kv, sinks):
    nb, seq, _ = q.shape
    tq = min(TQ_ATTN, seq)
    nblk = tq // WINDOW
    nseq_blk = seq // WINDOW
    kvw = kv.shape[-1]
    assert nblk >= 2 and seq % tq == 0, "first/last window masks are applied to distinct query blocks"
    bias = _swa_bias()
    return pl.pallas_call(
        functools.partial(_swa_kernel, nblk=nblk),
        out_shape=jax.ShapeDtypeStruct((nb, seq, ATTN_WIDTH), BF16),
        grid=(nb, seq // tq),
        in_specs=[
            pl.BlockSpec(memory_space=pltpu.SMEM),
            pl.BlockSpec((1, tq, ATTN_WIDTH), lambda b, i: (b, i, 0)),
            pl.BlockSpec((1, tq, kvw), lambda b, i: (b, i, 0)),
            pl.BlockSpec((1, WINDOW, kvw), lambda b, i: (b, jnp.maximum(i * nblk - 1, 0), 0)),
            pl.BlockSpec((1, WINDOW, kvw), lambda b, i: (b, jnp.minimum((i + 1) * nblk, nseq_blk - 1), 0)),
            pl.BlockSpec(bias.shape, lambda b, i: (0, 0, 0)),
        ],
        out_specs=pl.BlockSpec((1, tq, ATTN_WIDTH), lambda b, i: (b, i, 0)),
        compiler_params=_cparams(("arbitrary", "arbitrary")),
        name="swa",
    )(sinks.astype(F32) * LOG2E, q, kv, kv, kv, bias)


_DFT_PAD_ROWS = 16


def _seqdft_kernel(ab_ref, ct_hbm, st_hbm, perm_ref, o_ref, ct, st, aebo, hbuf, sem, *, scale, tk):
    n = ab_ref.shape[1]
    m = n // 2
    fw = FOURIER_WIDTH
    nblk = m // LANES

    @pl.when(pl.program_id(0) == 0)
    def _():
        copies = [pltpu.make_async_copy(ct_hbm, ct, sem.at[0]), pltpu.make_async_copy(st_hbm, st, sem.at[1])]
        for c in copies:
            c.start()
        for c in copies:
            c.wait()

    perm = perm_ref[...]
    for blk in range(nblk):
        mirrored = [ab_ref[0, n - LANES * (blk + 1):n - LANES * blk, :]]
        if blk > 0:
            mirrored.append(ab_ref[0, n - LANES * blk:n - LANES * (blk - 1), :])
        else:
            mirrored.append(jnp.zeros((LANES, 2 * fw), BF16))
        r = jnp.dot(perm, jnp.concatenate(mirrored, axis=0), preferred_element_type=F32)
        cur = ab_ref[0, LANES * blk:LANES * (blk + 1), :].astype(F32)
        aebo[LANES * blk:LANES * (blk + 1), :] = jnp.concatenate(
            [cur[:, :fw] + r[:, :fw], cur[:, fw:] - r[:, fw:]], axis=1).astype(BF16)

    a_mid = ab_ref[0, m:m + _DFT_PAD_ROWS, :fw][0:1, :].astype(F32)

    def pq(rows, nrows):
        p = jnp.dot(ct[rows, :], aebo[:, :fw], preferred_element_type=F32)
        q = jnp.dot(st[rows, :], aebo[:, fw:], preferred_element_type=F32)
        odd = (lax.broadcasted_iota(I32, (nrows, fw), 0) & 1) == 1
        return p + jnp.where(odd, -a_mid, a_mid), q

    for kt in range(m // tk):
        rows = slice(kt * tk, (kt + 1) * tk)
        p, q = pq(rows, tk)
        o_ref[0, rows, :] = ((p - q) * scale).astype(o_ref.dtype)
        hbuf[rows, :] = p + q
    p, q = pq(slice(m, m + _DFT_PAD_ROWS), _DFT_PAD_ROWS)
    hbuf[m:m + _DFT_PAD_ROWS, :] = p + q
    hbuf[m + _DFT_PAD_ROWS:, :] = jnp.zeros((LANES - _DFT_PAD_ROWS, fw), F32)
    for c in range(nblk):
        win = hbuf[LANES * c:LANES * (c + 2), :].astype(BF16)
        blk_out = jnp.dot(perm, win, preferred_element_type=F32)
        o_ref[0, n - LANES * (c + 1):n - LANES * c, :] = (blk_out * scale).astype(o_ref.dtype)


def _seq_dft_tables(seq):
    m = seq // 2
    n2 = 64
    n1 = m // n2
    k = np.arange(m + _DFT_PAD_ROWS)[:, None]
    a1 = 2.0 * np.pi * ((k * n2 * np.arange(n1)[None, :]) % seq) / seq
    a2 = 2.0 * np.pi * ((k * np.arange(n2)[None, :]) % seq) / seq
    c1, s1 = jnp.asarray(np.cos(a1), F32), jnp.asarray(np.sin(a1), F32)
    c2, s2 = jnp.asarray(np.cos(a2), F32), jnp.asarray(np.sin(a2), F32)
    cs = c1[:, :, None] * c2[:, None, :] - s1[:, :, None] * s2[:, None, :]
    ss = s1[:, :, None] * c2[:, None, :] + c1[:, :, None] * s2[:, None, :]
    return cs.reshape(-1, m).astype(BF16), ss.reshape(-1, m).astype(BF16)


def _seqdft(ab):
    nb, seq, _ = ab.shape
    m = seq // 2
    tk = min(TR_DFT, m)
    assert m % LANES == 0 and m % tk == 0 and m % 64 == 0
    ct, st = _seq_dft_tables(seq)
    perm = np.zeros((LANES, 2 * LANES), np.float32)
    perm[np.arange(LANES), LANES - np.arange(LANES)] = 1.0
    scale = float(1.0 / np.sqrt(seq * FOURIER_GROUP))
    return pl.pallas_call(
        functools.partial(_seqdft_kernel, scale=scale, tk=tk),
        out_shape=jax.ShapeDtypeStruct((nb, seq, FOURIER_WIDTH), BF16),
        grid=(nb,),
        in_specs=[
            pl.BlockSpec((1, seq, 2 * FOURIER_WIDTH), lambda b: (b, 0, 0)),
            pl.BlockSpec(memory_space=pl.ANY),
            pl.BlockSpec(memory_space=pl.ANY),
            pl.BlockSpec(perm.shape, lambda b: (0, 0)),
        ],
        out_specs=pl.BlockSpec((1, seq, FOURIER_WIDTH), lambda b: (b, 0, 0)),
        scratch_shapes=[pltpu.VMEM(ct.shape, BF16), pltpu.VMEM(st.shape, BF16),
                        pltpu.VMEM((m, 2 * FOURIER_WIDTH), BF16),
                        pltpu.VMEM((m + LANES, FOURIER_WIDTH), F32),
                        pltpu.SemaphoreType.DMA((2,))],
        compiler_params=_cparams(("arbitrary",)),
        name="seqdft",
    )(ab, ct, st, jnp.asarray(perm, BF16))


def _mid_kernel(xp_ref, xs_ref, a_ref, f_ref, kv_ref, gao_ref, gfo_ref, wout_ref, gc_ref, wq_ref,
                wo_ref, gm_ref, wr_ref, br_ref, tri_ref,
                x2_ref, meta_ref, gate_ref, cnt_ref, slots_ref,
                cnt_sc, stage, posv, pos_sm, cntv, cnt_sm, zbuf, rsem, psem, zsem, *, n_prompt, cap, te):
    b = pl.program_id(0)
    i = pl.program_id(1)
    tm = a_ref.shape[1]
    step = b * pl.num_programs(1) + i
    nsteps = pl.num_programs(0) * pl.num_programs(1)
    tile_rows = tm * SUBLANES
    dump_base = N_EXPERTS * cap

    def wait_scatter(sem_idx):
        for _ in range(TOP_K):
            _row_copy(stage.at[0], slots_ref.at[pl.ds(0, tile_rows), :], rsem.at[sem_idx]).wait()

    @pl.when(step == 0)
    def _():
        cnt_sc[...] = jnp.zeros_like(cnt_sc)
        stage[2] = jnp.zeros((tile_rows, LANES), F32)

        def init(t, carry):
            for k in range(TOP_K):
                pos_sm[1, k, t] = dump_base + k * tm + t
            return carry

        lax.fori_loop(0, tm, init, 0)

    @pl.when(step >= 2)
    def _():
        wait_scatter(step % 2)

    prev_stage = (step + 2) % 3
    prev_pos = (step + 1) % 2
    chunk_tokens = tm // SCATTER_CHUNKS

    def zero_after(v):
        bits = lax.bitcast_convert_type(v[0:1, 0:1], I32)
        z = lax.shift_right_logical(lax.shift_right_logical(bits, 31), 1)
        return z[0, 0]

    def scatter_chunk(c, z):
        for t in range(c * chunk_tokens, (c + 1) * chunk_tokens):
            for k in range(TOP_K):
                r0 = pl.multiple_of((pos_sm[prev_pos, k, t] + z) * SUBLANES, SUBLANES)
                _row_copy(stage.at[prev_stage, pl.ds(t * SUBLANES, SUBLANES), :],
                          slots_ref.at[pl.ds(r0, SUBLANES), :], rsem.at[step % 2]).start(priority=k % 2)

    cur_stage = step % 3

    def token_chain(r0, nr):
        rows = slice(r0, r0 + nr)
        scatter_chunk(0, 0)
        x = jnp.where(b < n_prompt, xp_ref[0, rows, :], xs_ref[0, rows, :])
        an = _rms(a_ref[0, rows, :].astype(F32), gao_ref[...]).astype(BF16)
        fn = _rms(f_ref[0, rows, :].astype(F32), gfo_ref[...]).astype(BF16)
        x1 = (x + jnp.dot(an, wout_ref[:ATTN_WIDTH, :], preferred_element_type=F32)
              + jnp.dot(fn, wout_ref[ATTN_WIDTH:, :], preferred_element_type=F32))
        scatter_chunk(1, zero_after(x1))

        h2 = _rms(x1, gc_ref[...]).astype(BF16)
        qc32 = jnp.dot(h2, wq_ref[...], preferred_element_type=F32) * (LOG2E * CROSS_HEAD_DIM ** -0.5)
        scatter_chunk(2, zero_after(qc32))
        qc = qc32.astype(BF16)
        heads = []
        for hd in range(N_CROSS_HEADS):
            cols = slice(hd * CROSS_HEAD_DIM, (hd + 1) * CROSS_HEAD_DIM)
            vcols = slice(CROSS_WIDTH + hd * CROSS_HEAD_DIM, CROSS_WIDTH + (hd + 1) * CROSS_HEAD_DIM)
            s = lax.dot_general(qc[:, cols], kv_ref[0, :, cols], (((1,), (1,)), ((), ())),
                                preferred_element_type=F32)
            e = jnp.exp2(s - jnp.max(s, axis=-1, keepdims=True))
            den = jnp.sum(e, axis=-1, keepdims=True)
            heads.append(jnp.dot(e.astype(BF16), kv_ref[0, :, vcols], preferred_element_type=F32) * (1.0 / den))
            scatter_chunk(3 + hd, zero_after(heads[-1]))
        oc = jnp.concatenate(heads, axis=1).astype(BF16)
        x2 = x1 + jnp.dot(oc, wo_ref[...], preferred_element_type=F32)
        x2_ref[0, rows, :] = x2
        scatter_chunk(3 + N_CROSS_HEADS, zero_after(x2))

        h3 = _rms(x2, gm_ref[...])
        for sl in range(SUBLANES):
            stage[cur_stage, pl.ds(r0 * SUBLANES + sl, nr, stride=SUBLANES), :] = h3[:, sl * LANES:(sl + 1) * LANES]

        hi = h3.astype(BF16)
        lo = (h3 - hi.astype(F32)).astype(BF16)
        hw = jnp.dot(hi, wr_ref[...], preferred_element_type=F32)
        return (hw[:, :LANES] + hw[:, LANES:]
                + jnp.dot(lo, wr_ref[:, :LANES], preferred_element_type=F32) + br_ref[...])

    assert MID_CHAINS == 1, "the scatter pieces are placed once, inside the single row chain"
    nr = tm // MID_CHAINS
    logits = jnp.concatenate([token_chain(c * nr, nr) for c in range(MID_CHAINS)], axis=0)
    lt = logits.T[:N_EXPERTS, :]
    eidx = lax.broadcasted_iota(I32, lt.shape, 0)
    vals, idxs, sels = [], [], []
    for _ in range(TOP_K):
        m = jnp.max(lt, axis=0, keepdims=True)
        idx = jnp.min(jnp.where(lt == m, eidx, N_EXPERTS), axis=0, keepdims=True)
        sel = eidx == idx
        lt = jnp.where(sel, -jnp.inf, lt)
        vals.append(m)
        idxs.append(idx)
        sels.append(sel)
    ex = [jnp.exp(vk - vals[0]) for vk in vals]
    inv = 1.0 / (ex[0] + ex[1] + ex[2] + ex[3])
    zrow = jnp.zeros_like(ex[0])
    gate_ref[...] = jnp.concatenate([e_ * inv for e_ in ex] + [zrow] * (SUBLANES - TOP_K), axis=0)

    onehot = jnp.zeros(lt.shape, F32)
    for sel in sels:
        onehot = onehot + sel.astype(F32)
    before = cnt_sc[:, 0:1] + jnp.dot(onehot.astype(BF16), tri_ref[...], preferred_element_type=F32)
    ranks = [jnp.sum(jnp.where(sel, before, 0.0), axis=0, keepdims=True).astype(I32) for sel in sels]
    meta_ref[...] = jnp.concatenate(idxs + ranks, axis=0)
    cnt_sc[...] = cnt_sc[...] + jnp.sum(onehot, axis=1, keepdims=True)
    cnt_ref[...] = cnt_sc[...]

    posv[...] = jnp.concatenate([ix * cap + rk for ix, rk in zip(idxs, ranks)], axis=0)
    to_smem = pltpu.make_async_copy(posv, pos_sm.at[step % 2], psem)
    to_smem.start()
    to_smem.wait()

    @pl.when(step == nsteps - 1)
    def _():
        @pl.when(step >= 1)
        def _():
            wait_scatter((step + 1) % 2)

        def issue(tb, carry):
            for u in range(ISSUE_UNROLL):
                t = tb * ISSUE_UNROLL + u
                for k in range(TOP_K):
                    r0 = pl.multiple_of(pos_sm[step % 2, k, t] * SUBLANES, SUBLANES)
                    _row_copy(stage.at[cur_stage, pl.ds(pl.multiple_of(t * SUBLANES, SUBLANES), SUBLANES), :],
                              slots_ref.at[pl.ds(r0, SUBLANES), :], rsem.at[(step + 1) % 2]).start(priority=k % 2)
            return carry

        lax.fori_loop(0, tm // ISSUE_UNROLL, issue, 0)
        wait_scatter(step % 2)
        wait_scatter((step + 1) % 2)

        cntv[...] = cnt_sc[...].astype(I32)
        counts_to_smem = pltpu.make_async_copy(cntv, cnt_sm, psem)
        counts_to_smem.start()
        counts_to_smem.wait()
        zbuf[...] = jnp.zeros_like(zbuf)
        zrows = zbuf.shape[0] // SUBLANES
        bits = [1 << j for j in range(zrows.bit_length() - 1, -1, -1)]
        assert sum(bits) == 2 * zrows - 1 and zrows * 2 == te

        def fill(start):
            for e in range(N_EXPERTS):
                cnt = cnt_sm[e, 0]
                pad = (te - cnt % te) % te
                row = e * cap + cnt
                for bit in bits:
                    @pl.when((pad & bit) != 0)
                    def _(row=row, bit=bit):
                        cp = _row_copy(zbuf.at[pl.ds(0, bit * SUBLANES), :],
                                       slots_ref.at[pl.ds(pl.multiple_of(row * SUBLANES, SUBLANES),
                                                          bit * SUBLANES), :], zsem)
                        cp.start() if start else cp.wait()
                    row = row + (pad & bit)

        fill(True)
        fill(False)


def _mid(x_p, x_s, a, f, kvc, g_attn_out, g_four_out, w_out, g_cross, w_q, w_o, g_moe, w_router, b_router,
         cap, te):
    n_prompt, seq, d = x_p.shape
    nb = a.shape[0]
    tm = min(TM_MID, seq)
    nt = seq // tm
    n_mem = kvc.shape[1]
    pb, sb = _two_group_maps(n_prompt)
    wr = jnp.zeros((d, LANES), F32).at[:, :N_EXPERTS].set(w_router)
    wr_hi = wr.astype(BF16)
    wr_lo = (wr - wr_hi.astype(F32)).astype(BF16)
    br = jnp.zeros((1, LANES), F32).at[0, :N_EXPERTS].set(b_router)
    tri = jnp.asarray(np.triu(np.ones((tm, tm), np.float32), 1), BF16)
    full = lambda arr: pl.BlockSpec(arr.shape, lambda b, i: (0,) * arr.ndim)
    row = lambda v: v.reshape(1, -1)
    args = [x_p, x_s, a, f, kvc, row(g_attn_out), row(g_four_out), w_out.astype(BF16), row(g_cross),
            w_q.astype(BF16), w_o.astype(BF16), row(g_moe), jnp.concatenate([wr_hi, wr_lo], axis=1), br, tri]
    in_specs = [
        pl.BlockSpec((1, tm, d), lambda b, i: (pb(b), i, 0)),
        pl.BlockSpec((1, tm, d), lambda b, i: (sb(b), i, 0)),
        pl.BlockSpec((1, tm, ATTN_WIDTH), lambda b, i: (b, i, 0)),
        pl.BlockSpec((1, tm, FOURIER_WIDTH), lambda b, i: (b, i, 0)),
        pl.BlockSpec((1, n_mem, 2 * CROSS_WIDTH), lambda b, i: (b, 0, 0)),
    ] + [full(v) for v in args[5:]]
    tok = nb * seq
    slot_rows = N_EXPERTS * cap + TOP_K * tm
    return pl.pallas_call(
        functools.partial(_mid_kernel, n_prompt=n_prompt, cap=cap, te=te),
        out_shape=(jax.ShapeDtypeStruct((nb, seq, d), F32),
                   jax.ShapeDtypeStruct((SUBLANES, tok), I32),
                   jax.ShapeDtypeStruct((SUBLANES, tok), F32),
                   jax.ShapeDtypeStruct((N_EXPERTS, LANES), F32),
                   jax.ShapeDtypeStruct((slot_rows * SUBLANES, LANES), F32)),
        grid=(nb, nt),
        in_specs=in_specs,
        out_specs=(pl.BlockSpec((1, tm, d), lambda b, i: (b, i, 0)),
                   pl.BlockSpec((SUBLANES, tm), lambda b, i: (0, b * nt + i)),
                   pl.BlockSpec((SUBLANES, tm), lambda b, i: (0, b * nt + i)),
                   pl.BlockSpec((N_EXPERTS, LANES), lambda b, i: (0, 0)),
                   pl.BlockSpec(memory_space=pl.ANY)),
        scratch_shapes=[pltpu.VMEM((N_EXPERTS, LANES), F32),
                        pltpu.VMEM((3, tm * SUBLANES, LANES), F32),
                        pltpu.VMEM((TOP_K, tm), I32), pltpu.SMEM((2, TOP_K, tm), I32),
                        pltpu.VMEM((N_EXPERTS, LANES), I32), pltpu.SMEM((N_EXPERTS, LANES), I32),
                        pltpu.VMEM((te // 2 * SUBLANES, LANES), F32),
                        pltpu.SemaphoreType.DMA((2,)), pltpu.SemaphoreType.DMA, pltpu.SemaphoreType.DMA],
        compiler_params=_cparams(("arbitrary", "arbitrary")),
        name="mid",
    )(*args)


def _expert_kernel(te_ref, ti_ref, to_ref, nu_ref, xs_ref, wgu_ref, bgu_ref, wd_ref, bd_ref, ys_ref, wgu_bf, wd_bf):
    i = pl.program_id(0)
    te = xs_ref.shape[0] // SUBLANES
    dff = wd_ref.shape[1]

    @pl.when((i == 0) | (te_ref[i] != te_ref[jnp.maximum(i - 1, 0)]))
    def _():
        wgu_bf[...] = wgu_ref[0].astype(BF16)
        wd_bf[...] = wd_ref[0].astype(BF16)

    @pl.when(i < nu_ref[0])
    def _():
        x = jnp.concatenate([xs_ref[pl.ds(sl, te, stride=SUBLANES), :] for sl in range(SUBLANES)],
                            axis=1).astype(BF16)
        gu = jnp.dot(x, wgu_bf[...], preferred_element_type=F32) + bgu_ref[0]
        gate = jnp.minimum(gu[:, :dff], SWIGLU_LIMIT)
        up = jnp.clip(gu[:, dff:], -SWIGLU_LIMIT, SWIGLU_LIMIT)
        act = (up + 1.0) * (gate * (1.0 / (1.0 + jnp.exp(-SWIGLU_ALPHA * gate))))
        y = jnp.dot(act.astype(BF16), wd_bf[...], preferred_element_type=F32) + bd_ref[0]
        for sl in range(SUBLANES):
            ys_ref[pl.ds(sl, te, stride=SUBLANES), :] = y[:, sl * LANES:(sl + 1) * LANES]


def _experts(xs, tile_expert, tile_in, tile_out, n_used, w_gate_up, b_gate_up, w_down, b_down, te):
    ne, d, two_dff = w_gate_up.shape
    dff = two_dff // 2
    ntile = tile_expert.shape[0]
    return pl.pallas_call(
        _expert_kernel,
        out_shape=jax.ShapeDtypeStruct((ntile * te * SUBLANES, LANES), F32),
        grid_spec=pltpu.PrefetchScalarGridSpec(
            num_scalar_prefetch=4,
            grid=(ntile,),
            in_specs=[
                pl.BlockSpec((te * SUBLANES, LANES), lambda i, te_r, ti_r, to_r, nu_r: (ti_r[i], 0)),
                pl.BlockSpec((1, d, two_dff), lambda i, te_r, ti_r, to_r, nu_r: (te_r[i], 0, 0)),
                pl.BlockSpec((1, 1, two_dff), lambda i, te_r, ti_r, to_r, nu_r: (te_r[i], 0, 0)),
                pl.BlockSpec((1, dff, d), lambda i, te_r, ti_r, to_r, nu_r: (te_r[i], 0, 0)),
                pl.BlockSpec((1, 1, d), lambda i, te_r, ti_r, to_r, nu_r: (te_r[i], 0, 0)),
            ],
            out_specs=pl.BlockSpec((te * SUBLANES, LANES), lambda i, te_r, ti_r, to_r, nu_r: (to_r[i], 0)),
            scratch_shapes=[pltpu.VMEM((d, two_dff), BF16), pltpu.VMEM((dff, d), BF16)],
        ),
        compiler_params=_cparams(("arbitrary",)),
        name="experts",
    )(tile_expert, tile_in, tile_out, n_used, xs, w_gate_up, b_gate_up.reshape(ne, 1, two_dff),
      w_down, b_down.reshape(ne, 1, d))


def _combine_kernel(pos_a, pos_b, pos_n, x2_ref, gate_ref, g_ref, ys_ref, o_ref, buf_a, buf_b, sem, *, tm):
    step = pl.program_id(0) * pl.num_programs(1) + pl.program_id(1)
    nsteps = pl.num_programs(0) * pl.num_programs(1)
    slot_rows = TOP_K * tm * SUBLANES

    def wait_rows(buf, s):
        _row_copy(ys_ref.at[pl.ds(0, slot_rows), :], buf, sem.at[s]).wait()

    @pl.when(step == 0)
    def _():
        def issue(tb, carry):
            for u in range(ISSUE_UNROLL):
                t = tb * ISSUE_UNROLL + u
                for k in range(TOP_K):
                    r0 = pl.multiple_of(pos_a[0, 0, k * tm + t] * SUBLANES, SUBLANES)
                    d0 = pl.multiple_of((k * tm + t) * SUBLANES, SUBLANES)
                    _row_copy(ys_ref.at[pl.ds(r0, SUBLANES), :], buf_a.at[pl.ds(d0, SUBLANES), :],
                              sem.at[0]).start(priority=k % 2)
            return carry

        lax.fori_loop(0, tm // ISSUE_UNROLL, issue, 0)

    def combine_tile(rows, buf, req_pos, req_buf, req_sem):
        x = x2_ref[0, rows, :]
        g = gate_ref[rows, :]
        for k in range(TOP_K):
            for t in range(tm):
                r0 = pl.multiple_of(req_pos[0, 0, k * tm + t] * SUBLANES, SUBLANES)
                _row_copy(ys_ref.at[pl.ds(r0, SUBLANES), :],
                          req_buf.at[pl.ds((k * tm + t) * SUBLANES, SUBLANES), :],
                          sem.at[req_sem]).start(priority=t % 2)
            yk = jnp.concatenate(
                [buf[pl.ds(k * tm * SUBLANES + sl, tm, stride=SUBLANES), :] for sl in range(SUBLANES)], axis=1)
            x = x + g[:, k:k + 1] * yk
        o_ref[0, rows, :] = _rms(x, g_ref[...])

    wait_rows(buf_a, 0)
    combine_tile(slice(0, tm), buf_a, pos_b, buf_b, 1)
    wait_rows(buf_b, 1)
    combine_tile(slice(tm, 2 * tm), buf_b, pos_n, buf_a, 0)

    @pl.when(step == nsteps - 1)
    def _():
        wait_rows(buf_a, 0)


def _combine(x2, gate_t, pos3, ys, g_final, b0, nbatch):
    _, seq, d = x2.shape
    tm = pos3.shape[2] // TOP_K
    nt = seq // (2 * tm)
    assert seq % (2 * tm) == 0
    last = 2 * (b0 + nbatch) * nt - 2
    tile_a = lambda b, i: 2 * ((b0 + b) * nt + i)
    smem_tile = lambda fn: pl.BlockSpec((1, 1, TOP_K * tm), lambda b, i: (fn(b, i), 0, 0),
                                        memory_space=pltpu.SMEM)
    return pl.pallas_call(
        functools.partial(_combine_kernel, tm=tm),
        out_shape=jax.ShapeDtypeStruct((nbatch, seq, d), F32),
        grid_spec=pltpu.PrefetchScalarGridSpec(
            num_scalar_prefetch=0,
            grid=(nbatch, nt),
            in_specs=[
                smem_tile(tile_a),
                smem_tile(lambda b, i: tile_a(b, i) + 1),
                smem_tile(lambda b, i: jnp.minimum(tile_a(b, i) + 2, last)),
                pl.BlockSpec((1, 2 * tm, d), lambda b, i: (b0 + b, i, 0)),
                pl.BlockSpec((2 * tm, SUBLANES), lambda b, i: ((b0 + b) * nt + i, 0)),
                pl.BlockSpec((1, d), lambda b, i: (0, 0)),
                pl.BlockSpec(memory_space=pl.ANY),
            ],
            out_specs=pl.BlockSpec((1, 2 * tm, d), lambda b, i: (b, i, 0)),
            scratch_shapes=[pltpu.VMEM((TOP_K * tm * SUBLANES, LANES), F32),
                            pltpu.VMEM((TOP_K * tm * SUBLANES, LANES), F32),
                            pltpu.SemaphoreType.DMA((2,))],
        ),
        compiler_params=_cparams(("arbitrary", "arbitrary")),
        name="combine",
    )(pos3, pos3, pos3, x2, gate_t, g_final.reshape(1, d), ys)


def _routing_plan(meta, cnt, te, tm_rows, cap):
    tok = meta.shape[1]
    idx, rank = meta[:TOP_K], meta[TOP_K:]
    counts = cnt[:, 0].astype(I32)
    tiles = (counts + te - 1) // te
    tile_end = jnp.cumsum(tiles)
    tile_start = tile_end - tiles
    offsets = tile_start * te
    n_used = tile_end[-1]
    pos = rank + jnp.sum(jnp.where(idx[None] == jnp.arange(N_EXPERTS, dtype=I32)[:, None, None],
                                   offsets[:, None, None], 0), axis=0)
    ntile_max = (tok * TOP_K) // te + N_EXPERTS
    t_ids = jnp.minimum(jnp.arange(ntile_max, dtype=I32), n_used - 1)
    tile_expert = jnp.sum((t_ids[:, None] >= tile_end[None, :]).astype(I32), axis=1)
    first = jnp.sum(jnp.where(tile_expert[:, None] == jnp.arange(N_EXPERTS, dtype=I32)[None, :],
                              tile_start[None, :], 0), axis=1)
    tile_in = tile_expert * (cap // te) + (t_ids - first)
    ntok_tiles = tok // tm_rows
    pos3 = pos.reshape(TOP_K, ntok_tiles, tm_rows).transpose(1, 0, 2).reshape(ntok_tiles, 1, TOP_K * tm_rows)
    return pos3, tile_expert, tile_in, t_ids, n_used.reshape(1)


def kernel(x_prompt, x_sample, mem_prompt, mem_sample, g_mix, w_in, sinks, g_attn_out, g_four_out, w_out,
           g_cross, g_mem, w_q_cross, w_kv_cross, w_o_cross, g_moe, w_router, b_router, w_gate_up,
           b_gate_up, w_down, b_down, g_final):
    n_prompt, seq, _ = x_prompt.shape
    n_sample = x_sample.shape[0]
    l = 0
    kvc = _memkv(mem_prompt, mem_sample, g_mem[l], w_kv_cross[l])
    q, kv, ab = _inproj(x_prompt, x_sample, g_mix[l], w_in[l])
    a = _swa(q, kv, sinks[l])
    f = _seqdft(ab)
    tok = (n_prompt + n_sample) * seq
    te = TM_EXPERT
    cap = -(-tok // te) * te
    x2, meta, gates, cnt, xs = _mid(x_prompt, x_sample, a, f, kvc, g_attn_out[l], g_four_out[l], w_out[l],
                                    g_cross[l], w_q_cross[l], w_o_cross[l], g_moe[l], w_router[l], b_router[l],
                                    cap, te)
    tm_rows = min(TM_ROWS, seq // 2)
    pos3, tile_expert, tile_in, tile_out, n_used = _routing_plan(meta, cnt, te, tm_rows, cap)
    ys = _experts(xs, tile_expert, tile_in, tile_out, n_used, w_gate_up[l], b_gate_up[l], w_down[l], b_down[l], te)
    gate_t = gates.T
    y_p = _combine(x2, gate_t, pos3, ys, g_final, 0, n_prompt)
    y_s = _combine(x2, gate_t, pos3, ys, g_final, n_prompt, n_sample)
    return (y_p, y_s)
```

```python
import functools

import numpy as np
import jax
import jax.numpy as jnp
from jax import lax
from jax.experimental import pallas as pl
from jax.experimental.pallas import tpu as pltpu

F32 = jnp.float32
BF16 = jnp.bfloat16
I32 = jnp.int32

HEAD_DIM = 64
N_Q_HEADS = 8
N_KV_HEADS = 2
ATTN_WIDTH = N_Q_HEADS * HEAD_DIM
KV_WIDTH = N_KV_HEADS * HEAD_DIM
FOURIER_WIDTH = 512
FOURIER_GROUP = 64
WINDOW = 128
ROPE_DIM = 16
ROPE_THETA = 500000.0
N_CROSS_HEADS = 4
CROSS_HEAD_DIM = 128
CROSS_WIDTH = N_CROSS_HEADS * CROSS_HEAD_DIM
N_EXPERTS = 32
TOP_K = 4
SWIGLU_LIMIT = 7.0
SWIGLU_ALPHA = 1.702
EPS = 1e-5
NEG_INF = -1e30
LOG2E = 1.4426950408889634

LANES = 128
SUBLANES = 8
VMEM_LIMIT_BYTES = 56 * 1024 * 1024

TS_INPROJ = 512
TQ_ATTN = 512
TR_DFT = 512
TM_MID = 512
TM_ROWS = 512
TM_EXPERT = 512
ISSUE_UNROLL = 4
MID_CHAINS = 1


def _cparams(sem):
    return pltpu.CompilerParams(dimension_semantics=sem, vmem_limit_bytes=VMEM_LIMIT_BYTES)


def _rms(x, g):
    return x * lax.rsqrt(jnp.mean(x * x, axis=-1, keepdims=True) + EPS) * g


def _row_copy(src_rows, dst_rows, sem):
    return pltpu.make_async_copy(src_rows, dst_rows, sem)


def _two_group_maps(n_prompt):
    def prompt_b(b):
        return jnp.minimum(b, n_prompt - 1)

    def sample_b(b):
        return jnp.maximum(b - n_prompt, 0)
    return prompt_b, sample_b


def _memkv_kernel(mp_ref, ms_ref, g_ref, w_ref, o_ref, *, n_prompt):
    b = pl.program_id(0)
    m = jnp.where(b < n_prompt, mp_ref[0], ms_ref[0])
    mn = _rms(m, g_ref[...]).astype(BF16)
    o_ref[0] = jnp.dot(mn, w_ref[...], preferred_element_type=F32).astype(BF16)


def _memkv(mem_p, mem_s, g_mem, w_kv):
    n_prompt, n_mem, d = mem_p.shape
    nb = n_prompt + mem_s.shape[0]
    pb, sb = _two_group_maps(n_prompt)
    return pl.pallas_call(
        functools.partial(_memkv_kernel, n_prompt=n_prompt),
        out_shape=jax.ShapeDtypeStruct((nb, n_mem, w_kv.shape[1]), BF16),
        grid=(nb,),
        in_specs=[
            pl.BlockSpec((1, n_mem, d), lambda b: (pb(b), 0, 0)),
            pl.BlockSpec((1, n_mem, d), lambda b: (sb(b), 0, 0)),
            pl.BlockSpec((1, d), lambda b: (0, 0)),
            pl.BlockSpec(w_kv.shape, lambda b: (0, 0)),
        ],
        out_specs=pl.BlockSpec((1, n_mem, w_kv.shape[1]), lambda b: (b, 0, 0)),
        compiler_params=_cparams(("arbitrary",)),
        name="memkv",
    )(mem_p, mem_s, g_mem.reshape(1, d), w_kv.astype(BF16))


_ROT_W = ATTN_WIDTH + 2 * KV_WIDTH


def _inproj_kernel(xp_ref, xs_ref, g_ref, w_ref, cos_ref, sin_ref, bd_ref,
                   q_ref, kv_ref, ab_ref, *, n_prompt):
    b = pl.program_id(1)
    x = jnp.where(b < n_prompt, xp_ref[0], xs_ref[0])
    h = _rms(x, g_ref[...]).astype(BF16)
    z = jnp.dot(h, w_ref[...], preferred_element_type=F32)
    cos = cos_ref[...]
    sin = sin_ref[...]
    lane = lax.broadcasted_iota(I32, cos.shape, 1) & (HEAD_DIM - 1)
    first_half = lane < ROPE_DIM // 2
    rot = []
    for c in range(_ROT_W // LANES):
        zc = z[:, c * LANES:(c + 1) * LANES]
        partner = jnp.where(first_half,
                            pltpu.roll(zc, LANES - ROPE_DIM // 2, axis=1),
                            pltpu.roll(zc, ROPE_DIM // 2, axis=1))
        rot.append(zc * cos + partner * sin)
    nq = ATTN_WIDTH // LANES
    q_ref[0] = (jnp.concatenate(rot[:nq], axis=1) * (LOG2E * HEAD_DIM ** -0.5)).astype(BF16)
    v_off = _ROT_W
    u_off = _ROT_W + 2 * KV_WIDTH
    kv_ref[0] = jnp.concatenate(rot[nq:] + [z[:, v_off:u_off]], axis=1).astype(BF16)
    half = FOURIER_WIDTH // 2
    u = z[:, u_off:].astype(BF16)
    r0 = jnp.dot(u[:, :half], bd_ref[...], preferred_element_type=F32)
    r1 = jnp.dot(u[:, half:], bd_ref[...], preferred_element_type=F32)
    ab_ref[0] = jnp.concatenate([r0[:, :half], r1[:, :half], r0[:, half:], r1[:, half:]],
                                axis=1).astype(BF16)


def _rope_tables(seq):
    half = ROPE_DIM // 2
    inv_freq = ROPE_THETA ** (-(jnp.arange(half, dtype=F32) * 2.0) / ROPE_DIM)
    ang = jnp.arange(seq).astype(F32)[:, None] * inv_freq[None, :]
    cos, sin = jnp.cos(ang), jnp.sin(ang)
    rest = HEAD_DIM - ROPE_DIM
    cos_h = jnp.concatenate([cos, cos, jnp.ones((seq, rest), F32)], axis=1)
    sin_h = jnp.concatenate([-sin, sin, jnp.zeros((seq, rest), F32)], axis=1)
    reps = LANES // HEAD_DIM
    return jnp.tile(cos_h, (1, reps)), jnp.tile(sin_h, (1, reps))


def _channel_dft_tables():
    n = FOURIER_GROUP
    idx = np.arange(n)
    ang = 2.0 * np.pi * ((idx[:, None] * idx[None, :]) % n) / n
    eye = np.eye(FOURIER_WIDTH // (2 * n))
    return jnp.asarray(np.concatenate([np.kron(eye, np.cos(ang)), np.kron(eye, np.sin(ang))], axis=1), BF16)


def _inproj(x_p, x_s, g_mix, w_in):
    n_prompt, seq, d = x_p.shape
    nb = n_prompt + x_s.shape[0]
    ts = min(TS_INPROJ, seq)
    o1, o2, o3 = ATTN_WIDTH, ATTN_WIDTH + KV_WIDTH, ATTN_WIDTH + 2 * KV_WIDTH
    swap = lambda w: jnp.concatenate([w[:, HEAD_DIM:], w[:, :HEAD_DIM]], axis=1)
    wk, wv = w_in[:, o1:o2], w_in[:, o2:o3]
    w_ext = jnp.concatenate([w_in[:, :o1], wk, swap(wk), wv, swap(wv), w_in[:, o3:]], axis=1).astype(BF16)
    cos_t, sin_t = _rope_tables(seq)
    bd = _channel_dft_tables()
    pb, sb = _two_group_maps(n_prompt)
    kvw = 4 * KV_WIDTH
    return pl.pallas_call(
        functools.partial(_inproj_kernel, n_prompt=n_prompt),
        out_shape=(jax.ShapeDtypeStruct((nb, seq, ATTN_WIDTH), BF16),
                   jax.ShapeDtypeStruct((nb, seq, kvw), BF16),
                   jax.ShapeDtypeStruct((nb, seq, 2 * FOURIER_WIDTH), BF16)),
        grid=(seq // ts, nb),
        in_specs=[
            pl.BlockSpec((1, ts, d), lambda i, b: (pb(b), i, 0)),
            pl.BlockSpec((1, ts, d), lambda i, b: (sb(b), i, 0)),
            pl.BlockSpec((1, d), lambda i, b: (0, 0)),
            pl.BlockSpec(w_ext.shape, lambda i, b: (0, 0)),
            pl.BlockSpec((ts, LANES), lambda i, b: (i, 0)),
            pl.BlockSpec((ts, LANES), lambda i, b: (i, 0)),
            pl.BlockSpec(bd.shape, lambda i, b: (0, 0)),
        ],
        out_specs=(pl.BlockSpec((1, ts, ATTN_WIDTH), lambda i, b: (b, i, 0)),
                   pl.BlockSpec((1, ts, kvw), lambda i, b: (b, i, 0)),
                   pl.BlockSpec((1, ts, 2 * FOURIER_WIDTH), lambda i, b: (b, i, 0))),
        compiler_params=_cparams(("arbitrary", "arbitrary")),
        name="inproj",
    )(x_p, x_s, g_mix.reshape(1, d), w_ext, cos_t, sin_t, bd)


def _swa_kernel(sink_ref, q_ref, kvm_ref, kvp_ref, kvn_ref, bias_ref, o_ref, *, nblk):
    i = pl.program_id(1)
    last = pl.num_programs(1) - 1
    kvw = jnp.concatenate([kvp_ref[0], kvm_ref[0], kvn_ref[0]], axis=0)
    k, ksw, v, vsw = [kvw[:, c * LANES:(c + 1) * LANES] for c in range(4)]
    lo_kv = lax.broadcasted_iota(I32, k.shape, 1) < HEAD_DIM
    k_dup = (jnp.where(lo_kv, k, ksw), jnp.where(lo_kv, ksw, k))
    v_dup = (jnp.where(lo_kv, v, vsw), jnp.where(lo_kv, vsw, v))
    lo_q = lax.broadcasted_iota(I32, (WINDOW, LANES), 1) < HEAD_DIM
    row = lax.broadcasted_iota(I32, (4 * WINDOW, 1), 0)
    for j in range(nblk):
        bias = bias_ref[1]
        if j == 0:
            bias = jnp.where(i == 0, bias_ref[0], bias)
        if j == nblk - 1:
            bias = jnp.where(i == last, bias_ref[2], bias)
        bias4 = jnp.concatenate([bias] * 4, axis=0)
        rows = slice(j * WINDOW, (j + 1) * WINDOW)
        win = slice(j * WINDOW, (j + 3) * WINDOW)
        for kvh in range(N_KV_HEADS):
            c0 = 2 * kvh
            qa = q_ref[0, rows, c0 * LANES:(c0 + 1) * LANES]
            qb = q_ref[0, rows, (c0 + 1) * LANES:(c0 + 2) * LANES]
            zero = jnp.zeros_like(qa)
            q4 = jnp.concatenate([jnp.where(lo_q, qa, zero), jnp.where(lo_q, zero, qa),
                                  jnp.where(lo_q, qb, zero), jnp.where(lo_q, zero, qb)], axis=0)
            s = lax.dot_general(q4, k_dup[kvh][win], (((1,), (1,)), ((), ())),
                                preferred_element_type=F32) + bias4
            h0 = 4 * kvh
            sink = jnp.where(row < WINDOW, sink_ref[h0],
                             jnp.where(row < 2 * WINDOW, sink_ref[h0 + 1],
                                       jnp.where(row < 3 * WINDOW, sink_ref[h0 + 2], sink_ref[h0 + 3])))
            m = jnp.maximum(jnp.max(s, axis=-1, keepdims=True), sink)
            e = jnp.exp2(s - m)
            den = jnp.sum(e, axis=-1, keepdims=True) + jnp.exp2(sink - m)
            o = jnp.dot(e.astype(BF16), v_dup[kvh][win], preferred_element_type=F32) * (1.0 / den)
            o_ref[0, rows, c0 * LANES:(c0 + 1) * LANES] = jnp.where(
                lo_q, o[:WINDOW], o[WINDOW:2 * WINDOW]).astype(o_ref.dtype)
            o_ref[0, rows, (c0 + 1) * LANES:(c0 + 2) * LANES] = jnp.where(
                lo_q, o[2 * WINDOW:3 * WINDOW], o[3 * WINDOW:]).astype(o_ref.dtype)


def _swa_bias():
    qi = np.arange(WINDOW)[:, None]
    c = np.arange(3 * WINDOW)[None, :]
    band = np.abs(c - WINDOW - qi) <= WINDOW
    first = band & (c >= WINDOW)
    lastb = band & (c < 2 * WINDOW)
    tab = np.stack([first, band, lastb]).astype(np.float32)
    return jnp.asarray((1.0 - tab) * NEG_INF, F32)


def _swa(q, kv, sinks):
    nb, seq, _ = q.shape
    tq = min(TQ_ATTN, seq)
    nblk = tq // WINDOW
    nseq_blk = seq // WINDOW
    kvw = kv.shape[-1]
    assert nblk >= 2 and seq % tq == 0, "first/last window masks are applied to distinct query blocks"
    bias = _swa_bias()
    return pl.pallas_call(
        functools.partial(_swa_kernel, nblk=nblk),
        out_shape=jax.ShapeDtypeStruct((nb, seq, ATTN_WIDTH), BF16),
        grid=(nb, seq // tq),
        in_specs=[
            pl.BlockSpec(memory_space=pltpu.SMEM),
            pl.BlockSpec((1, tq, ATTN_WIDTH), lambda b, i: (b, i, 0)),
            pl.BlockSpec((1, tq, kvw), lambda b, i: (b, i, 0)),
            pl.BlockSpec((1, WINDOW, kvw), lambda b, i: (b, jnp.maximum(i * nblk - 1, 0), 0)),
            pl.BlockSpec((1, WINDOW, kvw), lambda b, i: (b, jnp.minimum((i + 1) * nblk, nseq_blk - 1), 0)),
            pl.BlockSpec(bias.shape, lambda b, i: (0, 0, 0)),
        ],
        out_specs=pl.BlockSpec((1, tq, ATTN_WIDTH), lambda b, i: (b, i, 0)),
        compiler_params=_cparams(("arbitrary", "arbitrary")),
        name="swa",
    )(sinks.astype(F32) * LOG2E, q, kv, kv, kv, bias)


_DFT_PAD_ROWS = 16


def _seqdft_kernel(ab_ref, ct_hbm, st_hbm, perm_ref, o_ref, ct, st, aebo, hbuf, sem, *, scale, tk):
    n = ab_ref.shape[1]
    m = n // 2
    fw = FOURIER_WIDTH
    nblk = m // LANES

    @pl.when(pl.program_id(0) == 0)
    def _():
        copies = [pltpu.make_async_copy(ct_hbm, ct, sem.at[0]), pltpu.make_async_copy(st_hbm, st, sem.at[1])]
        for c in copies:
            c.start()
        for c in copies:
            c.wait()

    perm = perm_ref[...]
    for blk in range(nblk):
        mirrored = [ab_ref[0, n - LANES * (blk + 1):n - LANES * blk, :]]
        if blk > 0:
            mirrored.append(ab_ref[0, n - LANES * blk:n - LANES * (blk - 1), :])
        else:
            mirrored.append(jnp.zeros((LANES, 2 * fw), BF16))
        r = jnp.dot(perm, jnp.concatenate(mirrored, axis=0), preferred_element_type=F32)
        cur = ab_ref[0, LANES * blk:LANES * (blk + 1), :].astype(F32)
        aebo[LANES * blk:LANES * (blk + 1), :] = jnp.concatenate(
            [cur[:, :fw] + r[:, :fw], cur[:, fw:] - r[:, fw:]], axis=1).astype(BF16)

    a_mid = ab_ref[0, m:m + _DFT_PAD_ROWS, :fw][0:1, :].astype(F32)

    def pq(rows, nrows):
        p = jnp.dot(ct[rows, :], aebo[:, :fw], preferred_element_type=F32)
        q = jnp.dot(st[rows, :], aebo[:, fw:], preferred_element_type=F32)
        odd = (lax.broadcasted_iota(I32, (nrows, fw), 0) & 1) == 1
        return p + jnp.where(odd, -a_mid, a_mid), q

    for kt in range(m // tk):
        rows = slice(kt * tk, (kt + 1) * tk)
        p, q = pq(rows, tk)
        o_ref[0, rows, :] = ((p - q) * scale).astype(o_ref.dtype)
        hbuf[rows, :] = p + q
    p, q = pq(slice(m, m + _DFT_PAD_ROWS), _DFT_PAD_ROWS)
    hbuf[m:m + _DFT_PAD_ROWS, :] = p + q
    hbuf[m + _DFT_PAD_ROWS:, :] = jnp.zeros((LANES - _DFT_PAD_ROWS, fw), F32)
    for c in range(nblk):
        win = hbuf[LANES * c:LANES * (c + 2), :].astype(BF16)
        blk_out = jnp.dot(perm, win, preferred_element_type=F32)
        o_ref[0, n - LANES * (c + 1):n - LANES * c, :] = (blk_out * scale).astype(o_ref.dtype)


def _seq_dft_tables(seq):
    m = seq // 2
    k = np.arange(m + _DFT_PAD_ROWS)[:, None]
    ang = 2.0 * np.pi * ((k * np.arange(m)[None, :]) % seq) / seq
    return jnp.asarray(np.cos(ang), BF16), jnp.asarray(np.sin(ang), BF16)


def _seqdft(ab):
    nb, seq, _ = ab.shape
    m = seq // 2
    tk = min(TR_DFT, m)
    assert m % LANES == 0 and m % tk == 0 and m % 64 == 0
    ct, st = _seq_dft_tables(seq)
    perm = np.zeros((LANES, 2 * LANES), np.float32)
    perm[np.arange(LANES), LANES - np.arange(LANES)] = 1.0
    scale = float(1.0 / np.sqrt(seq * FOURIER_GROUP))
    return pl.pallas_call(
        functools.partial(_seqdft_kernel, scale=scale, tk=tk),
        out_shape=jax.ShapeDtypeStruct((nb, seq, FOURIER_WIDTH), BF16),
        grid=(nb,),
        in_specs=[
            pl.BlockSpec((1, seq, 2 * FOURIER_WIDTH), lambda b: (b, 0, 0)),
            pl.BlockSpec(memory_space=pl.ANY),
            pl.BlockSpec(memory_space=pl.ANY),
            pl.BlockSpec(perm.shape, lambda b: (0, 0)),
        ],
        out_specs=pl.BlockSpec((1, seq, FOURIER_WIDTH), lambda b: (b, 0, 0)),
        scratch_shapes=[pltpu.VMEM(ct.shape, BF16), pltpu.VMEM(st.shape, BF16),
                        pltpu.VMEM((m, 2 * FOURIER_WIDTH), BF16),
                        pltpu.VMEM((m + LANES, FOURIER_WIDTH), F32),
                        pltpu.SemaphoreType.DMA((2,))],
        compiler_params=_cparams(("arbitrary",)),
        name="seqdft",
    )(ab, ct, st, jnp.asarray(perm, BF16))


def _mid_kernel(xp_ref, xs_ref, a_ref, f_ref, kv_ref, gao_ref, gfo_ref, wout_ref, gc_ref, wq_ref,
                wo_ref, gm_ref, wr_ref, br_ref, tri_ref,
                x2_ref, meta_ref, gate_ref, cnt_ref, slots_ref,
                cnt_sc, stage, posv, pos_sm, cntv, cnt_sm, zbuf, rsem, psem, zsem, *, n_prompt, cap, te):
    b = pl.program_id(0)
    i = pl.program_id(1)
    tm = a_ref.shape[1]
    step = b * pl.num_programs(1) + i
    nsteps = pl.num_programs(0) * pl.num_programs(1)
    tile_rows = tm * SUBLANES
    dump_base = N_EXPERTS * cap

    def wait_scatter(sem_idx):
        for _ in range(TOP_K):
            _row_copy(stage.at[0], slots_ref.at[pl.ds(0, tile_rows), :], rsem.at[sem_idx]).wait()

    @pl.when(step == 0)
    def _():
        cnt_sc[...] = jnp.zeros_like(cnt_sc)
        stage[2] = jnp.zeros((tile_rows, LANES), F32)

        def init(t, carry):
            for k in range(TOP_K):
                pos_sm[1, k, t] = dump_base + k * tm + t
            return carry

        lax.fori_loop(0, tm, init, 0)

    @pl.when(step >= 2)
    def _():
        wait_scatter(step % 2)

    prev_stage = (step + 2) % 3
    prev_pos = (step + 1) % 2
    for t in range(tm):
        for k in range(TOP_K):
            r0 = pl.multiple_of(pos_sm[prev_pos, k, t] * SUBLANES, SUBLANES)
            _row_copy(stage.at[prev_stage, pl.ds(t * SUBLANES, SUBLANES), :],
                      slots_ref.at[pl.ds(r0, SUBLANES), :], rsem.at[step % 2]).start(priority=k % 2)
    cur_stage = step % 3

    def token_chain(r0, nr):
        rows = slice(r0, r0 + nr)
        x = jnp.where(b < n_prompt, xp_ref[0, rows, :], xs_ref[0, rows, :])
        an = _rms(a_ref[0, rows, :].astype(F32), gao_ref[...]).astype(BF16)
        fn = _rms(f_ref[0, rows, :].astype(F32), gfo_ref[...]).astype(BF16)
        x1 = (x + jnp.dot(an, wout_ref[:ATTN_WIDTH, :], preferred_element_type=F32)
              + jnp.dot(fn, wout_ref[ATTN_WIDTH:, :], preferred_element_type=F32))

        h2 = _rms(x1, gc_ref[...]).astype(BF16)
        qc = (jnp.dot(h2, wq_ref[...], preferred_element_type=F32)
              * (LOG2E * CROSS_HEAD_DIM ** -0.5)).astype(BF16)
        heads = []
        for hd in range(N_CROSS_HEADS):
            cols = slice(hd * CROSS_HEAD_DIM, (hd + 1) * CROSS_HEAD_DIM)
            vcols = slice(CROSS_WIDTH + hd * CROSS_HEAD_DIM, CROSS_WIDTH + (hd + 1) * CROSS_HEAD_DIM)
            s = lax.dot_general(qc[:, cols], kv_ref[0, :, cols], (((1,), (1,)), ((), ())),
                                preferred_element_type=F32)
            e = jnp.exp2(s - jnp.max(s, axis=-1, keepdims=True))
            den = jnp.sum(e, axis=-1, keepdims=True)
            heads.append(jnp.dot(e.astype(BF16), kv_ref[0, :, vcols], preferred_element_type=F32) * (1.0 / den))
        oc = jnp.concatenate(heads, axis=1).astype(BF16)
        x2 = x1 + jnp.dot(oc, wo_ref[...], preferred_element_type=F32)
        x2_ref[0, rows, :] = x2

        h3 = _rms(x2, gm_ref[...])
        for sl in range(SUBLANES):
            stage[cur_stage, pl.ds(r0 * SUBLANES + sl, nr, stride=SUBLANES), :] = h3[:, sl * LANES:(sl + 1) * LANES]

        hi = h3.astype(BF16)
        lo = (h3 - hi.astype(F32)).astype(BF16)
        hw = jnp.dot(hi, wr_ref[...], preferred_element_type=F32)
        return (hw[:, :LANES] + hw[:, LANES:]
                + jnp.dot(lo, wr_ref[:, :LANES], preferred_element_type=F32) + br_ref[...])

    nr = tm // MID_CHAINS
    logits = jnp.concatenate([token_chain(c * nr, nr) for c in range(MID_CHAINS)], axis=0)
    lt = logits.T[:N_EXPERTS, :]
    eidx = lax.broadcasted_iota(I32, lt.shape, 0)
    vals, idxs, sels = [], [], []
    for _ in range(TOP_K):
        m = jnp.max(lt, axis=0, keepdims=True)
        idx = jnp.min(jnp.where(lt == m, eidx, N_EXPERTS), axis=0, keepdims=True)
        sel = eidx == idx
        lt = jnp.where(sel, -jnp.inf, lt)
        vals.append(m)
        idxs.append(idx)
        sels.append(sel)
    ex = [jnp.exp(vk - vals[0]) for vk in vals]
    inv = 1.0 / (ex[0] + ex[1] + ex[2] + ex[3])
    zrow = jnp.zeros_like(ex[0])
    gate_ref[...] = jnp.concatenate([e_ * inv for e_ in ex] + [zrow] * (SUBLANES - TOP_K), axis=0)

    onehot = jnp.zeros(lt.shape, F32)
    for sel in sels:
        onehot = onehot + sel.astype(F32)
    before = cnt_sc[:, 0:1] + jnp.dot(onehot.astype(BF16), tri_ref[...], preferred_element_type=F32)
    ranks = [jnp.sum(jnp.where(sel, before, 0.0), axis=0, keepdims=True).astype(I32) for sel in sels]
    meta_ref[...] = jnp.concatenate(idxs + ranks, axis=0)
    cnt_sc[...] = cnt_sc[...] + jnp.sum(onehot, axis=1, keepdims=True)
    cnt_ref[...] = cnt_sc[...]

    posv[...] = jnp.concatenate([ix * cap + rk for ix, rk in zip(idxs, ranks)], axis=0)
    to_smem = pltpu.make_async_copy(posv, pos_sm.at[step % 2], psem)
    to_smem.start()
    to_smem.wait()

    @pl.when(step == nsteps - 1)
    def _():
        @pl.when(step >= 1)
        def _():
            wait_scatter((step + 1) % 2)

        def issue(tb, carry):
            for u in range(ISSUE_UNROLL):
                t = tb * ISSUE_UNROLL + u
                for k in range(TOP_K):
                    r0 = pl.multiple_of(pos_sm[step % 2, k, t] * SUBLANES, SUBLANES)
                    _row_copy(stage.at[cur_stage, pl.ds(pl.multiple_of(t * SUBLANES, SUBLANES), SUBLANES), :],
                              slots_ref.at[pl.ds(r0, SUBLANES), :], rsem.at[(step + 1) % 2]).start(priority=k % 2)
            return carry

        lax.fori_loop(0, tm // ISSUE_UNROLL, issue, 0)
        wait_scatter(step % 2)
        wait_scatter((step + 1) % 2)

        cntv[...] = cnt_sc[...].astype(I32)
        counts_to_smem = pltpu.make_async_copy(cntv, cnt_sm, psem)
        counts_to_smem.start()
        counts_to_smem.wait()
        zbuf[...] = jnp.zeros_like(zbuf)
        zrows = zbuf.shape[0] // SUBLANES
        bits = [1 << j for j in range(zrows.bit_length() - 1, -1, -1)]
        assert sum(bits) == 2 * zrows - 1 and zrows * 2 == te

        def fill(start):
            for e in range(N_EXPERTS):
                cnt = cnt_sm[e, 0]
                pad = (te - cnt % te) % te
                row = e * cap + cnt
                for bit in bits:
                    @pl.when((pad & bit) != 0)
                    def _(row=row, bit=bit):
                        cp = _row_copy(zbuf.at[pl.ds(0, bit * SUBLANES), :],
                                       slots_ref.at[pl.ds(pl.multiple_of(row * SUBLANES, SUBLANES),
                                                          bit * SUBLANES), :], zsem)
                        cp.start() if start else cp.wait()
                    row = row + (pad & bit)

        fill(True)
        fill(False)


def _mid(x_p, x_s, a, f, kvc, g_attn_out, g_four_out, w_out, g_cross, w_q, w_o, g_moe, w_router, b_router,
         cap, te):
    n_prompt, seq, d = x_p.shape
    nb = a.shape[0]
    tm = min(TM_MID, seq)
    nt = seq // tm
    n_mem = kvc.shape[1]
    pb, sb = _two_group_maps(n_prompt)
    wr = jnp.zeros((d, LANES), F32).at[:, :N_EXPERTS].set(w_router)
    wr_hi = wr.astype(BF16)
    wr_lo = (wr - wr_hi.astype(F32)).astype(BF16)
    br = jnp.zeros((1, LANES), F32).at[0, :N_EXPERTS].set(b_router)
    tri = jnp.asarray(np.triu(np.ones((tm, tm), np.float32), 1), BF16)
    full = lambda arr: pl.BlockSpec(arr.shape, lambda b, i: (0,) * arr.ndim)
    row = lambda v: v.reshape(1, -1)
    args = [x_p, x_s, a, f, kvc, row(g_attn_out), row(g_four_out), w_out.astype(BF16), row(g_cross),
            w_q.astype(BF16), w_o.astype(BF16), row(g_moe), jnp.concatenate([wr_hi, wr_lo], axis=1), br, tri]
    in_specs = [
        pl.BlockSpec((1, tm, d), lambda b, i: (pb(b), i, 0)),
        pl.BlockSpec((1, tm, d), lambda b, i: (sb(b), i, 0)),
        pl.BlockSpec((1, tm, ATTN_WIDTH), lambda b, i: (b, i, 0)),
        pl.BlockSpec((1, tm, FOURIER_WIDTH), lambda b, i: (b, i, 0)),
        pl.BlockSpec((1, n_mem, 2 * CROSS_WIDTH), lambda b, i: (b, 0, 0)),
    ] + [full(v) for v in args[5:]]
    tok = nb * seq
    slot_rows = N_EXPERTS * cap + TOP_K * tm
    return pl.pallas_call(
        functools.partial(_mid_kernel, n_prompt=n_prompt, cap=cap, te=te),
        out_shape=(jax.ShapeDtypeStruct((nb, seq, d), F32),
                   jax.ShapeDtypeStruct((SUBLANES, tok), I32),
                   jax.ShapeDtypeStruct((SUBLANES, tok), F32),
                   jax.ShapeDtypeStruct((N_EXPERTS, LANES), F32),
                   jax.ShapeDtypeStruct((slot_rows * SUBLANES, LANES), F32)),
        grid=(nb, nt),
        in_specs=in_specs,
        out_specs=(pl.BlockSpec((1, tm, d), lambda b, i: (b, i, 0)),
                   pl.BlockSpec((SUBLANES, tm), lambda b, i: (0, b * nt + i)),
                   pl.BlockSpec((SUBLANES, tm), lambda b, i: (0, b * nt + i)),
                   pl.BlockSpec((N_EXPERTS, LANES), lambda b, i: (0, 0)),
                   pl.BlockSpec(memory_space=pl.ANY)),
        scratch_shapes=[pltpu.VMEM((N_EXPERTS, LANES), F32),
                        pltpu.VMEM((3, tm * SUBLANES, LANES), F32),
                        pltpu.VMEM((TOP_K, tm), I32), pltpu.SMEM((2, TOP_K, tm), I32),
                        pltpu.VMEM((N_EXPERTS, LANES), I32), pltpu.SMEM((N_EXPERTS, LANES), I32),
                        pltpu.VMEM((te // 2 * SUBLANES, LANES), F32),
                        pltpu.SemaphoreType.DMA((2,)), pltpu.SemaphoreType.DMA, pltpu.SemaphoreType.DMA],
        compiler_params=_cparams(("arbitrary", "arbitrary")),
        name="mid",
    )(*args)


def _expert_kernel(te_ref, ti_ref, to_ref, nu_ref, xs_ref, wgu_ref, bgu_ref, wd_ref, bd_ref, ys_ref, wgu_bf, wd_bf):
    i = pl.program_id(0)
    te = xs_ref.shape[0] // SUBLANES
    dff = wd_ref.shape[1]

    @pl.when((i == 0) | (te_ref[i] != te_ref[jnp.maximum(i - 1, 0)]))
    def _():
        wgu_bf[...] = wgu_ref[0].astype(BF16)
        wd_bf[...] = wd_ref[0].astype(BF16)

    @pl.when(i < nu_ref[0])
    def _():
        x = jnp.concatenate([xs_ref[pl.ds(sl, te, stride=SUBLANES), :] for sl in range(SUBLANES)],
                            axis=1).astype(BF16)
        gu = jnp.dot(x, wgu_bf[...], preferred_element_type=F32) + bgu_ref[0]
        gate = jnp.minimum(gu[:, :dff], SWIGLU_LIMIT)
        up = jnp.clip(gu[:, dff:], -SWIGLU_LIMIT, SWIGLU_LIMIT)
        act = (up + 1.0) * (gate * (1.0 / (1.0 + jnp.exp(-SWIGLU_ALPHA * gate))))
        y = jnp.dot(act.astype(BF16), wd_bf[...], preferred_element_type=F32) + bd_ref[0]
        for sl in range(SUBLANES):
            ys_ref[pl.ds(sl, te, stride=SUBLANES), :] = y[:, sl * LANES:(sl + 1) * LANES]


def _experts(xs, tile_expert, tile_in, tile_out, n_used, w_gate_up, b_gate_up, w_down, b_down, te):
    ne, d, two_dff = w_gate_up.shape
    dff = two_dff // 2
    ntile = tile_expert.shape[0]
    return pl.pallas_call(
        _expert_kernel,
        out_shape=jax.ShapeDtypeStruct((ntile * te * SUBLANES, LANES), F32),
        grid_spec=pltpu.PrefetchScalarGridSpec(
            num_scalar_prefetch=4,
            grid=(ntile,),
            in_specs=[
                pl.BlockSpec((te * SUBLANES, LANES), lambda i, te_r, ti_r, to_r, nu_r: (ti_r[i], 0)),
                pl.BlockSpec((1, d, two_dff), lambda i, te_r, ti_r, to_r, nu_r: (te_r[i], 0, 0)),
                pl.BlockSpec((1, 1, two_dff), lambda i, te_r, ti_r, to_r, nu_r: (te_r[i], 0, 0)),
                pl.BlockSpec((1, dff, d), lambda i, te_r, ti_r, to_r, nu_r: (te_r[i], 0, 0)),
                pl.BlockSpec((1, 1, d), lambda i, te_r, ti_r, to_r, nu_r: (te_r[i], 0, 0)),
            ],
            out_specs=pl.BlockSpec((te * SUBLANES, LANES), lambda i, te_r, ti_r, to_r, nu_r: (to_r[i], 0)),
            scratch_shapes=[pltpu.VMEM((d, two_dff), BF16), pltpu.VMEM((dff, d), BF16)],
        ),
        compiler_params=_cparams(("arbitrary",)),
        name="experts",
    )(tile_expert, tile_in, tile_out, n_used, xs, w_gate_up, b_gate_up.reshape(ne, 1, two_dff),
      w_down, b_down.reshape(ne, 1, d))


def _combine_kernel(pos_a, pos_b, pos_n, x2_ref, gate_ref, g_ref, ys_ref, o_ref, buf_a, buf_b, sem, *, tm):
    step = pl.program_id(0) * pl.num_programs(1) + pl.program_id(1)
    nsteps = pl.num_programs(0) * pl.num_programs(1)
    slot_rows = TOP_K * tm * SUBLANES

    def wait_rows(buf, s):
        _row_copy(ys_ref.at[pl.ds(0, slot_rows), :], buf, sem.at[s]).wait()

    @pl.when(step == 0)
    def _():
        def issue(tb, carry):
            for u in range(ISSUE_UNROLL):
                t = tb * ISSUE_UNROLL + u
                for k in range(TOP_K):
                    r0 = pl.multiple_of(pos_a[0, 0, k * tm + t] * SUBLANES, SUBLANES)
                    d0 = pl.multiple_of((k * tm + t) * SUBLANES, SUBLANES)
                    _row_copy(ys_ref.at[pl.ds(r0, SUBLANES), :], buf_a.at[pl.ds(d0, SUBLANES), :],
                              sem.at[0]).start(priority=k % 2)
            return carry

        lax.fori_loop(0, tm // ISSUE_UNROLL, issue, 0)

    def combine_tile(rows, buf, req_pos, req_buf, req_sem):
        x = x2_ref[0, rows, :]
        g = jnp.concatenate([gate_ref[:, rows], jnp.zeros((LANES - SUBLANES, tm), F32)], axis=0).T
        for k in range(TOP_K):
            for t in range(tm):
                r0 = pl.multiple_of(req_pos[0, 0, k * tm + t] * SUBLANES, SUBLANES)
                _row_copy(ys_ref.at[pl.ds(r0, SUBLANES), :],
                          req_buf.at[pl.ds((k * tm + t) * SUBLANES, SUBLANES), :],
                          sem.at[req_sem]).start(priority=t % 2)
            yk = jnp.concatenate(
                [buf[pl.ds(k * tm * SUBLANES + sl, tm, stride=SUBLANES), :] for sl in range(SUBLANES)], axis=1)
            x = x + g[:, k:k + 1] * yk
        o_ref[0, rows, :] = _rms(x, g_ref[...])

    wait_rows(buf_a, 0)
    combine_tile(slice(0, tm), buf_a, pos_b, buf_b, 1)
    wait_rows(buf_b, 1)
    combine_tile(slice(tm, 2 * tm), buf_b, pos_n, buf_a, 0)

    @pl.when(step == nsteps - 1)
    def _():
        wait_rows(buf_a, 0)


def _combine(x2, gates, pos3, ys, g_final, b0, nbatch):
    _, seq, d = x2.shape
    tm = pos3.shape[2] // TOP_K
    nt = seq // (2 * tm)
    assert seq % (2 * tm) == 0
    last = 2 * (b0 + nbatch) * nt - 2
    tile_a = lambda b, i: 2 * ((b0 + b) * nt + i)
    smem_tile = lambda fn: pl.BlockSpec((1, 1, TOP_K * tm), lambda b, i: (fn(b, i), 0, 0),
                                        memory_space=pltpu.SMEM)
    return pl.pallas_call(
        functools.partial(_combine_kernel, tm=tm),
        out_shape=jax.ShapeDtypeStruct((nbatch, seq, d), F32),
        grid_spec=pltpu.PrefetchScalarGridSpec(
            num_scalar_prefetch=0,
            grid=(nbatch, nt),
            in_specs=[
                smem_tile(tile_a),
                smem_tile(lambda b, i: tile_a(b, i) + 1),
                smem_tile(lambda b, i: jnp.minimum(tile_a(b, i) + 2, last)),
                pl.BlockSpec((1, 2 * tm, d), lambda b, i: (b0 + b, i, 0)),
                pl.BlockSpec((SUBLANES, 2 * tm), lambda b, i: (0, (b0 + b) * nt + i)),
                pl.BlockSpec((1, d), lambda b, i: (0, 0)),
                pl.BlockSpec(memory_space=pl.ANY),
            ],
            out_specs=pl.BlockSpec((1, 2 * tm, d), lambda b, i: (b, i, 0)),
            scratch_shapes=[pltpu.VMEM((TOP_K * tm * SUBLANES, LANES), F32),
                            pltpu.VMEM((TOP_K * tm * SUBLANES, LANES), F32),
                            pltpu.SemaphoreType.DMA((2,))],
        ),
        compiler_params=_cparams(("arbitrary", "arbitrary")),
        name="combine",
    )(pos3, pos3, pos3, x2, gates, g_final.reshape(1, d), ys)


def _routing_plan(meta, cnt, te, tm_rows, cap):
    tok = meta.shape[1]
    idx, rank = meta[:TOP_K], meta[TOP_K:]
    counts = cnt[:, 0].astype(I32)
    tiles = (counts + te - 1) // te
    tile_end = jnp.cumsum(tiles)
    tile_start = tile_end - tiles
    offsets = tile_start * te
    n_used = tile_end[-1]
    pos = rank + jnp.sum(jnp.where(idx[None] == jnp.arange(N_EXPERTS, dtype=I32)[:, None, None],
                                   offsets[:, None, None], 0), axis=0)
    ntile_max = (tok * TOP_K) // te + N_EXPERTS
    t_ids = jnp.minimum(jnp.arange(ntile_max, dtype=I32), n_used - 1)
    tile_expert = jnp.sum((t_ids[:, None] >= tile_end[None, :]).astype(I32), axis=1)
    first = jnp.sum(jnp.where(tile_expert[:, None] == jnp.arange(N_EXPERTS, dtype=I32)[None, :],
                              tile_start[None, :], 0), axis=1)
    tile_in = tile_expert * (cap // te) + (t_ids - first)
    ntok_tiles = tok // tm_rows
    pos3 = pos.reshape(TOP_K, ntok_tiles, tm_rows).transpose(1, 0, 2).reshape(ntok_tiles, 1, TOP_K * tm_rows)
    return pos3, tile_expert, tile_in, t_ids, n_used.reshape(1)


def kernel(x_prompt, x_sample, mem_prompt, mem_sample, g_mix, w_in, sinks, g_attn_out, g_four_out, w_out,
           g_cross, g_mem, w_q_cross, w_kv_cross, w_o_cross, g_moe, w_router, b_router, w_gate_up,
           b_gate_up, w_down, b_down, g_final):
    n_prompt, seq, _ = x_prompt.shape
    n_sample = x_sample.shape[0]
    l = 0
    kvc = _memkv(mem_prompt, mem_sample, g_mem[l], w_kv_cross[l])
    q, kv, ab = _inproj(x_prompt, x_sample, g_mix[l], w_in[l])
    a = _swa(q, kv, sinks[l])
    f = _seqdft(ab)
    tok = (n_prompt + n_sample) * seq
    te = TM_EXPERT
    cap = -(-tok // te) * te
    x2, meta, gates, cnt, xs = _mid(x_prompt, x_sample, a, f, kvc, g_attn_out[l], g_four_out[l], w_out[l],
                                    g_cross[l], w_q_cross[l], w_o_cross[l], g_moe[l], w_router[l], b_router[l],
                                    cap, te)
    tm_rows = min(TM_ROWS, seq // 2)
    pos3, tile_expert, tile_in, tile_out, n_used = _routing_plan(meta, cnt, te, tm_rows, cap)
    ys = _experts(xs, tile_expert, tile_in, tile_out, n_used, w_gate_up[l], b_gate_up[l], w_down[l], b_down[l], te)
    y_p = _combine(x2, gates, pos3, ys, g_final, 0, n_prompt)
    y_s = _combine(x2, gates, pos3, ys, g_final, n_prompt, n_sample)
    return (y_p, y_s)
```

```python
import functools

import numpy as np
import jax
import jax.numpy as jnp
from jax import lax
from jax.experimental import pallas as pl
from jax.experimental.pallas import tpu as pltpu

F32 = jnp.float32
BF16 = jnp.bfloat16
I32 = jnp.int32

HEAD_DIM = 64
N_Q_HEADS = 8
N_KV_HEADS = 2
ATTN_WIDTH = N_Q_HEADS * HEAD_DIM
KV_WIDTH = N_KV_HEADS * HEAD_DIM
FOURIER_WIDTH = 512
FOURIER_GROUP = 64
WINDOW = 128
ROPE_DIM = 16
ROPE_THETA = 500000.0
N_CROSS_HEADS = 4
CROSS_HEAD_DIM = 128
CROSS_WIDTH = N_CROSS_HEADS * CROSS_HEAD_DIM
N_EXPERTS = 32
TOP_K = 4
SWIGLU_LIMIT = 7.0
SWIGLU_ALPHA = 1.702
EPS = 1e-5
NEG_INF = -1e30
LOG2E = 1.4426950408889634

LANES = 128
SUBLANES = 8
VMEM_LIMIT_BYTES = 56 * 1024 * 1024

TS_INPROJ = 1024
TQ_ATTN = 1024
TR_DFT = 512
TM_MID = 512
TM_ROWS = 512
TM_EXPERT = 512
ISSUE_UNROLL = 4
MID_CHAINS = 1


def _cparams(sem):
    return pltpu.CompilerParams(dimension_semantics=sem, vmem_limit_bytes=VMEM_LIMIT_BYTES)


def _rms(x, g):
    return x * lax.rsqrt(jnp.mean(x * x, axis=-1, keepdims=True) + EPS) * g


def _row_copy(src_rows, dst_rows, sem):
    return pltpu.make_async_copy(src_rows, dst_rows, sem)


def _two_group_maps(n_prompt):
    def prompt_b(b):
        return jnp.minimum(b, n_prompt - 1)

    def sample_b(b):
        return jnp.maximum(b - n_prompt, 0)
    return prompt_b, sample_b


def _memkv_kernel(mp_ref, ms_ref, g_ref, w_ref, o_ref, *, n_prompt):
    b = pl.program_id(0)
    m = jnp.where(b < n_prompt, mp_ref[0], ms_ref[0])
    mn = _rms(m, g_ref[...]).astype(BF16)
    o_ref[0] = jnp.dot(mn, w_ref[...], preferred_element_type=F32).astype(BF16)


def _memkv(mem_p, mem_s, g_mem, w_kv):
    n_prompt, n_mem, d = mem_p.shape
    nb = n_prompt + mem_s.shape[0]
    pb, sb = _two_group_maps(n_prompt)
    return pl.pallas_call(
        functools.partial(_memkv_kernel, n_prompt=n_prompt),
        out_shape=jax.ShapeDtypeStruct((nb, n_mem, w_kv.shape[1]), BF16),
        grid=(nb,),
        in_specs=[
            pl.BlockSpec((1, n_mem, d), lambda b: (pb(b), 0, 0)),
            pl.BlockSpec((1, n_mem, d), lambda b: (sb(b), 0, 0)),
            pl.BlockSpec((1, d), lambda b: (0, 0)),
            pl.BlockSpec(w_kv.shape, lambda b: (0, 0)),
        ],
        out_specs=pl.BlockSpec((1, n_mem, w_kv.shape[1]), lambda b: (b, 0, 0)),
        compiler_params=_cparams(("arbitrary",)),
        name="memkv",
    )(mem_p, mem_s, g_mem.reshape(1, d), w_kv.astype(BF16))


_ROT_W = ATTN_WIDTH + 2 * KV_WIDTH


def _inproj_kernel(xp_ref, xs_ref, g_ref, w_ref, cos_ref, sin_ref, bd_ref,
                   q_ref, kv_ref, ab_ref, *, n_prompt):
    b = pl.program_id(1)
    x = jnp.where(b < n_prompt, xp_ref[0], xs_ref[0])
    h = _rms(x, g_ref[...]).astype(BF16)
    z = jnp.dot(h, w_ref[...], preferred_element_type=F32)
    cos = cos_ref[...]
    sin = sin_ref[...]
    lane = lax.broadcasted_iota(I32, cos.shape, 1) & (HEAD_DIM - 1)
    first_half = lane < ROPE_DIM // 2
    rot = []
    for c in range(_ROT_W // LANES):
        zc = z[:, c * LANES:(c + 1) * LANES]
        partner = jnp.where(first_half,
                            pltpu.roll(zc, LANES - ROPE_DIM // 2, axis=1),
                            pltpu.roll(zc, ROPE_DIM // 2, axis=1))
        rot.append(zc * cos + partner * sin)
    nq = ATTN_WIDTH // LANES
    q_ref[0] = (jnp.concatenate(rot[:nq], axis=1) * (LOG2E * HEAD_DIM ** -0.5)).astype(BF16)
    v_off = _ROT_W
    u_off = _ROT_W + 2 * KV_WIDTH
    kv_ref[0] = jnp.concatenate(rot[nq:] + [z[:, v_off:u_off]], axis=1).astype(BF16)
    half = FOURIER_WIDTH // 2
    u = z[:, u_off:].astype(BF16)
    r0 = jnp.dot(u[:, :half], bd_ref[...], preferred_element_type=F32)
    r1 = jnp.dot(u[:, half:], bd_ref[...], preferred_element_type=F32)
    ab_ref[0] = jnp.concatenate([r0[:, :half], r1[:, :half], r0[:, half:], r1[:, half:]],
                                axis=1).astype(BF16)


def _rope_tables(seq):
    half = ROPE_DIM // 2
    inv_freq = ROPE_THETA ** (-(jnp.arange(half, dtype=F32) * 2.0) / ROPE_DIM)
    ang = jnp.arange(seq).astype(F32)[:, None] * inv_freq[None, :]
    cos, sin = jnp.cos(ang), jnp.sin(ang)
    rest = HEAD_DIM - ROPE_DIM
    cos_h = jnp.concatenate([cos, cos, jnp.ones((seq, rest), F32)], axis=1)
    sin_h = jnp.concatenate([-sin, sin, jnp.zeros((seq, rest), F32)], axis=1)
    reps = LANES // HEAD_DIM
    return jnp.tile(cos_h, (1, reps)), jnp.tile(sin_h, (1, reps))


def _channel_dft_tables():
    n = FOURIER_GROUP
    idx = np.arange(n)
    ang = 2.0 * np.pi * ((idx[:, None] * idx[None, :]) % n) / n
    eye = np.eye(FOURIER_WIDTH // (2 * n))
    return jnp.asarray(np.concatenate([np.kron(eye, np.cos(ang)), np.kron(eye, np.sin(ang))], axis=1), BF16)


def _inproj(x_p, x_s, g_mix, w_in):
    n_prompt, seq, d = x_p.shape
    nb = n_prompt + x_s.shape[0]
    ts = min(TS_INPROJ, seq)
    o1, o2, o3 = ATTN_WIDTH, ATTN_WIDTH + KV_WIDTH, ATTN_WIDTH + 2 * KV_WIDTH
    swap = lambda w: jnp.concatenate([w[:, HEAD_DIM:], w[:, :HEAD_DIM]], axis=1)
    wk, wv = w_in[:, o1:o2], w_in[:, o2:o3]
    w_ext = jnp.concatenate([w_in[:, :o1], wk, swap(wk), wv, swap(wv), w_in[:, o3:]], axis=1).astype(BF16)
    cos_t, sin_t = _rope_tables(seq)
    bd = _channel_dft_tables()
    pb, sb = _two_group_maps(n_prompt)
    kvw = 4 * KV_WIDTH
    return pl.pallas_call(
        functools.partial(_inproj_kernel, n_prompt=n_prompt),
        out_shape=(jax.ShapeDtypeStruct((nb, seq, ATTN_WIDTH), BF16),
                   jax.ShapeDtypeStruct((nb, seq, kvw), BF16),
                   jax.ShapeDtypeStruct((nb, seq, 2 * FOURIER_WIDTH), BF16)),
        grid=(seq // ts, nb),
        in_specs=[
            pl.BlockSpec((1, ts, d), lambda i, b: (pb(b), i, 0)),
            pl.BlockSpec((1, ts, d), lambda i, b: (sb(b), i, 0)),
            pl.BlockSpec((1, d), lambda i, b: (0, 0)),
            pl.BlockSpec(w_ext.shape, lambda i, b: (0, 0)),
            pl.BlockSpec((ts, LANES), lambda i, b: (i, 0)),
            pl.BlockSpec((ts, LANES), lambda i, b: (i, 0)),
            pl.BlockSpec(bd.shape, lambda i, b: (0, 0)),
        ],
        out_specs=(pl.BlockSpec((1, ts, ATTN_WIDTH), lambda i, b: (b, i, 0)),
                   pl.BlockSpec((1, ts, kvw), lambda i, b: (b, i, 0)),
                   pl.BlockSpec((1, ts, 2 * FOURIER_WIDTH), lambda i, b: (b, i, 0))),
        compiler_params=_cparams(("arbitrary", "arbitrary")),
        name="inproj",
    )(x_p, x_s, g_mix.reshape(1, d), w_ext, cos_t, sin_t, bd)


def _swa_kernel(sink_ref, q_ref, kvm_ref, kvp_ref, kvn_ref, bias_ref, o_ref, *, nblk):
    i = pl.program_id(1)
    last = pl.num_programs(1) - 1
    kvw = jnp.concatenate([kvp_ref[0], kvm_ref[0], kvn_ref[0]], axis=0)
    k, ksw, v, vsw = [kvw[:, c * LANES:(c + 1) * LANES] for c in range(4)]
    lo_kv = lax.broadcasted_iota(I32, k.shape, 1) < HEAD_DIM
    k_dup = (jnp.where(lo_kv, k, ksw), jnp.where(lo_kv, ksw, k))
    v_dup = (jnp.where(lo_kv, v, vsw), jnp.where(lo_kv, vsw, v))
    lo_q = lax.broadcasted_iota(I32, (WINDOW, LANES), 1) < HEAD_DIM
    row = lax.broadcasted_iota(I32, (4 * WINDOW, 1), 0)
    for j in range(nblk):
        bias = bias_ref[1]
        if j == 0:
            bias = jnp.where(i == 0, bias_ref[0], bias)
        if j == nblk - 1:
            bias = jnp.where(i == last, bias_ref[2], bias)
        bias4 = jnp.concatenate([bias] * 4, axis=0)
        rows = slice(j * WINDOW, (j + 1) * WINDOW)
        win = slice(j * WINDOW, (j + 3) * WINDOW)
        for kvh in range(N_KV_HEADS):
            c0 = 2 * kvh
            qa = q_ref[0, rows, c0 * LANES:(c0 + 1) * LANES]
            qb = q_ref[0, rows, (c0 + 1) * LANES:(c0 + 2) * LANES]
            zero = jnp.zeros_like(qa)
            q4 = jnp.concatenate([jnp.where(lo_q, qa, zero), jnp.where(lo_q, zero, qa),
                                  jnp.where(lo_q, qb, zero), jnp.where(lo_q, zero, qb)], axis=0)
            s = lax.dot_general(q4, k_dup[kvh][win], (((1,), (1,)), ((), ())),
                                preferred_element_type=F32) + bias4
            h0 = 4 * kvh
            sink = jnp.where(row < WINDOW, sink_ref[h0],
                             jnp.where(row < 2 * WINDOW, sink_ref[h0 + 1],
                                       jnp.where(row < 3 * WINDOW, sink_ref[h0 + 2], sink_ref[h0 + 3])))
            m = jnp.maximum(jnp.max(s, axis=-1, keepdims=True), sink)
            e = jnp.exp2(s - m)
            den = jnp.sum(e, axis=-1, keepdims=True) + jnp.exp2(sink - m)
            o = jnp.dot(e.astype(BF16), v_dup[kvh][win], preferred_element_type=F32) * (1.0 / den)
            o_ref[0, rows, c0 * LANES:(c0 + 1) * LANES] = jnp.where(
                lo_q, o[:WINDOW], o[WINDOW:2 * WINDOW]).astype(o_ref.dtype)
            o_ref[0, rows, (c0 + 1) * LANES:(c0 + 2) * LANES] = jnp.where(
                lo_q, o[2 * WINDOW:3 * WINDOW], o[3 * WINDOW:]).astype(o_ref.dtype)


def _swa_bias():
    qi = np.arange(WINDOW)[:, None]
    c = np.arange(3 * WINDOW)[None, :]
    band = np.abs(c - WINDOW - qi) <= WINDOW
    first = band & (c >= WINDOW)
    lastb = band & (c < 2 * WINDOW)
    tab = np.stack([first, band, lastb]).astype(np.float32)
    return jnp.asarray((1.0 - tab) * NEG_INF, F32)


def _swa(q, kv, sinks):
    nb, seq, _ = q.shape
    tq = min(TQ_ATTN, seq)
    nblk = tq // WINDOW
    nseq_blk = seq // WINDOW
    kvw = kv.shape[-1]
    assert nblk >= 2 and seq % tq == 0, "first/last window masks are applied to distinct query blocks"
    bias = _swa_bias()
    return pl.pallas_call(
        functools.partial(_swa_kernel, nblk=nblk),
        out_shape=jax.ShapeDtypeStruct((nb, seq, ATTN_WIDTH), BF16),
        grid=(nb, seq // tq),
        in_specs=[
            pl.BlockSpec(memory_space=pltpu.SMEM),
            pl.BlockSpec((1, tq, ATTN_WIDTH), lambda b, i: (b, i, 0)),
            pl.BlockSpec((1, tq, kvw), lambda b, i: (b, i, 0)),
            pl.BlockSpec((1, WINDOW, kvw), lambda b, i: (b, jnp.maximum(i * nblk - 1, 0), 0)),
            pl.BlockSpec((1, WINDOW, kvw), lambda b, i: (b, jnp.minimum((i + 1) * nblk, nseq_blk - 1), 0)),
            pl.BlockSpec(bias.shape, lambda b, i: (0, 0, 0)),
        ],
        out_specs=pl.BlockSpec((1, tq, ATTN_WIDTH), lambda b, i: (b, i, 0)),
        compiler_params=_cparams(("arbitrary", "arbitrary")),
        name="swa",
    )(sinks.astype(F32) * LOG2E, q, kv, kv, kv, bias)


_DFT_PAD_ROWS = 16


def _seqdft_kernel(ab_ref, ct_hbm, st_hbm, perm_ref, o_ref, ct, st, aebo, hbuf, sem, *, scale, tk):
    n = ab_ref.shape[1]
    m = n // 2
    fw = FOURIER_WIDTH
    nblk = m // LANES

    @pl.when(pl.program_id(0) == 0)
    def _():
        copies = [pltpu.make_async_copy(ct_hbm, ct, sem.at[0]), pltpu.make_async_copy(st_hbm, st, sem.at[1])]
        for c in copies:
            c.start()
        for c in copies:
            c.wait()

    perm = perm_ref[...]
    for blk in range(nblk):
        mirrored = [ab_ref[0, n - LANES * (blk + 1):n - LANES * blk, :]]
        if blk > 0:
            mirrored.append(ab_ref[0, n - LANES * blk:n - LANES * (blk - 1), :])
        else:
            mirrored.append(jnp.zeros((LANES, 2 * fw), BF16))
        r = jnp.dot(perm, jnp.concatenate(mirrored, axis=0), preferred_element_type=F32)
        cur = ab_ref[0, LANES * blk:LANES * (blk + 1), :].astype(F32)
        aebo[LANES * blk:LANES * (blk + 1), :] = jnp.concatenate(
            [cur[:, :fw] + r[:, :fw], cur[:, fw:] - r[:, fw:]], axis=1).astype(BF16)

    a_mid = ab_ref[0, m:m + _DFT_PAD_ROWS, :fw][0:1, :].astype(F32)

    def pq(rows, nrows):
        p = jnp.dot(ct[rows, :], aebo[:, :fw], preferred_element_type=F32)
        q = jnp.dot(st[rows, :], aebo[:, fw:], preferred_element_type=F32)
        odd = (lax.broadcasted_iota(I32, (nrows, fw), 0) & 1) == 1
        return p + jnp.where(odd, -a_mid, a_mid), q

    for kt in range(m // tk):
        rows = slice(kt * tk, (kt + 1) * tk)
        p, q = pq(rows, tk)
        o_ref[0, rows, :] = ((p - q) * scale).astype(o_ref.dtype)
        hbuf[rows, :] = p + q
    p, q = pq(slice(m, m + _DFT_PAD_ROWS), _DFT_PAD_ROWS)
    hbuf[m:m + _DFT_PAD_ROWS, :] = p + q
    hbuf[m + _DFT_PAD_ROWS:, :] = jnp.zeros((LANES - _DFT_PAD_ROWS, fw), F32)
    for c in range(nblk):
        win = hbuf[LANES * c:LANES * (c + 2), :].astype(BF16)
        blk_out = jnp.dot(perm, win, preferred_element_type=F32)
        o_ref[0, n - LANES * (c + 1):n - LANES * c, :] = (blk_out * scale).astype(o_ref.dtype)


def _seq_dft_tables(seq):
    m = seq // 2
    k = np.arange(m + _DFT_PAD_ROWS)[:, None]
    ang = 2.0 * np.pi * ((k * np.arange(m)[None, :]) % seq) / seq
    return jnp.asarray(np.cos(ang), BF16), jnp.asarray(np.sin(ang), BF16)


def _seqdft(ab):
    nb, seq, _ = ab.shape
    m = seq // 2
    tk = min(TR_DFT, m)
    assert m % LANES == 0 and m % tk == 0 and m % 64 == 0
    ct, st = _seq_dft_tables(seq)
    perm = np.zeros((LANES, 2 * LANES), np.float32)
    perm[np.arange(LANES), LANES - np.arange(LANES)] = 1.0
    scale = float(1.0 / np.sqrt(seq * FOURIER_GROUP))
    return pl.pallas_call(
        functools.partial(_seqdft_kernel, scale=scale, tk=tk),
        out_shape=jax.ShapeDtypeStruct((nb, seq, FOURIER_WIDTH), BF16),
        grid=(nb,),
        in_specs=[
            pl.BlockSpec((1, seq, 2 * FOURIER_WIDTH), lambda b: (b, 0, 0)),
            pl.BlockSpec(memory_space=pl.ANY),
            pl.BlockSpec(memory_space=pl.ANY),
            pl.BlockSpec(perm.shape, lambda b: (0, 0)),
        ],
        out_specs=pl.BlockSpec((1, seq, FOURIER_WIDTH), lambda b: (b, 0, 0)),
        scratch_shapes=[pltpu.VMEM(ct.shape, BF16), pltpu.VMEM(st.shape, BF16),
                        pltpu.VMEM((m, 2 * FOURIER_WIDTH), BF16),
                        pltpu.VMEM((m + LANES, FOURIER_WIDTH), F32),
                        pltpu.SemaphoreType.DMA((2,))],
        compiler_params=_cparams(("arbitrary",)),
        name="seqdft",
    )(ab, ct, st, jnp.asarray(perm, BF16))


def _mid_kernel(xp_ref, xs_ref, a_ref, f_ref, kv_ref, gao_ref, gfo_ref, wout_ref, gc_ref, wq_ref,
                wo_ref, gm_ref, wr_ref, br_ref, tri_ref,
                x2_ref, meta_ref, gate_ref, cnt_ref, slots_ref,
                cnt_sc, stage, posv, pos_sm, cntv, cnt_sm, zbuf, rsem, psem, zsem, *, n_prompt, cap, te):
    b = pl.program_id(0)
    i = pl.program_id(1)
    tm = a_ref.shape[1]
    step = b * pl.num_programs(1) + i
    nsteps = pl.num_programs(0) * pl.num_programs(1)
    tile_rows = tm * SUBLANES
    dump_base = N_EXPERTS * cap

    def wait_scatter(sem_idx):
        for _ in range(TOP_K):
            _row_copy(stage.at[0], slots_ref.at[pl.ds(0, tile_rows), :], rsem.at[sem_idx]).wait()

    @pl.when(step == 0)
    def _():
        cnt_sc[...] = jnp.zeros_like(cnt_sc)
        stage[2] = jnp.zeros((tile_rows, LANES), F32)

        def init(t, carry):
            for k in range(TOP_K):
                pos_sm[1, k, t] = dump_base + k * tm + t
            return carry

        lax.fori_loop(0, tm, init, 0)

    @pl.when(step >= 2)
    def _():
        wait_scatter(step % 2)

    prev_stage = (step + 2) % 3
    prev_pos = (step + 1) % 2
    for t in range(tm):
        for k in range(TOP_K):
            r0 = pl.multiple_of(pos_sm[prev_pos, k, t] * SUBLANES, SUBLANES)
            _row_copy(stage.at[prev_stage, pl.ds(t * SUBLANES, SUBLANES), :],
                      slots_ref.at[pl.ds(r0, SUBLANES), :], rsem.at[step % 2]).start(priority=k % 2)
    cur_stage = step % 3

    def token_chain(r0, nr):
        rows = slice(r0, r0 + nr)
        x = jnp.where(b < n_prompt, xp_ref[0, rows, :], xs_ref[0, rows, :])
        an = _rms(a_ref[0, rows, :].astype(F32), gao_ref[...]).astype(BF16)
        fn = _rms(f_ref[0, rows, :].astype(F32), gfo_ref[...]).astype(BF16)
        x1 = (x + jnp.dot(an, wout_ref[:ATTN_WIDTH, :], preferred_element_type=F32)
              + jnp.dot(fn, wout_ref[ATTN_WIDTH:, :], preferred_element_type=F32))

        h2 = _rms(x1, gc_ref[...]).astype(BF16)
        qc = (jnp.dot(h2, wq_ref[...], preferred_element_type=F32)
              * (LOG2E * CROSS_HEAD_DIM ** -0.5)).astype(BF16)
        heads = []
        for hd in range(N_CROSS_HEADS):
            cols = slice(hd * CROSS_HEAD_DIM, (hd + 1) * CROSS_HEAD_DIM)
            vcols = slice(CROSS_WIDTH + hd * CROSS_HEAD_DIM, CROSS_WIDTH + (hd + 1) * CROSS_HEAD_DIM)
            s = lax.dot_general(qc[:, cols], kv_ref[0, :, cols], (((1,), (1,)), ((), ())),
                                preferred_element_type=F32)
            e = jnp.exp2(s - jnp.max(s, axis=-1, keepdims=True))
            den = jnp.sum(e, axis=-1, keepdims=True)
            heads.append(jnp.dot(e.astype(BF16), kv_ref[0, :, vcols], preferred_element_type=F32) * (1.0 / den))
        oc = jnp.concatenate(heads, axis=1).astype(BF16)
        x2 = x1 + jnp.dot(oc, wo_ref[...], preferred_element_type=F32)
        x2_ref[0, rows, :] = x2

        h3 = _rms(x2, gm_ref[...])
        for sl in range(SUBLANES):
            stage[cur_stage, pl.ds(r0 * SUBLANES + sl, nr, stride=SUBLANES), :] = h3[:, sl * LANES:(sl + 1) * LANES]

        hi = h3.astype(BF16)
        lo = (h3 - hi.astype(F32)).astype(BF16)
        hw = jnp.dot(hi, wr_ref[...], preferred_element_type=F32)
        return (hw[:, :LANES] + hw[:, LANES:]
                + jnp.dot(lo, wr_ref[:, :LANES], preferred_element_type=F32) + br_ref[...])

    nr = tm // MID_CHAINS
    logits = jnp.concatenate([token_chain(c * nr, nr) for c in range(MID_CHAINS)], axis=0)
    lt = logits.T[:N_EXPERTS, :]
    eidx = lax.broadcasted_iota(I32, lt.shape, 0)
    vals, idxs, sels = [], [], []
    for _ in range(TOP_K):
        m = jnp.max(lt, axis=0, keepdims=True)
        idx = jnp.min(jnp.where(lt == m, eidx, N_EXPERTS), axis=0, keepdims=True)
        sel = eidx == idx
        lt = jnp.where(sel, -jnp.inf, lt)
        vals.append(m)
        idxs.append(idx)
        sels.append(sel)
    ex = [jnp.exp(vk - vals[0]) for vk in vals]
    inv = 1.0 / (ex[0] + ex[1] + ex[2] + ex[3])
    zrow = jnp.zeros_like(ex[0])
    gate_ref[...] = jnp.concatenate([e_ * inv for e_ in ex] + [zrow] * (SUBLANES - TOP_K), axis=0)

    onehot = jnp.zeros(lt.shape, F32)
    for sel in sels:
        onehot = onehot + sel.astype(F32)
    before = cnt_sc[:, 0:1] + jnp.dot(onehot.astype(BF16), tri_ref[...], preferred_element_type=F32)
    ranks = [jnp.sum(jnp.where(sel, before, 0.0), axis=0, keepdims=True).astype(I32) for sel in sels]
    meta_ref[...] = jnp.concatenate(idxs + ranks, axis=0)
    cnt_sc[...] = cnt_sc[...] + jnp.sum(onehot, axis=1, keepdims=True)
    cnt_ref[...] = cnt_sc[...]

    posv[...] = jnp.concatenate([ix * cap + rk for ix, rk in zip(idxs, ranks)], axis=0)
    to_smem = pltpu.make_async_copy(posv, pos_sm.at[step % 2], psem)
    to_smem.start()
    to_smem.wait()

    @pl.when(step == nsteps - 1)
    def _():
        @pl.when(step >= 1)
        def _():
            wait_scatter((step + 1) % 2)

        def issue(tb, carry):
            for u in range(ISSUE_UNROLL):
                t = tb * ISSUE_UNROLL + u
                for k in range(TOP_K):
                    r0 = pl.multiple_of(pos_sm[step % 2, k, t] * SUBLANES, SUBLANES)
                    _row_copy(stage.at[cur_stage, pl.ds(pl.multiple_of(t * SUBLANES, SUBLANES), SUBLANES), :],
                              slots_ref.at[pl.ds(r0, SUBLANES), :], rsem.at[(step + 1) % 2]).start(priority=k % 2)
            return carry

        lax.fori_loop(0, tm // ISSUE_UNROLL, issue, 0)
        wait_scatter(step % 2)
        wait_scatter((step + 1) % 2)

        cntv[...] = cnt_sc[...].astype(I32)
        counts_to_smem = pltpu.make_async_copy(cntv, cnt_sm, psem)
        counts_to_smem.start()
        counts_to_smem.wait()
        zbuf[...] = jnp.zeros_like(zbuf)
        zrows = zbuf.shape[0] // SUBLANES
        bits = [1 << j for j in range(zrows.bit_length() - 1, -1, -1)]
        assert sum(bits) == 2 * zrows - 1 and zrows * 2 == te

        def fill(start):
            for e in range(N_EXPERTS):
                cnt = cnt_sm[e, 0]
                pad = (te - cnt % te) % te
                row = e * cap + cnt
                for bit in bits:
                    @pl.when((pad & bit) != 0)
                    def _(row=row, bit=bit):
                        cp = _row_copy(zbuf.at[pl.ds(0, bit * SUBLANES), :],
                                       slots_ref.at[pl.ds(pl.multiple_of(row * SUBLANES, SUBLANES),
                                                          bit * SUBLANES), :], zsem)
                        cp.start() if start else cp.wait()
                    row = row + (pad & bit)

        fill(True)
        fill(False)


def _mid(x_p, x_s, a, f, kvc, g_attn_out, g_four_out, w_out, g_cross, w_q, w_o, g_moe, w_router, b_router,
         cap, te):
    n_prompt, seq, d = x_p.shape
    nb = a.shape[0]
    tm = min(TM_MID, seq)
    nt = seq // tm
    n_mem = kvc.shape[1]
    pb, sb = _two_group_maps(n_prompt)
    wr = jnp.zeros((d, LANES), F32).at[:, :N_EXPERTS].set(w_router)
    wr_hi = wr.astype(BF16)
    wr_lo = (wr - wr_hi.astype(F32)).astype(BF16)
    br = jnp.zeros((1, LANES), F32).at[0, :N_EXPERTS].set(b_router)
    tri = jnp.asarray(np.triu(np.ones((tm, tm), np.float32), 1), BF16)
    full = lambda arr: pl.BlockSpec(arr.shape, lambda b, i: (0,) * arr.ndim)
    row = lambda v: v.reshape(1, -1)
    args = [x_p, x_s, a, f, kvc, row(g_attn_out), row(g_four_out), w_out.astype(BF16), row(g_cross),
            w_q.astype(BF16), w_o.astype(BF16), row(g_moe), jnp.concatenate([wr_hi, wr_lo], axis=1), br, tri]
    in_specs = [
        pl.BlockSpec((1, tm, d), lambda b, i: (pb(b), i, 0)),
        pl.BlockSpec((1, tm, d), lambda b, i: (sb(b), i, 0)),
        pl.BlockSpec((1, tm, ATTN_WIDTH), lambda b, i: (b, i, 0)),
        pl.BlockSpec((1, tm, FOURIER_WIDTH), lambda b, i: (b, i, 0)),
        pl.BlockSpec((1, n_mem, 2 * CROSS_WIDTH), lambda b, i: (b, 0, 0)),
    ] + [full(v) for v in args[5:]]
    tok = nb * seq
    slot_rows = N_EXPERTS * cap + TOP_K * tm
    return pl.pallas_call(
        functools.partial(_mid_kernel, n_prompt=n_prompt, cap=cap, te=te),
        out_shape=(jax.ShapeDtypeStruct((nb, seq, d), F32),
                   jax.ShapeDtypeStruct((SUBLANES, tok), I32),
                   jax.ShapeDtypeStruct((SUBLANES, tok), F32),
                   jax.ShapeDtypeStruct((N_EXPERTS, LANES), F32),
                   jax.ShapeDtypeStruct((slot_rows * SUBLANES, LANES), F32)),
        grid=(nb, nt),
        in_specs=in_specs,
        out_specs=(pl.BlockSpec((1, tm, d), lambda b, i: (b, i, 0)),
                   pl.BlockSpec((SUBLANES, tm), lambda b, i: (0, b * nt + i)),
                   pl.BlockSpec((SUBLANES, tm), lambda b, i: (0, b * nt + i)),
                   pl.BlockSpec((N_EXPERTS, LANES), lambda b, i: (0, 0)),
                   pl.BlockSpec(memory_space=pl.ANY)),
        scratch_shapes=[pltpu.VMEM((N_EXPERTS, LANES), F32),
                        pltpu.VMEM((3, tm * SUBLANES, LANES), F32),
                        pltpu.VMEM((TOP_K, tm), I32), pltpu.SMEM((2, TOP_K, tm), I32),
                        pltpu.VMEM((N_EXPERTS, LANES), I32), pltpu.SMEM((N_EXPERTS, LANES), I32),
                        pltpu.VMEM((te // 2 * SUBLANES, LANES), F32),
                        pltpu.SemaphoreType.DMA((2,)), pltpu.SemaphoreType.DMA, pltpu.SemaphoreType.DMA],
        compiler_params=_cparams(("arbitrary", "arbitrary")),
        name="mid",
    )(*args)


def _expert_kernel(te_ref, ti_ref, to_ref, nu_ref, xs_ref, wgu_ref, bgu_ref, wd_ref, bd_ref, ys_ref, wgu_bf, wd_bf):
    i = pl.program_id(0)
    te = xs_ref.shape[0] // SUBLANES
    dff = wd_ref.shape[1]

    @pl.when((i == 0) | (te_ref[i] != te_ref[jnp.maximum(i - 1, 0)]))
    def _():
        wgu_bf[...] = wgu_ref[0].astype(BF16)
        wd_bf[...] = wd_ref[0].astype(BF16)

    @pl.when(i < nu_ref[0])
    def _():
        x = jnp.concatenate([xs_ref[pl.ds(sl, te, stride=SUBLANES), :] for sl in range(SUBLANES)],
                            axis=1).astype(BF16)
        gu = jnp.dot(x, wgu_bf[...], preferred_element_type=F32) + bgu_ref[0]
        gate = jnp.minimum(gu[:, :dff], SWIGLU_LIMIT)
        up = jnp.clip(gu[:, dff:], -SWIGLU_LIMIT, SWIGLU_LIMIT)
        act = (up + 1.0) * (gate * (1.0 / (1.0 + jnp.exp(-SWIGLU_ALPHA * gate))))
        y = jnp.dot(act.astype(BF16), wd_bf[...], preferred_element_type=F32) + bd_ref[0]
        for sl in range(SUBLANES):
            ys_ref[pl.ds(sl, te, stride=SUBLANES), :] = y[:, sl * LANES:(sl + 1) * LANES]


def _experts(xs, tile_expert, tile_in, tile_out, n_used, w_gate_up, b_gate_up, w_down, b_down, te):
    ne, d, two_dff = w_gate_up.shape
    dff = two_dff // 2
    ntile = tile_expert.shape[0]
    return pl.pallas_call(
        _expert_kernel,
        out_shape=jax.ShapeDtypeStruct((ntile * te * SUBLANES, LANES), F32),
        grid_spec=pltpu.PrefetchScalarGridSpec(
            num_scalar_prefetch=4,
            grid=(ntile,),
            in_specs=[
                pl.BlockSpec((te * SUBLANES, LANES), lambda i, te_r, ti_r, to_r, nu_r: (ti_r[i], 0)),
                pl.BlockSpec((1, d, two_dff), lambda i, te_r, ti_r, to_r, nu_r: (te_r[i], 0, 0)),
                pl.BlockSpec((1, 1, two_dff), lambda i, te_r, ti_r, to_r, nu_r: (te_r[i], 0, 0)),
                pl.BlockSpec((1, dff, d), lambda i, te_r, ti_r, to_r, nu_r: (te_r[i], 0, 0)),
                pl.BlockSpec((1, 1, d), lambda i, te_r, ti_r, to_r, nu_r: (te_r[i], 0, 0)),
            ],
            out_specs=pl.BlockSpec((te * SUBLANES, LANES), lambda i, te_r, ti_r, to_r, nu_r: (to_r[i], 0)),
            scratch_shapes=[pltpu.VMEM((d, two_dff), BF16), pltpu.VMEM((dff, d), BF16)],
        ),
        compiler_params=_cparams(("arbitrary",)),
        name="experts",
    )(tile_expert, tile_in, tile_out, n_used, xs, w_gate_up, b_gate_up.reshape(ne, 1, two_dff),
      w_down, b_down.reshape(ne, 1, d))


def _combine_kernel(pos_a, pos_b, pos_n, x2_ref, gate_ref, g_ref, ys_ref, o_ref, buf_a, buf_b, sem, *, tm):
    step = pl.program_id(0) * pl.num_programs(1) + pl.program_id(1)
    nsteps = pl.num_programs(0) * pl.num_programs(1)
    slot_rows = TOP_K * tm * SUBLANES

    def wait_rows(buf, s):
        _row_copy(ys_ref.at[pl.ds(0, slot_rows), :], buf, sem.at[s]).wait()

    @pl.when(step == 0)
    def _():
        def issue(tb, carry):
            for u in range(ISSUE_UNROLL):
                t = tb * ISSUE_UNROLL + u
                for k in range(TOP_K):
                    r0 = pl.multiple_of(pos_a[0, 0, k * tm + t] * SUBLANES, SUBLANES)
                    d0 = pl.multiple_of((k * tm + t) * SUBLANES, SUBLANES)
                    _row_copy(ys_ref.at[pl.ds(r0, SUBLANES), :], buf_a.at[pl.ds(d0, SUBLANES), :],
                              sem.at[0]).start(priority=k % 2)
            return carry

        lax.fori_loop(0, tm // ISSUE_UNROLL, issue, 0)

    def combine_tile(rows, buf, req_pos, req_buf, req_sem):
        x = x2_ref[0, rows, :]
        g = jnp.concatenate([gate_ref[:, rows], jnp.zeros((LANES - SUBLANES, tm), F32)], axis=0).T
        for k in range(TOP_K):
            for t in range(tm):
                r0 = pl.multiple_of(req_pos[0, 0, k * tm + t] * SUBLANES, SUBLANES)
                _row_copy(ys_ref.at[pl.ds(r0, SUBLANES), :],
                          req_buf.at[pl.ds((k * tm + t) * SUBLANES, SUBLANES), :],
                          sem.at[req_sem]).start(priority=t % 2)
            yk = jnp.concatenate(
                [buf[pl.ds(k * tm * SUBLANES + sl, tm, stride=SUBLANES), :] for sl in range(SUBLANES)], axis=1)
            x = x + g[:, k:k + 1] * yk
        o_ref[0, rows, :] = _rms(x, g_ref[...])

    wait_rows(buf_a, 0)
    combine_tile(slice(0, tm), buf_a, pos_b, buf_b, 1)
    wait_rows(buf_b, 1)
    combine_tile(slice(tm, 2 * tm), buf_b, pos_n, buf_a, 0)

    @pl.when(step == nsteps - 1)
    def _():
        wait_rows(buf_a, 0)


def _combine(x2, gates, pos3, ys, g_final, b0, nbatch):
    _, seq, d = x2.shape
    tm = pos3.shape[2] // TOP_K
    nt = seq // (2 * tm)
    assert seq % (2 * tm) == 0
    last = 2 * (b0 + nbatch) * nt - 2
    tile_a = lambda b, i: 2 * ((b0 + b) * nt + i)
    smem_tile = lambda fn: pl.BlockSpec((1, 1, TOP_K * tm), lambda b, i: (fn(b, i), 0, 0),
                                        memory_space=pltpu.SMEM)
    return pl.pallas_call(
        functools.partial(_combine_kernel, tm=tm),
        out_shape=jax.ShapeDtypeStruct((nbatch, seq, d), F32),
        grid_spec=pltpu.PrefetchScalarGridSpec(
            num_scalar_prefetch=0,
            grid=(nbatch, nt),
            in_specs=[
                smem_tile(tile_a),
                smem_tile(lambda b, i: tile_a(b, i) + 1),
                smem_tile(lambda b, i: jnp.minimum(tile_a(b, i) + 2, last)),
                pl.BlockSpec((1, 2 * tm, d), lambda b, i: (b0 + b, i, 0)),
                pl.BlockSpec((SUBLANES, 2 * tm), lambda b, i: (0, (b0 + b) * nt + i)),
                pl.BlockSpec((1, d), lambda b, i: (0, 0)),
                pl.BlockSpec(memory_space=pl.ANY),
            ],
            out_specs=pl.BlockSpec((1, 2 * tm, d), lambda b, i: (b, i, 0)),
            scratch_shapes=[pltpu.VMEM((TOP_K * tm * SUBLANES, LANES), F32),
                            pltpu.VMEM((TOP_K * tm * SUBLANES, LANES), F32),
                            pltpu.SemaphoreType.DMA((2,))],
        ),
        compiler_params=_cparams(("arbitrary", "arbitrary")),
        name="combine",
    )(pos3, pos3, pos3, x2, gates, g_final.reshape(1, d), ys)


def _routing_plan(meta, cnt, te, tm_rows, cap):
    tok = meta.shape[1]
    idx, rank = meta[:TOP_K], meta[TOP_K:]
    counts = cnt[:, 0].astype(I32)
    tiles = (counts + te - 1) // te
    tile_end = jnp.cumsum(tiles)
    tile_start = tile_end - tiles
    offsets = tile_start * te
    n_used = tile_end[-1]
    pos = rank + jnp.sum(jnp.where(idx[None] == jnp.arange(N_EXPERTS, dtype=I32)[:, None, None],
                                   offsets[:, None, None], 0), axis=0)
    ntile_max = (tok * TOP_K) // te + N_EXPERTS
    t_ids = jnp.minimum(jnp.arange(ntile_max, dtype=I32), n_used - 1)
    tile_expert = jnp.sum((t_ids[:, None] >= tile_end[None, :]).astype(I32), axis=1)
    first = jnp.sum(jnp.where(tile_expert[:, None] == jnp.arange(N_EXPERTS, dtype=I32)[None, :],
                              tile_start[None, :], 0), axis=1)
    tile_in = tile_expert * (cap // te) + (t_ids - first)
    ntok_tiles = tok // tm_rows
    pos3 = pos.reshape(TOP_K, ntok_tiles, tm_rows).transpose(1, 0, 2).reshape(ntok_tiles, 1, TOP_K * tm_rows)
    return pos3, tile_expert, tile_in, t_ids, n_used.reshape(1)


def kernel(x_prompt, x_sample, mem_prompt, mem_sample, g_mix, w_in, sinks, g_attn_out, g_four_out, w_out,
           g_cross, g_mem, w_q_cross, w_kv_cross, w_o_cross, g_moe, w_router, b_router, w_gate_up,
           b_gate_up, w_down, b_down, g_final):
    n_prompt, seq, _ = x_prompt.shape
    n_sample = x_sample.shape[0]
    l = 0
    kvc = _memkv(mem_prompt, mem_sample, g_mem[l], w_kv_cross[l])
    q, kv, ab = _inproj(x_prompt, x_sample, g_mix[l], w_in[l])
    a = _swa(q, kv, sinks[l])
    f = _seqdft(ab)
    tok = (n_prompt + n_sample) * seq
    te = TM_EXPERT
    cap = -(-tok // te) * te
    x2, meta, gates, cnt, xs = _mid(x_prompt, x_sample, a, f, kvc, g_attn_out[l], g_four_out[l], w_out[l],
                                    g_cross[l], w_q_cross[l], w_o_cross[l], g_moe[l], w_router[l], b_router[l],
                                    cap, te)
    tm_rows = min(TM_ROWS, seq // 2)
    pos3, tile_expert, tile_in, tile_out, n_used = _routing_plan(meta, cnt, te, tm_rows, cap)
    ys = _experts(xs, tile_expert, tile_in, tile_out, n_used, w_gate_up[l], b_gate_up[l], w_down[l], b_down[l], te)
    y_p = _combine(x2, gates, pos3, ys, g_final, 0, n_prompt)
    y_s = _combine(x2, gates, pos3, ys, g_final, n_prompt, n_sample)
    return (y_p, y_s)
```

```python
import functools

import numpy as np
import jax
import jax.numpy as jnp
from jax import lax
from jax.experimental import pallas as pl
from jax.experimental.pallas import tpu as pltpu

F32 = jnp.float32
BF16 = jnp.bfloat16
I32 = jnp.int32

HEAD_DIM = 64
N_Q_HEADS = 8
N_KV_HEADS = 2
ATTN_WIDTH = N_Q_HEADS * HEAD_DIM
KV_WIDTH = N_KV_HEADS * HEAD_DIM
FOURIER_WIDTH = 512
FOURIER_GROUP = 64
WINDOW = 128
ROPE_DIM = 16
ROPE_THETA = 500000.0
N_CROSS_HEADS = 4
CROSS_HEAD_DIM = 128
CROSS_WIDTH = N_CROSS_HEADS * CROSS_HEAD_DIM
N_EXPERTS = 32
TOP_K = 4
SWIGLU_LIMIT = 7.0
SWIGLU_ALPHA = 1.702
EPS = 1e-5
NEG_INF = -1e30
LOG2E = 1.4426950408889634

LANES = 128
SUBLANES = 8
VMEM_LIMIT_BYTES = 56 * 1024 * 1024

TS_INPROJ = 1024
TQ_ATTN = 1024
TR_DFT = 512
TM_MID = 512
TM_ROWS = 512
TM_EXPERT = 512
ISSUE_UNROLL = 4
MID_CHAINS = 1


def _cparams(sem):
    return pltpu.CompilerParams(dimension_semantics=sem, vmem_limit_bytes=VMEM_LIMIT_BYTES)


def _rms(x, g):
    return x * lax.rsqrt(jnp.mean(x * x, axis=-1, keepdims=True) + EPS) * g


def _row_copy(src_rows, dst_rows, sem):
    return pltpu.make_async_copy(src_rows, dst_rows, sem)


def _two_group_maps(n_prompt):
    def prompt_b(b):
        return jnp.minimum(b, n_prompt - 1)

    def sample_b(b):
        return jnp.maximum(b - n_prompt, 0)
    return prompt_b, sample_b


def _memkv_kernel(mp_ref, ms_ref, g_ref, w_ref, o_ref, *, n_prompt):
    b = pl.program_id(0)
    m = jnp.where(b < n_prompt, mp_ref[0], ms_ref[0])
    mn = _rms(m, g_ref[...]).astype(BF16)
    o_ref[0] = jnp.dot(mn, w_ref[...], preferred_element_type=F32).astype(BF16)


def _memkv(mem_p, mem_s, g_mem, w_kv):
    n_prompt, n_mem, d = mem_p.shape
    nb = n_prompt + mem_s.shape[0]
    pb, sb = _two_group_maps(n_prompt)
    return pl.pallas_call(
        functools.partial(_memkv_kernel, n_prompt=n_prompt),
        out_shape=jax.ShapeDtypeStruct((nb, n_mem, w_kv.shape[1]), BF16),
        grid=(nb,),
        in_specs=[
            pl.BlockSpec((1, n_mem, d), lambda b: (pb(b), 0, 0)),
            pl.BlockSpec((1, n_mem, d), lambda b: (sb(b), 0, 0)),
            pl.BlockSpec((1, d), lambda b: (0, 0)),
            pl.BlockSpec(w_kv.shape, lambda b: (0, 0)),
        ],
        out_specs=pl.BlockSpec((1, n_mem, w_kv.shape[1]), lambda b: (b, 0, 0)),
        compiler_params=_cparams(("arbitrary",)),
        name="memkv",
    )(mem_p, mem_s, g_mem.reshape(1, d), w_kv.astype(BF16))


_ROT_W = ATTN_WIDTH + 2 * KV_WIDTH


def _inproj_kernel(xp_ref, xs_ref, g_ref, w_ref, cos_ref, sin_ref, bd_ref,
                   q_ref, kv_ref, ab_ref, *, n_prompt):
    b = pl.program_id(1)
    x = jnp.where(b < n_prompt, xp_ref[0], xs_ref[0])
    h = _rms(x, g_ref[...]).astype(BF16)
    z = jnp.dot(h, w_ref[...], preferred_element_type=F32)
    cos = cos_ref[...]
    sin = sin_ref[...]
    lane = lax.broadcasted_iota(I32, cos.shape, 1) & (HEAD_DIM - 1)
    first_half = lane < ROPE_DIM // 2
    rot = []
    for c in range(_ROT_W // LANES):
        zc = z[:, c * LANES:(c + 1) * LANES]
        partner = jnp.where(first_half,
                            pltpu.roll(zc, LANES - ROPE_DIM // 2, axis=1),
                            pltpu.roll(zc, ROPE_DIM // 2, axis=1))
        rot.append(zc * cos + partner * sin)
    nq = ATTN_WIDTH // LANES
    q_ref[0] = (jnp.concatenate(rot[:nq], axis=1) * (LOG2E * HEAD_DIM ** -0.5)).astype(BF16)
    v_off = _ROT_W
    u_off = _ROT_W + 2 * KV_WIDTH
    kv_ref[0] = jnp.concatenate(rot[nq:] + [z[:, v_off:u_off]], axis=1).astype(BF16)
    half = FOURIER_WIDTH // 2
    u = z[:, u_off:].astype(BF16)
    r0 = jnp.dot(u[:, :half], bd_ref[...], preferred_element_type=F32)
    r1 = jnp.dot(u[:, half:], bd_ref[...], preferred_element_type=F32)
    ab_ref[0] = jnp.concatenate([r0[:, :half], r1[:, :half], r0[:, half:], r1[:, half:]],
                                axis=1).astype(BF16)


def _rope_tables(seq):
    half = ROPE_DIM // 2
    inv_freq = ROPE_THETA ** (-(jnp.arange(half, dtype=F32) * 2.0) / ROPE_DIM)
    ang = jnp.arange(seq).astype(F32)[:, None] * inv_freq[None, :]
    cos, sin = jnp.cos(ang), jnp.sin(ang)
    rest = HEAD_DIM - ROPE_DIM
    cos_h = jnp.concatenate([cos, cos, jnp.ones((seq, rest), F32)], axis=1)
    sin_h = jnp.concatenate([-sin, sin, jnp.zeros((seq, rest), F32)], axis=1)
    reps = LANES // HEAD_DIM
    return jnp.tile(cos_h, (1, reps)), jnp.tile(sin_h, (1, reps))


def _channel_dft_tables():
    n = FOURIER_GROUP
    idx = np.arange(n)
    ang = 2.0 * np.pi * ((idx[:, None] * idx[None, :]) % n) / n
    eye = np.eye(FOURIER_WIDTH // (2 * n))
    return jnp.asarray(np.concatenate([np.kron(eye, np.cos(ang)), np.kron(eye, np.sin(ang))], axis=1), BF16)


def _inproj(x_p, x_s, g_mix, w_in):
    n_prompt, seq, d = x_p.shape
    nb = n_prompt + x_s.shape[0]
    ts = min(TS_INPROJ, seq)
    o1, o2, o3 = ATTN_WIDTH, ATTN_WIDTH + KV_WIDTH, ATTN_WIDTH + 2 * KV_WIDTH
    swap = lambda w: jnp.concatenate([w[:, HEAD_DIM:], w[:, :HEAD_DIM]], axis=1)
    wk, wv = w_in[:, o1:o2], w_in[:, o2:o3]
    w_ext = jnp.concatenate([w_in[:, :o1], wk, swap(wk), wv, swap(wv), w_in[:, o3:]], axis=1).astype(BF16)
    cos_t, sin_t = _rope_tables(seq)
    bd = _channel_dft_tables()
    pb, sb = _two_group_maps(n_prompt)
    kvw = 4 * KV_WIDTH
    return pl.pallas_call(
        functools.partial(_inproj_kernel, n_prompt=n_prompt),
        out_shape=(jax.ShapeDtypeStruct((nb, seq, ATTN_WIDTH), BF16),
                   jax.ShapeDtypeStruct((nb, seq, kvw), BF16),
                   jax.ShapeDtypeStruct((nb, seq, 2 * FOURIER_WIDTH), BF16)),
        grid=(seq // ts, nb),
        in_specs=[
            pl.BlockSpec((1, ts, d), lambda i, b: (pb(b), i, 0)),
            pl.BlockSpec((1, ts, d), lambda i, b: (sb(b), i, 0)),
            pl.BlockSpec((1, d), lambda i, b: (0, 0)),
            pl.BlockSpec(w_ext.shape, lambda i, b: (0, 0)),
            pl.BlockSpec((ts, LANES), lambda i, b: (i, 0)),
            pl.BlockSpec((ts, LANES), lambda i, b: (i, 0)),
            pl.BlockSpec(bd.shape, lambda i, b: (0, 0)),
        ],
        out_specs=(pl.BlockSpec((1, ts, ATTN_WIDTH), lambda i, b: (b, i, 0)),
                   pl.BlockSpec((1, ts, kvw), lambda i, b: (b, i, 0)),
                   pl.BlockSpec((1, ts, 2 * FOURIER_WIDTH), lambda i, b: (b, i, 0))),
        compiler_params=_cparams(("arbitrary", "arbitrary")),
        name="inproj",
    )(x_p, x_s, g_mix.reshape(1, d), w_ext, cos_t, sin_t, bd)


def _swa_kernel(sink_ref, q_ref, kvm_ref, kvp_ref, kvn_ref, bias_ref, o_ref, *, nblk):
    i = pl.program_id(1)
    last = pl.num_programs(1) - 1
    kvw = jnp.concatenate([kvp_ref[0], kvm_ref[0], kvn_ref[0]], axis=0)
    k, ksw, v, vsw = [kvw[:, c * LANES:(c + 1) * LANES] for c in range(4)]
    lo_kv = lax.broadcasted_iota(I32, k.shape, 1) < HEAD_DIM
    k_dup = (jnp.where(lo_kv, k, ksw), jnp.where(lo_kv, ksw, k))
    v_dup = (jnp.where(lo_kv, v, vsw), jnp.where(lo_kv, vsw, v))
    lo_q = lax.broadcasted_iota(I32, (WINDOW, LANES), 1) < HEAD_DIM
    row = lax.broadcasted_iota(I32, (4 * WINDOW, 1), 0)
    for j in range(nblk):
        bias = bias_ref[1]
        if j == 0:
            bias = jnp.where(i == 0, bias_ref[0], bias)
        if j == nblk - 1:
            bias = jnp.where(i == last, bias_ref[2], bias)
        bias4 = jnp.concatenate([bias] * 4, axis=0)
        rows = slice(j * WINDOW, (j + 1) * WINDOW)
        win = slice(j * WINDOW, (j + 3) * WINDOW)
        for kvh in range(N_KV_HEADS):
            c0 = 2 * kvh
            qa = q_ref[0, rows, c0 * LANES:(c0 + 1) * LANES]
            qb = q_ref[0, rows, (c0 + 1) * LANES:(c0 + 2) * LANES]
            zero = jnp.zeros_like(qa)
            q4 = jnp.concatenate([jnp.where(lo_q, qa, zero), jnp.where(lo_q, zero, qa),
                                  jnp.where(lo_q, qb, zero), jnp.where(lo_q, zero, qb)], axis=0)
            s = lax.dot_general(q4, k_dup[kvh][win], (((1,), (1,)), ((), ())),
                                preferred_element_type=F32) + bias4
            h0 = 4 * kvh
            sink = jnp.where(row < WINDOW, sink_ref[h0],
                             jnp.where(row < 2 * WINDOW, sink_ref[h0 + 1],
                                       jnp.where(row < 3 * WINDOW, sink_ref[h0 + 2], sink_ref[h0 + 3])))
            m = jnp.maximum(jnp.max(s, axis=-1, keepdims=True), sink)
            e = jnp.exp2(s - m)
            den = jnp.sum(e, axis=-1, keepdims=True) + jnp.exp2(sink - m)
            o = jnp.dot(e.astype(BF16), v_dup[kvh][win], preferred_element_type=F32) * (1.0 / den)
            o_ref[0, rows, c0 * LANES:(c0 + 1) * LANES] = jnp.where(
                lo_q, o[:WINDOW], o[WINDOW:2 * WINDOW]).astype(o_ref.dtype)
            o_ref[0, rows, (c0 + 1) * LANES:(c0 + 2) * LANES] = jnp.where(
                lo_q, o[2 * WINDOW:3 * WINDOW], o[3 * WINDOW:]).astype(o_ref.dtype)


def _swa_bias():
    qi = np.arange(WINDOW)[:, None]
    c = np.arange(3 * WINDOW)[None, :]
    band = np.abs(c - WINDOW - qi) <= WINDOW
    first = band & (c >= WINDOW)
    lastb = band & (c < 2 * WINDOW)
    tab = np.stack([first, band, lastb]).astype(np.float32)
    return jnp.asarray((1.0 - tab) * NEG_INF, F32)


def _swa(q, kv, sinks):
    nb, seq, _ = q.shape
    tq = min(TQ_ATTN, seq)
    nblk = tq // WINDOW
    nseq_blk = seq // WINDOW
    kvw = kv.shape[-1]
    assert nblk >= 2 and seq % tq == 0, "first/last window masks are applied to distinct query blocks"
    bias = _swa_bias()
    return pl.pallas_call(
        functools.partial(_swa_kernel, nblk=nblk),
        out_shape=jax.ShapeDtypeStruct((nb, seq, ATTN_WIDTH), BF16),
        grid=(nb, seq // tq),
        in_specs=[
            pl.BlockSpec(memory_space=pltpu.SMEM),
            pl.BlockSpec((1, tq, ATTN_WIDTH), lambda b, i: (b, i, 0)),
            pl.BlockSpec((1, tq, kvw), lambda b, i: (b, i, 0)),
            pl.BlockSpec((1, WINDOW, kvw), lambda b, i: (b, jnp.maximum(i * nblk - 1, 0), 0)),
            pl.BlockSpec((1, WINDOW, kvw), lambda b, i: (b, jnp.minimum((i + 1) * nblk, nseq_blk - 1), 0)),
            pl.BlockSpec(bias.shape, lambda b, i: (0, 0, 0)),
        ],
        out_specs=pl.BlockSpec((1, tq, ATTN_WIDTH), lambda b, i: (b, i, 0)),
        compiler_params=_cparams(("arbitrary", "arbitrary")),
        name="swa",
    )(sinks.astype(F32) * LOG2E, q, kv, kv, kv, bias)


_DFT_PAD_ROWS = 16


def _seqdft_kernel(ab_ref, ct_hbm, st_hbm, perm_ref, o_ref, ct, st, aebo, hbuf, sem, *, scale, tk):
    n = ab_ref.shape[1]
    m = n // 2
    fw = FOURIER_WIDTH
    nblk = m // LANES

    @pl.when(pl.program_id(0) == 0)
    def _():
        copies = [pltpu.make_async_copy(ct_hbm, ct, sem.at[0]), pltpu.make_async_copy(st_hbm, st, sem.at[1])]
        for c in copies:
            c.start()
        for c in copies:
            c.wait()

    perm = perm_ref[...]
    for blk in range(nblk):
        mirrored = [ab_ref[0, n - LANES * (blk + 1):n - LANES * blk, :]]
        if blk > 0:
            mirrored.append(ab_ref[0, n - LANES * blk:n - LANES * (blk - 1), :])
        else:
            mirrored.append(jnp.zeros((LANES, 2 * fw), BF16))
        r = jnp.dot(perm, jnp.concatenate(mirrored, axis=0), preferred_element_type=F32)
        cur = ab_ref[0, LANES * blk:LANES * (blk + 1), :].astype(F32)
        aebo[LANES * blk:LANES * (blk + 1), :] = jnp.concatenate(
            [cur[:, :fw] + r[:, :fw], cur[:, fw:] - r[:, fw:]], axis=1).astype(BF16)

    a_mid = ab_ref[0, m:m + _DFT_PAD_ROWS, :fw][0:1, :].astype(F32)

    def pq(rows, nrows):
        p = jnp.dot(ct[rows, :], aebo[:, :fw], preferred_element_type=F32)
        q = jnp.dot(st[rows, :], aebo[:, fw:], preferred_element_type=F32)
        odd = (lax.broadcasted_iota(I32, (nrows, fw), 0) & 1) == 1
        return p + jnp.where(odd, -a_mid, a_mid), q

    for kt in range(m // tk):
        rows = slice(kt * tk, (kt + 1) * tk)
        p, q = pq(rows, tk)
        o_ref[0, rows, :] = ((p - q) * scale).astype(o_ref.dtype)
        hbuf[rows, :] = p + q
    p, q = pq(slice(m, m + _DFT_PAD_ROWS), _DFT_PAD_ROWS)
    hbuf[m:m + _DFT_PAD_ROWS, :] = p + q
    hbuf[m + _DFT_PAD_ROWS:, :] = jnp.zeros((LANES - _DFT_PAD_ROWS, fw), F32)
    for c in range(nblk):
        win = hbuf[LANES * c:LANES * (c + 2), :].astype(BF16)
        blk_out = jnp.dot(perm, win, preferred_element_type=F32)
        o_ref[0, n - LANES * (c + 1):n - LANES * c, :] = (blk_out * scale).astype(o_ref.dtype)


def _seq_dft_tables(seq):
    m = seq // 2
    k = np.arange(m + _DFT_PAD_ROWS)[:, None]
    ang = 2.0 * np.pi * ((k * np.arange(m)[None, :]) % seq) / seq
    return jnp.asarray(np.cos(ang), BF16), jnp.asarray(np.sin(ang), BF16)


def _seqdft(ab):
    nb, seq, _ = ab.shape
    m = seq // 2
    tk = min(TR_DFT, m)
    assert m % LANES == 0 and m % tk == 0 and m % 64 == 0
    ct, st = _seq_dft_tables(seq)
    perm = np.zeros((LANES, 2 * LANES), np.float32)
    perm[np.arange(LANES), LANES - np.arange(LANES)] = 1.0
    scale = float(1.0 / np.sqrt(seq * FOURIER_GROUP))
    return pl.pallas_call(
        functools.partial(_seqdft_kernel, scale=scale, tk=tk),
        out_shape=jax.ShapeDtypeStruct((nb, seq, FOURIER_WIDTH), BF16),
        grid=(nb,),
        in_specs=[
            pl.BlockSpec((1, seq, 2 * FOURIER_WIDTH), lambda b: (b, 0, 0)),
            pl.BlockSpec(memory_space=pl.ANY),
            pl.BlockSpec(memory_space=pl.ANY),
            pl.BlockSpec(perm.shape, lambda b: (0, 0)),
        ],
        out_specs=pl.BlockSpec((1, seq, FOURIER_WIDTH), lambda b: (b, 0, 0)),
        scratch_shapes=[pltpu.VMEM(ct.shape, BF16), pltpu.VMEM(st.shape, BF16),
                        pltpu.VMEM((m, 2 * FOURIER_WIDTH), BF16),
                        pltpu.VMEM((m + LANES, FOURIER_WIDTH), F32),
                        pltpu.SemaphoreType.DMA((2,))],
        compiler_params=_cparams(("arbitrary",)),
        name="seqdft",
    )(ab, ct, st, jnp.asarray(perm, BF16))


def _mid_kernel(xp_ref, xs_ref, a_ref, f_ref, kv_ref, gao_ref, gfo_ref, wout_ref, gc_ref, wq_ref,
                wo_ref, gm_ref, wr_ref, br_ref, tri_ref,
                x2_ref, meta_ref, gate_ref, cnt_ref, slots_ref,
                cnt_sc, stage, posv, pos_sm, cntv, cnt_sm, zbuf, rsem, psem, zsem, *, n_prompt, cap, te):
    b = pl.program_id(0)
    i = pl.program_id(1)
    tm = a_ref.shape[1]
    step = b * pl.num_programs(1) + i
    nsteps = pl.num_programs(0) * pl.num_programs(1)
    tile_rows = tm * SUBLANES
    dump_base = N_EXPERTS * cap

    def wait_scatter(sem_idx):
        for _ in range(TOP_K):
            _row_copy(stage.at[0], slots_ref.at[pl.ds(0, tile_rows), :], rsem.at[sem_idx]).wait()

    @pl.when(step == 0)
    def _():
        cnt_sc[...] = jnp.zeros_like(cnt_sc)
        stage[2] = jnp.zeros((tile_rows, LANES), F32)

        def init(t, carry):
            for k in range(TOP_K):
                pos_sm[1, k, t] = dump_base + k * tm + t
            return carry

        lax.fori_loop(0, tm, init, 0)

    @pl.when(step >= 2)
    def _():
        wait_scatter(step % 2)

    prev_stage = (step + 2) % 3
    prev_pos = (step + 1) % 2
    for t in range(tm):
        for k in range(TOP_K):
            r0 = pl.multiple_of(pos_sm[prev_pos, k, t] * SUBLANES, SUBLANES)
            _row_copy(stage.at[prev_stage, pl.ds(t * SUBLANES, SUBLANES), :],
                      slots_ref.at[pl.ds(r0, SUBLANES), :], rsem.at[step % 2]).start(priority=k % 2)
    cur_stage = step % 3

    def token_chain(r0, nr):
        rows = slice(r0, r0 + nr)
        x = jnp.where(b < n_prompt, xp_ref[0, rows, :], xs_ref[0, rows, :])
        an = _rms(a_ref[0, rows, :].astype(F32), gao_ref[...]).astype(BF16)
        fn = _rms(f_ref[0, rows, :].astype(F32), gfo_ref[...]).astype(BF16)
        x1 = (x + jnp.dot(an, wout_ref[:ATTN_WIDTH, :], preferred_element_type=F32)
              + jnp.dot(fn, wout_ref[ATTN_WIDTH:, :], preferred_element_type=F32))

        h2 = _rms(x1, gc_ref[...]).astype(BF16)
        qc = (jnp.dot(h2, wq_ref[...], preferred_element_type=F32)
              * (LOG2E * CROSS_HEAD_DIM ** -0.5)).astype(BF16)
        heads = []
        for hd in range(N_CROSS_HEADS):
            cols = slice(hd * CROSS_HEAD_DIM, (hd + 1) * CROSS_HEAD_DIM)
            vcols = slice(CROSS_WIDTH + hd * CROSS_HEAD_DIM, CROSS_WIDTH + (hd + 1) * CROSS_HEAD_DIM)
            s = lax.dot_general(qc[:, cols], kv_ref[0, :, cols], (((1,), (1,)), ((), ())),
                                preferred_element_type=F32)
            e = jnp.exp2(s - jnp.max(s, axis=-1, keepdims=True))
            den = jnp.sum(e, axis=-1, keepdims=True)
            heads.append(jnp.dot(e.astype(BF16), kv_ref[0, :, vcols], preferred_element_type=F32) * (1.0 / den))
        oc = jnp.concatenate(heads, axis=1).astype(BF16)
        x2 = x1 + jnp.dot(oc, wo_ref[...], preferred_element_type=F32)
        x2_ref[0, rows, :] = x2

        h3 = _rms(x2, gm_ref[...])
        for sl in range(SUBLANES):
            stage[cur_stage, pl.ds(r0 * SUBLANES + sl, nr, stride=SUBLANES), :] = h3[:, sl * LANES:(sl + 1) * LANES]

        hi = h3.astype(BF16)
        lo = (h3 - hi.astype(F32)).astype(BF16)
        hw = jnp.dot(hi, wr_ref[...], preferred_element_type=F32)
        return (hw[:, :LANES] + hw[:, LANES:]
                + jnp.dot(lo, wr_ref[:, :LANES], preferred_element_type=F32) + br_ref[...])

    nr = tm // MID_CHAINS
    logits = jnp.concatenate([token_chain(c * nr, nr) for c in range(MID_CHAINS)], axis=0)
    lt = logits.T[:N_EXPERTS, :]
    eidx = lax.broadcasted_iota(I32, lt.shape, 0)
    vals, idxs, sels = [], [], []
    for _ in range(TOP_K):
        m = jnp.max(lt, axis=0, keepdims=True)
        idx = jnp.min(jnp.where(lt == m, eidx, N_EXPERTS), axis=0, keepdims=True)
        sel = eidx == idx
        lt = jnp.where(sel, -jnp.inf, lt)
        vals.append(m)
        idxs.append(idx)
        sels.append(sel)
    ex = [jnp.exp(vk - vals[0]) for vk in vals]
    inv = 1.0 / (ex[0] + ex[1] + ex[2] + ex[3])
    zrow = jnp.zeros_like(ex[0])
    gate_ref[...] = jnp.concatenate([e_ * inv for e_ in ex] + [zrow] * (SUBLANES - TOP_K), axis=0)

    onehot = jnp.zeros(lt.shape, F32)
    for sel in sels:
        onehot = onehot + sel.astype(F32)
    before = cnt_sc[:, 0:1] + jnp.dot(onehot.astype(BF16), tri_ref[...], preferred_element_type=F32)
    ranks = [jnp.sum(jnp.where(sel, before, 0.0), axis=0, keepdims=True).astype(I32) for sel in sels]
    meta_ref[...] = jnp.concatenate(idxs + ranks, axis=0)
    cnt_sc[...] = cnt_sc[...] + jnp.sum(onehot, axis=1, keepdims=True)
    cnt_ref[...] = cnt_sc[...]

    posv[...] = jnp.concatenate([ix * cap + rk for ix, rk in zip(idxs, ranks)], axis=0)
    to_smem = pltpu.make_async_copy(posv, pos_sm.at[step % 2], psem)
    to_smem.start()
    to_smem.wait()

    @pl.when(step == nsteps - 1)
    def _():
        @pl.when(step >= 1)
        def _():
            wait_scatter((step + 1) % 2)

        def issue(tb, carry):
            for u in range(ISSUE_UNROLL):
                t = tb * ISSUE_UNROLL + u
                for k in range(TOP_K):
                    r0 = pl.multiple_of(pos_sm[step % 2, k, t] * SUBLANES, SUBLANES)
                    _row_copy(stage.at[cur_stage, pl.ds(pl.multiple_of(t * SUBLANES, SUBLANES), SUBLANES), :],
                              slots_ref.at[pl.ds(r0, SUBLANES), :], rsem.at[(step + 1) % 2]).start(priority=k % 2)
            return carry

        lax.fori_loop(0, tm // ISSUE_UNROLL, issue, 0)
        wait_scatter(step % 2)
        wait_scatter((step + 1) % 2)

        cntv[...] = cnt_sc[...].astype(I32)
        counts_to_smem = pltpu.make_async_copy(cntv, cnt_sm, psem)
        counts_to_smem.start()
        counts_to_smem.wait()
        zbuf[...] = jnp.zeros_like(zbuf)
        zrows = zbuf.shape[0] // SUBLANES
        bits = [1 << j for j in range(zrows.bit_length() - 1, -1, -1)]
        assert sum(bits) == 2 * zrows - 1 and zrows * 2 == te

        def fill(start):
            for e in range(N_EXPERTS):
                cnt = cnt_sm[e, 0]
                pad = (te - cnt % te) % te
                row = e * cap + cnt
                for bit in bits:
                    @pl.when((pad & bit) != 0)
                    def _(row=row, bit=bit):
                        cp = _row_copy(zbuf.at[pl.ds(0, bit * SUBLANES), :],
                                       slots_ref.at[pl.ds(pl.multiple_of(row * SUBLANES, SUBLANES),
                                                          bit * SUBLANES), :], zsem)
                        cp.start() if start else cp.wait()
                    row = row + (pad & bit)

        fill(True)
        fill(False)


def _mid(x_p, x_s, a, f, kvc, g_attn_out, g_four_out, w_out, g_cross, w_q, w_o, g_moe, w_router, b_router,
         cap, te):
    n_prompt, seq, d = x_p.shape
    nb = a.shape[0]
    tm = min(TM_MID, seq)
    nt = seq // tm
    n_mem = kvc.shape[1]
    pb, sb = _two_group_maps(n_prompt)
    wr = jnp.zeros((d, LANES), F32).at[:, :N_EXPERTS].set(w_router)
    wr_hi = wr.astype(BF16)
    wr_lo = (wr - wr_hi.astype(F32)).astype(BF16)
    br = jnp.zeros((1, LANES), F32).at[0, :N_EXPERTS].set(b_router)
    tri = jnp.asarray(np.triu(np.ones((tm, tm), np.float32), 1), BF16)
    full = lambda arr: pl.BlockSpec(arr.shape, lambda b, i: (0,) * arr.ndim)
    row = lambda v: v.reshape(1, -1)
    args = [x_p, x_s, a, f, kvc, row(g_attn_out), row(g_four_out), w_out.astype(BF16), row(g_cross),
            w_q.astype(BF16), w_o.astype(BF16), row(g_moe), jnp.concatenate([wr_hi, wr_lo], axis=1), br, tri]
    in_specs = [
        pl.BlockSpec((1, tm, d), lambda b, i: (pb(b), i, 0)),
        pl.BlockSpec((1, tm, d), lambda b, i: (sb(b), i, 0)),
        pl.BlockSpec((1, tm, ATTN_WIDTH), lambda b, i: (b, i, 0)),
        pl.BlockSpec((1, tm, FOURIER_WIDTH), lambda b, i: (b, i, 0)),
        pl.BlockSpec((1, n_mem, 2 * CROSS_WIDTH), lambda b, i: (b, 0, 0)),
    ] + [full(v) for v in args[5:]]
    tok = nb * seq
    slot_rows = N_EXPERTS * cap + TOP_K * tm
    return pl.pallas_call(
        functools.partial(_mid_kernel, n_prompt=n_prompt, cap=cap, te=te),
        out_shape=(jax.ShapeDtypeStruct((nb, seq, d), F32),
                   jax.ShapeDtypeStruct((SUBLANES, tok), I32),
                   jax.ShapeDtypeStruct((SUBLANES, tok), F32),
                   jax.ShapeDtypeStruct((N_EXPERTS, LANES), F32),
                   jax.ShapeDtypeStruct((slot_rows * SUBLANES, LANES), F32)),
        grid=(nb, nt),
        in_specs=in_specs,
        out_specs=(pl.BlockSpec((1, tm, d), lambda b, i: (b, i, 0)),
                   pl.BlockSpec((SUBLANES, tm), lambda b, i: (0, b * nt + i)),
                   pl.BlockSpec((SUBLANES, tm), lambda b, i: (0, b * nt + i)),
                   pl.BlockSpec((N_EXPERTS, LANES), lambda b, i: (0, 0)),
                   pl.BlockSpec(memory_space=pl.ANY)),
        scratch_shapes=[pltpu.VMEM((N_EXPERTS, LANES), F32),
                        pltpu.VMEM((3, tm * SUBLANES, LANES), F32),
                        pltpu.VMEM((TOP_K, tm), I32), pltpu.SMEM((2, TOP_K, tm), I32),
                        pltpu.VMEM((N_EXPERTS, LANES), I32), pltpu.SMEM((N_EXPERTS, LANES), I32),
                        pltpu.VMEM((te // 2 * SUBLANES, LANES), F32),
                        pltpu.SemaphoreType.DMA((2,)), pltpu.SemaphoreType.DMA, pltpu.SemaphoreType.DMA],
        compiler_params=_cparams(("arbitrary", "arbitrary")),
        name="mid",
    )(*args)


def _expert_kernel(te_ref, ti_ref, to_ref, nu_ref, xs_ref, wgu_ref, bgu_ref, wd_ref, bd_ref, ys_ref, wgu_bf, wd_bf):
    i = pl.program_id(0)
    te = xs_ref.shape[0] // SUBLANES
    dff = wd_ref.shape[1]

    @pl.when((i == 0) | (te_ref[i] != te_ref[jnp.maximum(i - 1, 0)]))
    def _():
        wgu_bf[...] = wgu_ref[0].astype(BF16)
        wd_bf[...] = wd_ref[0].astype(BF16)

    @pl.when(i < nu_ref[0])
    def _():
        x = jnp.concatenate([xs_ref[pl.ds(sl, te, stride=SUBLANES), :] for sl in range(SUBLANES)],
                            axis=1).astype(BF16)
        gu = jnp.dot(x, wgu_bf[...], preferred_element_type=F32) + bgu_ref[0]
        gate = jnp.minimum(gu[:, :dff], SWIGLU_LIMIT)
        up = jnp.clip(gu[:, dff:], -SWIGLU_LIMIT, SWIGLU_LIMIT)
        act = (up + 1.0) * (gate * (1.0 / (1.0 + jnp.exp(-SWIGLU_ALPHA * gate))))
        y = jnp.dot(act.astype(BF16), wd_bf[...], preferred_element_type=F32) + bd_ref[0]
        for sl in range(SUBLANES):
            ys_ref[pl.ds(sl, te, stride=SUBLANES), :] = y[:, sl * LANES:(sl + 1) * LANES]


def _experts(xs, tile_expert, tile_in, tile_out, n_used, w_gate_up, b_gate_up, w_down, b_down, te):
    ne, d, two_dff = w_gate_up.shape
    dff = two_dff // 2
    ntile = tile_expert.shape[0]
    return pl.pallas_call(
        _expert_kernel,
        out_shape=jax.ShapeDtypeStruct(((ntile * te + SEG_ROWS) * SUBLANES, LANES), F32),
        grid_spec=pltpu.PrefetchScalarGridSpec(
            num_scalar_prefetch=4,
            grid=(ntile,),
            in_specs=[
                pl.BlockSpec((te * SUBLANES, LANES), lambda i, te_r, ti_r, to_r, nu_r: (ti_r[i], 0)),
                pl.BlockSpec((1, d, two_dff), lambda i, te_r, ti_r, to_r, nu_r: (te_r[i], 0, 0)),
                pl.BlockSpec((1, 1, two_dff), lambda i, te_r, ti_r, to_r, nu_r: (te_r[i], 0, 0)),
                pl.BlockSpec((1, dff, d), lambda i, te_r, ti_r, to_r, nu_r: (te_r[i], 0, 0)),
                pl.BlockSpec((1, 1, d), lambda i, te_r, ti_r, to_r, nu_r: (te_r[i], 0, 0)),
            ],
            out_specs=pl.BlockSpec((te * SUBLANES, LANES), lambda i, te_r, ti_r, to_r, nu_r: (to_r[i], 0)),
            scratch_shapes=[pltpu.VMEM((d, two_dff), BF16), pltpu.VMEM((dff, d), BF16)],
        ),
        compiler_params=_cparams(("arbitrary",)),
        name="experts",
    )(tile_expert, tile_in, tile_out, n_used, xs, w_gate_up, b_gate_up.reshape(ne, 1, two_dff),
      w_down, b_down.reshape(ne, 1, d))


SEG_ROWS = 32


def _combine_kernel(seg_ref, segn_ref, lidx_ref, gate_ref, x2_ref, g_ref, ys_ref, o_ref, buf, obuf, sem, *, tm):
    step = pl.program_id(0) * pl.num_programs(1) + pl.program_id(1)
    nsteps = pl.num_programs(0) * pl.num_programs(1)
    slot = step % 2
    seg_tile = SEG_ROWS * SUBLANES

    def segments(seg, to_slot, start):
        def per_segment(j, carry):
            src = pl.multiple_of(seg[0, 0, 1 + j] * SUBLANES, SUBLANES)
            dst = pl.multiple_of(j * seg_tile, seg_tile)
            cp = _row_copy(ys_ref.at[pl.ds(src, seg_tile), :], buf.at[to_slot, pl.ds(dst, seg_tile), :],
                           sem.at[to_slot])
            if start:
                cp.start()
            else:
                cp.wait()
            return carry

        lax.fori_loop(0, seg[0, 0, 0], per_segment, 0)

    @pl.when(step == 0)
    def _():
        segments(seg_ref, 0, True)

    @pl.when(step + 1 < nsteps)
    def _():
        segments(segn_ref, 1 - slot, True)

    segments(seg_ref, slot, False)

    for t in range(tm):
        acc = None
        for k in range(TOP_K):
            p = pl.multiple_of(lidx_ref[0, 0, k * tm + t] * SUBLANES, SUBLANES)
            term = gate_ref[0, 0, k * tm + t] * buf[slot, pl.ds(p, SUBLANES), :]
            acc = term if acc is None else acc + term
        obuf[pl.ds(t * SUBLANES, SUBLANES), :] = acc
    y = jnp.concatenate([obuf[pl.ds(sl, tm, stride=SUBLANES), :] for sl in range(SUBLANES)], axis=1)
    o_ref[0] = _rms(x2_ref[0] + y, g_ref[...])


def _combine(x2, seg, lidx3, gate3, ys, g_final, b0, nbatch, tm):
    _, seq, d = x2.shape
    nt = seq // tm
    last = (b0 + nbatch) * nt - 1
    tile = lambda b, i: (b0 + b) * nt + i
    buf_rows = -(-(TOP_K * tm + N_EXPERTS * (SEG_ROWS - 1)) // SEG_ROWS) * SEG_ROWS
    smem = lambda width, fn: pl.BlockSpec((1, 1, width), lambda b, i: (fn(b, i), 0, 0), memory_space=pltpu.SMEM)
    return pl.pallas_call(
        functools.partial(_combine_kernel, tm=tm),
        out_shape=jax.ShapeDtypeStruct((nbatch, seq, d), F32),
        grid_spec=pltpu.PrefetchScalarGridSpec(
            num_scalar_prefetch=0,
            grid=(nbatch, nt),
            in_specs=[
                smem(seg.shape[2], tile),
                smem(seg.shape[2], lambda b, i: jnp.minimum(tile(b, i) + 1, last)),
                smem(TOP_K * tm, tile),
                smem(TOP_K * tm, tile),
                pl.BlockSpec((1, tm, d), lambda b, i: (b0 + b, i, 0)),
                pl.BlockSpec((1, d), lambda b, i: (0, 0)),
                pl.BlockSpec(memory_space=pl.ANY),
            ],
            out_specs=pl.BlockSpec((1, tm, d), lambda b, i: (b, i, 0)),
            scratch_shapes=[pltpu.VMEM((2, buf_rows * SUBLANES, LANES), F32),
                            pltpu.VMEM((tm * SUBLANES, LANES), F32),
                            pltpu.SemaphoreType.DMA((2,))],
        ),
        compiler_params=_cparams(("arbitrary", "arbitrary")),
        name="combine",
    )(seg, seg, lidx3, gate3, x2, g_final.reshape(1, d), ys)


def _routing_plan(meta, gates, cnt, te, tm, cap):
    tok = meta.shape[1]
    idx, rank = meta[:TOP_K], meta[TOP_K:]
    counts = cnt[:, 0].astype(I32)
    tiles = (counts + te - 1) // te
    tile_end = jnp.cumsum(tiles)
    tile_start = tile_end - tiles
    offsets = tile_start * te
    n_used = tile_end[-1]
    ntile_max = (tok * TOP_K) // te + N_EXPERTS
    t_ids = jnp.minimum(jnp.arange(ntile_max, dtype=I32), n_used - 1)
    tile_expert = jnp.sum((t_ids[:, None] >= tile_end[None, :]).astype(I32), axis=1)
    first = jnp.sum(jnp.where(tile_expert[:, None] == jnp.arange(N_EXPERTS, dtype=I32)[None, :],
                              tile_start[None, :], 0), axis=1)
    tile_in = tile_expert * (cap // te) + (t_ids - first)

    ng = tok // tm
    onehot = idx.reshape(TOP_K, ng, tm)[..., None] == jnp.arange(N_EXPERTS, dtype=I32)
    n_ge = jnp.sum(onehot.astype(I32), axis=(0, 2))
    before = jnp.cumsum(n_ge, axis=0) - n_ge
    nseg = (n_ge + SEG_ROWS - 1) // SEG_ROWS
    seg_end = jnp.cumsum(nseg, axis=1)
    dst = (seg_end - nseg) * SEG_ROWS
    max_seg = (TOP_K * tm) // SEG_ROWS + N_EXPERTS
    assert max_seg < LANES
    j = jnp.arange(max_seg, dtype=I32)[None, :, None]
    owner = j >= seg_end[:, None, :]
    e_j = jnp.minimum(jnp.sum(owner.astype(I32), axis=-1), N_EXPERTS - 1)
    pick = e_j[..., None] == jnp.arange(N_EXPERTS, dtype=I32)
    first_row = (offsets[None, :] + before - dst)[:, None, :]
    src = jnp.sum(jnp.where(pick, first_row, 0), axis=-1) + j[..., 0] * SEG_ROWS
    seg = jnp.concatenate([seg_end[:, -1:], src, jnp.zeros((ng, LANES - 1 - max_seg), I32)],
                          axis=1).reshape(ng, 1, LANES)
    shift = (dst - before)[None, :, None, :]
    lidx = rank.reshape(TOP_K, ng, tm) + jnp.sum(jnp.where(onehot, shift, 0), axis=-1)
    by_tile = lambda v: v.reshape(TOP_K, ng, tm).transpose(1, 0, 2).reshape(ng, 1, TOP_K * tm)
    return tile_expert, tile_in, t_ids, n_used.reshape(1), seg, by_tile(lidx), by_tile(gates[:TOP_K])


def kernel(x_prompt, x_sample, mem_prompt, mem_sample, g_mix, w_in, sinks, g_attn_out, g_four_out, w_out,
           g_cross, g_mem, w_q_cross, w_kv_cross, w_o_cross, g_moe, w_router, b_router, w_gate_up,
           b_gate_up, w_down, b_down, g_final):
    n_prompt, seq, _ = x_prompt.shape
    n_sample = x_sample.shape[0]
    l = 0
    kvc = _memkv(mem_prompt, mem_sample, g_mem[l], w_kv_cross[l])
    q, kv, ab = _inproj(x_prompt, x_sample, g_mix[l], w_in[l])
    a = _swa(q, kv, sinks[l])
    f = _seqdft(ab)
    tok = (n_prompt + n_sample) * seq
    te = TM_EXPERT
    cap = -(-tok // te) * te
    x2, meta, gates, cnt, xs = _mid(x_prompt, x_sample, a, f, kvc, g_attn_out[l], g_four_out[l], w_out[l],
                                    g_cross[l], w_q_cross[l], w_o_cross[l], g_moe[l], w_router[l], b_router[l],
                                    cap, te)
    tm_rows = min(TM_ROWS, seq)
    tile_expert, tile_in, tile_out, n_used, seg, lidx3, gate3 = _routing_plan(meta, gates, cnt, te, tm_rows, cap)
    ys = _experts(xs, tile_expert, tile_in, tile_out, n_used, w_gate_up[l], b_gate_up[l], w_down[l], b_down[l], te)
    y_p = _combine(x2, seg, lidx3, gate3, ys, g_final, 0, n_prompt, tm_rows)
    y_s = _combine(x2, seg, lidx3, gate3, ys, g_final, n_prompt, n_sample, tm_rows)
    return (y_p, y_s)
```

```python
import functools

import numpy as np
import jax
import jax.numpy as jnp
from jax import lax
from jax.experimental import pallas as pl
from jax.experimental.pallas import tpu as pltpu

F32 = jnp.float32
BF16 = jnp.bfloat16
I32 = jnp.int32

HEAD_DIM = 64
N_Q_HEADS = 8
N_KV_HEADS = 2
ATTN_WIDTH = N_Q_HEADS * HEAD_DIM
KV_WIDTH = N_KV_HEADS * HEAD_DIM
FOURIER_WIDTH = 512
FOURIER_GROUP = 64
WINDOW = 128
ROPE_DIM = 16
ROPE_THETA = 500000.0
N_CROSS_HEADS = 4
CROSS_HEAD_DIM = 128
CROSS_WIDTH = N_CROSS_HEADS * CROSS_HEAD_DIM
N_EXPERTS = 32
TOP_K = 4
SWIGLU_LIMIT = 7.0
SWIGLU_ALPHA = 1.702
EPS = 1e-5
NEG_INF = -1e30
LOG2E = 1.4426950408889634

LANES = 128
SUBLANES = 8
VMEM_LIMIT_BYTES = 56 * 1024 * 1024

TS_INPROJ = 1024
TQ_ATTN = 1024
TR_DFT = 512
TM_MID = 512
TM_ROWS = 512
TM_EXPERT = 512
ISSUE_UNROLL = 4
MID_CHAINS = 1


def _cparams(sem):
    return pltpu.CompilerParams(dimension_semantics=sem, vmem_limit_bytes=VMEM_LIMIT_BYTES)


def _rms(x, g):
    return x * lax.rsqrt(jnp.mean(x * x, axis=-1, keepdims=True) + EPS) * g


def _row_copy(src_rows, dst_rows, sem):
    return pltpu.make_async_copy(src_rows, dst_rows, sem)


def _two_group_maps(n_prompt):
    def prompt_b(b):
        return jnp.minimum(b, n_prompt - 1)

    def sample_b(b):
        return jnp.maximum(b - n_prompt, 0)
    return prompt_b, sample_b


def _memkv_kernel(mp_ref, ms_ref, g_ref, w_ref, o_ref, *, n_prompt):
    b = pl.program_id(0)
    m = jnp.where(b < n_prompt, mp_ref[0], ms_ref[0])
    mn = _rms(m, g_ref[...]).astype(BF16)
    o_ref[0] = jnp.dot(mn, w_ref[...], preferred_element_type=F32).astype(BF16)


def _memkv(mem_p, mem_s, g_mem, w_kv):
    n_prompt, n_mem, d = mem_p.shape
    nb = n_prompt + mem_s.shape[0]
    pb, sb = _two_group_maps(n_prompt)
    return pl.pallas_call(
        functools.partial(_memkv_kernel, n_prompt=n_prompt),
        out_shape=jax.ShapeDtypeStruct((nb, n_mem, w_kv.shape[1]), BF16),
        grid=(nb,),
        in_specs=[
            pl.BlockSpec((1, n_mem, d), lambda b: (pb(b), 0, 0)),
            pl.BlockSpec((1, n_mem, d), lambda b: (sb(b), 0, 0)),
            pl.BlockSpec((1, d), lambda b: (0, 0)),
            pl.BlockSpec(w_kv.shape, lambda b: (0, 0)),
        ],
        out_specs=pl.BlockSpec((1, n_mem, w_kv.shape[1]), lambda b: (b, 0, 0)),
        compiler_params=_cparams(("arbitrary",)),
        name="memkv",
    )(mem_p, mem_s, g_mem.reshape(1, d), w_kv.astype(BF16))


_ROT_W = ATTN_WIDTH + 2 * KV_WIDTH


def _inproj_kernel(xp_ref, xs_ref, g_ref, w_ref, cos_ref, sin_ref, bd_ref,
                   q_ref, kv_ref, ab_ref, *, n_prompt):
    b = pl.program_id(1)
    x = jnp.where(b < n_prompt, xp_ref[0], xs_ref[0])
    h = _rms(x, g_ref[...]).astype(BF16)
    z = jnp.dot(h, w_ref[...], preferred_element_type=F32)
    cos = cos_ref[...]
    sin = sin_ref[...]
    lane = lax.broadcasted_iota(I32, cos.shape, 1) & (HEAD_DIM - 1)
    first_half = lane < ROPE_DIM // 2
    rot = []
    for c in range(_ROT_W // LANES):
        zc = z[:, c * LANES:(c + 1) * LANES]
        partner = jnp.where(first_half,
                            pltpu.roll(zc, LANES - ROPE_DIM // 2, axis=1),
                            pltpu.roll(zc, ROPE_DIM // 2, axis=1))
        rot.append(zc * cos + partner * sin)
    nq = ATTN_WIDTH // LANES
    q_ref[0] = (jnp.concatenate(rot[:nq], axis=1) * (LOG2E * HEAD_DIM ** -0.5)).astype(BF16)
    v_off = _ROT_W
    u_off = _ROT_W + 2 * KV_WIDTH
    kv_ref[0] = jnp.concatenate(rot[nq:] + [z[:, v_off:u_off]], axis=1).astype(BF16)
    half = FOURIER_WIDTH // 2
    u = z[:, u_off:].astype(BF16)
    r0 = jnp.dot(u[:, :half], bd_ref[...], preferred_element_type=F32)
    r1 = jnp.dot(u[:, half:], bd_ref[...], preferred_element_type=F32)
    ab_ref[0] = jnp.concatenate([r0[:, :half], r1[:, :half], r0[:, half:], r1[:, half:]],
                                axis=1).astype(BF16)


def _rope_tables(seq):
    half = ROPE_DIM // 2
    inv_freq = ROPE_THETA ** (-(jnp.arange(half, dtype=F32) * 2.0) / ROPE_DIM)
    ang = jnp.arange(seq).astype(F32)[:, None] * inv_freq[None, :]
    cos, sin = jnp.cos(ang), jnp.sin(ang)
    rest = HEAD_DIM - ROPE_DIM
    cos_h = jnp.concatenate([cos, cos, jnp.ones((seq, rest), F32)], axis=1)
    sin_h = jnp.concatenate([-sin, sin, jnp.zeros((seq, rest), F32)], axis=1)
    reps = LANES // HEAD_DIM
    return jnp.tile(cos_h, (1, reps)), jnp.tile(sin_h, (1, reps))


def _channel_dft_tables():
    n = FOURIER_GROUP
    idx = np.arange(n)
    ang = 2.0 * np.pi * ((idx[:, None] * idx[None, :]) % n) / n
    eye = np.eye(FOURIER_WIDTH // (2 * n))
    return jnp.asarray(np.concatenate([np.kron(eye, np.cos(ang)), np.kron(eye, np.sin(ang))], axis=1), BF16)


def _inproj(x_p, x_s, g_mix, w_in):
    n_prompt, seq, d = x_p.shape
    nb = n_prompt + x_s.shape[0]
    ts = min(TS_INPROJ, seq)
    o1, o2, o3 = ATTN_WIDTH, ATTN_WIDTH + KV_WIDTH, ATTN_WIDTH + 2 * KV_WIDTH
    swap = lambda w: jnp.concatenate([w[:, HEAD_DIM:], w[:, :HEAD_DIM]], axis=1)
    wk, wv = w_in[:, o1:o2], w_in[:, o2:o3]
    w_ext = jnp.concatenate([w_in[:, :o1], wk, swap(wk), wv, swap(wv), w_in[:, o3:]], axis=1).astype(BF16)
    cos_t, sin_t = _rope_tables(seq)
    bd = _channel_dft_tables()
    pb, sb = _two_group_maps(n_prompt)
    kvw = 4 * KV_WIDTH
    return pl.pallas_call(
        functools.partial(_inproj_kernel, n_prompt=n_prompt),
        out_shape=(jax.ShapeDtypeStruct((nb, seq, ATTN_WIDTH), BF16),
                   jax.ShapeDtypeStruct((nb, seq, kvw), BF16),
                   jax.ShapeDtypeStruct((nb, seq, 2 * FOURIER_WIDTH), BF16)),
        grid=(seq // ts, nb),
        in_specs=[
            pl.BlockSpec((1, ts, d), lambda i, b: (pb(b), i, 0)),
            pl.BlockSpec((1, ts, d), lambda i, b: (sb(b), i, 0)),
            pl.BlockSpec((1, d), lambda i, b: (0, 0)),
            pl.BlockSpec(w_ext.shape, lambda i, b: (0, 0)),
            pl.BlockSpec((ts, LANES), lambda i, b: (i, 0)),
            pl.BlockSpec((ts, LANES), lambda i, b: (i, 0)),
            pl.BlockSpec(bd.shape, lambda i, b: (0, 0)),
        ],
        out_specs=(pl.BlockSpec((1, ts, ATTN_WIDTH), lambda i, b: (b, i, 0)),
                   pl.BlockSpec((1, ts, kvw), lambda i, b: (b, i, 0)),
                   pl.BlockSpec((1, ts, 2 * FOURIER_WIDTH), lambda i, b: (b, i, 0))),
        compiler_params=_cparams(("arbitrary", "arbitrary")),
        name="inproj",
    )(x_p, x_s, g_mix.reshape(1, d), w_ext, cos_t, sin_t, bd)


def _swa_kernel(sink_ref, q_ref, kvm_ref, kvp_ref, kvn_ref, bias_ref, o_ref, *, nblk):
    i = pl.program_id(1)
    last = pl.num_programs(1) - 1
    kvw = jnp.concatenate([kvp_ref[0], kvm_ref[0], kvn_ref[0]], axis=0)
    k, ksw, v, vsw = [kvw[:, c * LANES:(c + 1) * LANES] for c in range(4)]
    lo_kv = lax.broadcasted_iota(I32, k.shape, 1) < HEAD_DIM
    k_dup = (jnp.where(lo_kv, k, ksw), jnp.where(lo_kv, ksw, k))
    v_dup = (jnp.where(lo_kv, v, vsw), jnp.where(lo_kv, vsw, v))
    lo_q = lax.broadcasted_iota(I32, (WINDOW, LANES), 1) < HEAD_DIM
    row = lax.broadcasted_iota(I32, (4 * WINDOW, 1), 0)
    for j in range(nblk):
        bias = bias_ref[1]
        if j == 0:
            bias = jnp.where(i == 0, bias_ref[0], bias)
        if j == nblk - 1:
            bias = jnp.where(i == last, bias_ref[2], bias)
        bias4 = jnp.concatenate([bias] * 4, axis=0)
        rows = slice(j * WINDOW, (j + 1) * WINDOW)
        win = slice(j * WINDOW, (j + 3) * WINDOW)
        for kvh in range(N_KV_HEADS):
            c0 = 2 * kvh
            qa = q_ref[0, rows, c0 * LANES:(c0 + 1) * LANES]
            qb = q_ref[0, rows, (c0 + 1) * LANES:(c0 + 2) * LANES]
            zero = jnp.zeros_like(qa)
            q4 = jnp.concatenate([jnp.where(lo_q, qa, zero), jnp.where(lo_q, zero, qa),
                                  jnp.where(lo_q, qb, zero), jnp.where(lo_q, zero, qb)], axis=0)
            s = lax.dot_general(q4, k_dup[kvh][win], (((1,), (1,)), ((), ())),
                                preferred_element_type=F32) + bias4
            h0 = 4 * kvh
            sink = jnp.where(row < WINDOW, sink_ref[h0],
                             jnp.where(row < 2 * WINDOW, sink_ref[h0 + 1],
                                       jnp.where(row < 3 * WINDOW, sink_ref[h0 + 2], sink_ref[h0 + 3])))
            m = jnp.maximum(jnp.max(s, axis=-1, keepdims=True), sink)
            e = jnp.exp2(s - m)
            den = jnp.sum(e, axis=-1, keepdims=True) + jnp.exp2(sink - m)
            o = jnp.dot(e.astype(BF16), v_dup[kvh][win], preferred_element_type=F32) * (1.0 / den)
            o_ref[0, rows, c0 * LANES:(c0 + 1) * LANES] = jnp.where(
                lo_q, o[:WINDOW], o[WINDOW:2 * WINDOW]).astype(o_ref.dtype)
            o_ref[0, rows, (c0 + 1) * LANES:(c0 + 2) * LANES] = jnp.where(
                lo_q, o[2 * WINDOW:3 * WINDOW], o[3 * WINDOW:]).astype(o_ref.dtype)


def _swa_bias():
    qi = np.arange(WINDOW)[:, None]
    c = np.arange(3 * WINDOW)[None, :]
    band = np.abs(c - WINDOW - qi) <= WINDOW
    first = band & (c >= WINDOW)
    lastb = band & (c < 2 * WINDOW)
    tab = np.stack([first, band, lastb]).astype(np.float32)
    return jnp.asarray((1.0 - tab) * NEG_INF, F32)


def _swa(q, kv, sinks):
    nb, seq, _ = q.shape
    tq = min(TQ_ATTN, seq)
    nblk = tq // WINDOW
    nseq_blk = seq // WINDOW
    kvw = kv.shape[-1]
    assert nblk >= 2 and seq % tq == 0, "first/last window masks are applied to distinct query blocks"
    bias = _swa_bias()
    return pl.pallas_call(
        functools.partial(_swa_kernel, nblk=nblk),
        out_shape=jax.ShapeDtypeStruct((nb, seq, ATTN_WIDTH), BF16),
        grid=(nb, seq // tq),
        in_specs=[
            pl.BlockSpec(memory_space=pltpu.SMEM),
            pl.BlockSpec((1, tq, ATTN_WIDTH), lambda b, i: (b, i, 0)),
            pl.BlockSpec((1, tq, kvw), lambda b, i: (b, i, 0)),
            pl.BlockSpec((1, WINDOW, kvw), lambda b, i: (b, jnp.maximum(i * nblk - 1, 0), 0)),
            pl.BlockSpec((1, WINDOW, kvw), lambda b, i: (b, jnp.minimum((i + 1) * nblk, nseq_blk - 1), 0)),
            pl.BlockSpec(bias.shape, lambda b, i: (0, 0, 0)),
        ],
        out_specs=pl.BlockSpec((1, tq, ATTN_WIDTH), lambda b, i: (b, i, 0)),
        compiler_params=_cparams(("arbitrary", "arbitrary")),
        name="swa",
    )(sinks.astype(F32) * LOG2E, q, kv, kv, kv, bias)


_DFT_PAD_ROWS = 16


def _seqdft_kernel(ab_ref, ct_hbm, st_hbm, perm_ref, o_ref, ct, st, aebo, hbuf, sem, *, scale, tk):
    n = ab_ref.shape[1]
    m = n // 2
    fw = FOURIER_WIDTH
    nblk = m // LANES

    @pl.when(pl.program_id(0) == 0)
    def _():
        copies = [pltpu.make_async_copy(ct_hbm, ct, sem.at[0]), pltpu.make_async_copy(st_hbm, st, sem.at[1])]
        for c in copies:
            c.start()
        for c in copies:
            c.wait()

    perm = perm_ref[...]
    for blk in range(nblk):
        mirrored = [ab_ref[0, n - LANES * (blk + 1):n - LANES * blk, :]]
        if blk > 0:
            mirrored.append(ab_ref[0, n - LANES * blk:n - LANES * (blk - 1), :])
        else:
            mirrored.append(jnp.zeros((LANES, 2 * fw), BF16))
        r = jnp.dot(perm, jnp.concatenate(mirrored, axis=0), preferred_element_type=F32)
        cur = ab_ref[0, LANES * blk:LANES * (blk + 1), :].astype(F32)
        aebo[LANES * blk:LANES * (blk + 1), :] = jnp.concatenate(
            [cur[:, :fw] + r[:, :fw], cur[:, fw:] - r[:, fw:]], axis=1).astype(BF16)

    a_mid = ab_ref[0, m:m + _DFT_PAD_ROWS, :fw][0:1, :].astype(F32)

    def pq(rows, nrows):
        p = jnp.dot(ct[rows, :], aebo[:, :fw], preferred_element_type=F32)
        q = jnp.dot(st[rows, :], aebo[:, fw:], preferred_element_type=F32)
        odd = (lax.broadcasted_iota(I32, (nrows, fw), 0) & 1) == 1
        return p + jnp.where(odd, -a_mid, a_mid), q

    for kt in range(m // tk):
        rows = slice(kt * tk, (kt + 1) * tk)
        p, q = pq(rows, tk)
        o_ref[0, rows, :] = ((p - q) * scale).astype(o_ref.dtype)
        hbuf[rows, :] = p + q
    p, q = pq(slice(m, m + _DFT_PAD_ROWS), _DFT_PAD_ROWS)
    hbuf[m:m + _DFT_PAD_ROWS, :] = p + q
    hbuf[m + _DFT_PAD_ROWS:, :] = jnp.zeros((LANES - _DFT_PAD_ROWS, fw), F32)
    for c in range(nblk):
        win = hbuf[LANES * c:LANES * (c + 2), :].astype(BF16)
        blk_out = jnp.dot(perm, win, preferred_element_type=F32)
        o_ref[0, n - LANES * (c + 1):n - LANES * c, :] = (blk_out * scale).astype(o_ref.dtype)


def _seq_dft_tables(seq):
    m = seq // 2
    k = np.arange(m + _DFT_PAD_ROWS)[:, None]
    ang = 2.0 * np.pi * ((k * np.arange(m)[None, :]) % seq) / seq
    return jnp.asarray(np.cos(ang), BF16), jnp.asarray(np.sin(ang), BF16)


def _seqdft(ab):
    nb, seq, _ = ab.shape
    m = seq // 2
    tk = min(TR_DFT, m)
    assert m % LANES == 0 and m % tk == 0 and m % 64 == 0
    ct, st = _seq_dft_tables(seq)
    perm = np.zeros((LANES, 2 * LANES), np.float32)
    perm[np.arange(LANES), LANES - np.arange(LANES)] = 1.0
    scale = float(1.0 / np.sqrt(seq * FOURIER_GROUP))
    return pl.pallas_call(
        functools.partial(_seqdft_kernel, scale=scale, tk=tk),
        out_shape=jax.ShapeDtypeStruct((nb, seq, FOURIER_WIDTH), BF16),
        grid=(nb,),
        in_specs=[
            pl.BlockSpec((1, seq, 2 * FOURIER_WIDTH), lambda b: (b, 0, 0)),
            pl.BlockSpec(memory_space=pl.ANY),
            pl.BlockSpec(memory_space=pl.ANY),
            pl.BlockSpec(perm.shape, lambda b: (0, 0)),
        ],
        out_specs=pl.BlockSpec((1, seq, FOURIER_WIDTH), lambda b: (b, 0, 0)),
        scratch_shapes=[pltpu.VMEM(ct.shape, BF16), pltpu.VMEM(st.shape, BF16),
                        pltpu.VMEM((m, 2 * FOURIER_WIDTH), BF16),
                        pltpu.VMEM((m + LANES, FOURIER_WIDTH), F32),
                        pltpu.SemaphoreType.DMA((2,))],
        compiler_params=_cparams(("arbitrary",)),
        name="seqdft",
    )(ab, ct, st, jnp.asarray(perm, BF16))


def _mid_kernel(xp_ref, xs_ref, a_ref, f_ref, kv_ref, gao_ref, gfo_ref, wout_ref, gc_ref, wq_ref,
                wo_ref, gm_ref, wr_ref, br_ref, tri_ref,
                x2_ref, meta_ref, gate_ref, cnt_ref, slots_ref,
                cnt_sc, stage, posv, pos_sm, cntv, cnt_sm, zbuf, rsem, psem, zsem, *, n_prompt, cap, te):
    b = pl.program_id(0)
    i = pl.program_id(1)
    tm = a_ref.shape[1]
    step = b * pl.num_programs(1) + i
    nsteps = pl.num_programs(0) * pl.num_programs(1)
    tile_rows = tm * SUBLANES
    dump_base = N_EXPERTS * cap

    def wait_scatter(sem_idx):
        for _ in range(TOP_K):
            _row_copy(stage.at[0], slots_ref.at[pl.ds(0, tile_rows), :], rsem.at[sem_idx]).wait()

    @pl.when(step == 0)
    def _():
        cnt_sc[...] = jnp.zeros_like(cnt_sc)
        stage[2] = jnp.zeros((tile_rows, LANES), F32)

        def init(t, carry):
            for k in range(TOP_K):
                pos_sm[1, k, t] = dump_base + k * tm + t
            return carry

        lax.fori_loop(0, tm, init, 0)

    @pl.when(step >= 2)
    def _():
        wait_scatter(step % 2)

    prev_stage = (step + 2) % 3
    prev_pos = (step + 1) % 2
    for t in range(tm):
        for k in range(TOP_K):
            r0 = pl.multiple_of(pos_sm[prev_pos, k, t] * SUBLANES, SUBLANES)
            _row_copy(stage.at[prev_stage, pl.ds(t * SUBLANES, SUBLANES), :],
                      slots_ref.at[pl.ds(r0, SUBLANES), :], rsem.at[step % 2]).start(priority=k % 2)
    cur_stage = step % 3

    def token_chain(r0, nr):
        rows = slice(r0, r0 + nr)
        x = jnp.where(b < n_prompt, xp_ref[0, rows, :], xs_ref[0, rows, :])
        an = _rms(a_ref[0, rows, :].astype(F32), gao_ref[...]).astype(BF16)
        fn = _rms(f_ref[0, rows, :].astype(F32), gfo_ref[...]).astype(BF16)
        x1 = (x + jnp.dot(an, wout_ref[:ATTN_WIDTH, :], preferred_element_type=F32)
              + jnp.dot(fn, wout_ref[ATTN_WIDTH:, :], preferred_element_type=F32))

        h2 = _rms(x1, gc_ref[...]).astype(BF16)
        qc = (jnp.dot(h2, wq_ref[...], preferred_element_type=F32)
              * (LOG2E * CROSS_HEAD_DIM ** -0.5)).astype(BF16)
        heads = []
        for hd in range(N_CROSS_HEADS):
            cols = slice(hd * CROSS_HEAD_DIM, (hd + 1) * CROSS_HEAD_DIM)
            vcols = slice(CROSS_WIDTH + hd * CROSS_HEAD_DIM, CROSS_WIDTH + (hd + 1) * CROSS_HEAD_DIM)
            s = lax.dot_general(qc[:, cols], kv_ref[0, :, cols], (((1,), (1,)), ((), ())),
                                preferred_element_type=F32)
            e = jnp.exp2(s - jnp.max(s, axis=-1, keepdims=True))
            den = jnp.sum(e, axis=-1, keepdims=True)
            heads.append(jnp.dot(e.astype(BF16), kv_ref[0, :, vcols], preferred_element_type=F32) * (1.0 / den))
        oc = jnp.concatenate(heads, axis=1).astype(BF16)
        x2 = x1 + jnp.dot(oc, wo_ref[...], preferred_element_type=F32)
        x2_ref[0, rows, :] = x2

        h3 = _rms(x2, gm_ref[...])
        for sl in range(SUBLANES):
            stage[cur_stage, pl.ds(r0 * SUBLANES + sl, nr, stride=SUBLANES), :] = h3[:, sl * LANES:(sl + 1) * LANES]

        hi = h3.astype(BF16)
        lo = (h3 - hi.astype(F32)).astype(BF16)
        hw = jnp.dot(hi, wr_ref[...], preferred_element_type=F32)
        return (hw[:, :LANES] + hw[:, LANES:]
                + jnp.dot(lo, wr_ref[:, :LANES], preferred_element_type=F32) + br_ref[...])

    nr = tm // MID_CHAINS
    logits = jnp.concatenate([token_chain(c * nr, nr) for c in range(MID_CHAINS)], axis=0)
    lt = logits.T[:N_EXPERTS, :]
    eidx = lax.broadcasted_iota(I32, lt.shape, 0)
    vals, idxs, sels = [], [], []
    for _ in range(TOP_K):
        m = jnp.max(lt, axis=0, keepdims=True)
        idx = jnp.min(jnp.where(lt == m, eidx, N_EXPERTS), axis=0, keepdims=True)
        sel = eidx == idx
        lt = jnp.where(sel, -jnp.inf, lt)
        vals.append(m)
        idxs.append(idx)
        sels.append(sel)
    ex = [jnp.exp(vk - vals[0]) for vk in vals]
    inv = 1.0 / (ex[0] + ex[1] + ex[2] + ex[3])
    zrow = jnp.zeros_like(ex[0])
    gate_ref[...] = jnp.concatenate([e_ * inv for e_ in ex] + [zrow] * (SUBLANES - TOP_K), axis=0)

    onehot = jnp.zeros(lt.shape, F32)
    for sel in sels:
        onehot = onehot + sel.astype(F32)
    before = cnt_sc[:, 0:1] + jnp.dot(onehot.astype(BF16), tri_ref[...], preferred_element_type=F32)
    ranks = [jnp.sum(jnp.where(sel, before, 0.0), axis=0, keepdims=True).astype(I32) for sel in sels]
    meta_ref[...] = jnp.concatenate(idxs + ranks, axis=0)
    cnt_sc[...] = cnt_sc[...] + jnp.sum(onehot, axis=1, keepdims=True)
    cnt_ref[...] = cnt_sc[...]

    posv[...] = jnp.concatenate([ix * cap + rk for ix, rk in zip(idxs, ranks)], axis=0)
    to_smem = pltpu.make_async_copy(posv, pos_sm.at[step % 2], psem)
    to_smem.start()
    to_smem.wait()

    @pl.when(step == nsteps - 1)
    def _():
        @pl.when(step >= 1)
        def _():
            wait_scatter((step + 1) % 2)

        def issue(tb, carry):
            for u in range(ISSUE_UNROLL):
                t = tb * ISSUE_UNROLL + u
                for k in range(TOP_K):
                    r0 = pl.multiple_of(pos_sm[step % 2, k, t] * SUBLANES, SUBLANES)
                    _row_copy(stage.at[cur_stage, pl.ds(pl.multiple_of(t * SUBLANES, SUBLANES), SUBLANES), :],
                              slots_ref.at[pl.ds(r0, SUBLANES), :], rsem.at[(step + 1) % 2]).start(priority=k % 2)
            return carry

        lax.fori_loop(0, tm // ISSUE_UNROLL, issue, 0)
        wait_scatter(step % 2)
        wait_scatter((step + 1) % 2)

        cntv[...] = cnt_sc[...].astype(I32)
        counts_to_smem = pltpu.make_async_copy(cntv, cnt_sm, psem)
        counts_to_smem.start()
        counts_to_smem.wait()
        zbuf[...] = jnp.zeros_like(zbuf)
        zrows = zbuf.shape[0] // SUBLANES
        bits = [1 << j for j in range(zrows.bit_length() - 1, -1, -1)]
        assert sum(bits) == 2 * zrows - 1 and zrows * 2 == te

        def fill(start):
            for e in range(N_EXPERTS):
                cnt = cnt_sm[e, 0]
                pad = (te - cnt % te) % te
                row = e * cap + cnt
                for bit in bits:
                    @pl.when((pad & bit) != 0)
                    def _(row=row, bit=bit):
                        cp = _row_copy(zbuf.at[pl.ds(0, bit * SUBLANES), :],
                                       slots_ref.at[pl.ds(pl.multiple_of(row * SUBLANES, SUBLANES),
                                                          bit * SUBLANES), :], zsem)
                        cp.start() if start else cp.wait()
                    row = row + (pad & bit)

        fill(True)
        fill(False)


def _mid(x_p, x_s, a, f, kvc, g_attn_out, g_four_out, w_out, g_cross, w_q, w_o, g_moe, w_router, b_router,
         cap, te):
    n_prompt, seq, d = x_p.shape
    nb = a.shape[0]
    tm = min(TM_MID, seq)
    nt = seq // tm
    n_mem = kvc.shape[1]
    pb, sb = _two_group_maps(n_prompt)
    wr = jnp.zeros((d, LANES), F32).at[:, :N_EXPERTS].set(w_router)
    wr_hi = wr.astype(BF16)
    wr_lo = (wr - wr_hi.astype(F32)).astype(BF16)
    br = jnp.zeros((1, LANES), F32).at[0, :N_EXPERTS].set(b_router)
    tri = jnp.asarray(np.triu(np.ones((tm, tm), np.float32), 1), BF16)
    full = lambda arr: pl.BlockSpec(arr.shape, lambda b, i: (0,) * arr.ndim)
    row = lambda v: v.reshape(1, -1)
    args = [x_p, x_s, a, f, kvc, row(g_attn_out), row(g_four_out), w_out.astype(BF16), row(g_cross),
            w_q.astype(BF16), w_o.astype(BF16), row(g_moe), jnp.concatenate([wr_hi, wr_lo], axis=1), br, tri]
    in_specs = [
        pl.BlockSpec((1, tm, d), lambda b, i: (pb(b), i, 0)),
        pl.BlockSpec((1, tm, d), lambda b, i: (sb(b), i, 0)),
        pl.BlockSpec((1, tm, ATTN_WIDTH), lambda b, i: (b, i, 0)),
        pl.BlockSpec((1, tm, FOURIER_WIDTH), lambda b, i: (b, i, 0)),
        pl.BlockSpec((1, n_mem, 2 * CROSS_WIDTH), lambda b, i: (b, 0, 0)),
    ] + [full(v) for v in args[5:]]
    tok = nb * seq
    slot_rows = N_EXPERTS * cap + TOP_K * tm
    return pl.pallas_call(
        functools.partial(_mid_kernel, n_prompt=n_prompt, cap=cap, te=te),
        out_shape=(jax.ShapeDtypeStruct((nb, seq, d), F32),
                   jax.ShapeDtypeStruct((SUBLANES, tok), I32),
                   jax.ShapeDtypeStruct((SUBLANES, tok), F32),
                   jax.ShapeDtypeStruct((N_EXPERTS, LANES), F32),
                   jax.ShapeDtypeStruct((slot_rows * SUBLANES, LANES), F32)),
        grid=(nb, nt),
        in_specs=in_specs,
        out_specs=(pl.BlockSpec((1, tm, d), lambda b, i: (b, i, 0)),
                   pl.BlockSpec((SUBLANES, tm), lambda b, i: (0, b * nt + i)),
                   pl.BlockSpec((SUBLANES, tm), lambda b, i: (0, b * nt + i)),
                   pl.BlockSpec((N_EXPERTS, LANES), lambda b, i: (0, 0)),
                   pl.BlockSpec(memory_space=pl.ANY)),
        scratch_shapes=[pltpu.VMEM((N_EXPERTS, LANES), F32),
                        pltpu.VMEM((3, tm * SUBLANES, LANES), F32),
                        pltpu.VMEM((TOP_K, tm), I32), pltpu.SMEM((2, TOP_K, tm), I32),
                        pltpu.VMEM((N_EXPERTS, LANES), I32), pltpu.SMEM((N_EXPERTS, LANES), I32),
                        pltpu.VMEM((te // 2 * SUBLANES, LANES), F32),
                        pltpu.SemaphoreType.DMA((2,)), pltpu.SemaphoreType.DMA, pltpu.SemaphoreType.DMA],
        compiler_params=_cparams(("arbitrary", "arbitrary")),
        name="mid",
    )(*args)


def _expert_kernel(te_ref, ti_ref, to_ref, nu_ref, xs_ref, gs_ref, wgu_ref, bgu_ref, wd_ref, bd_ref, ys_ref,
                   wgu_bf, wd_bf):
    i = pl.program_id(0)
    te = xs_ref.shape[0] // SUBLANES
    dff = wd_ref.shape[1]

    @pl.when((i == 0) | (te_ref[i] != te_ref[jnp.maximum(i - 1, 0)]))
    def _():
        wgu_bf[...] = wgu_ref[0].astype(BF16)
        wd_bf[...] = wd_ref[0].astype(BF16)

    @pl.when(i < nu_ref[0])
    def _():
        x = jnp.concatenate([xs_ref[pl.ds(sl, te, stride=SUBLANES), :] for sl in range(SUBLANES)],
                            axis=1).astype(BF16)
        gu = jnp.dot(x, wgu_bf[...], preferred_element_type=F32) + bgu_ref[0]
        gate = jnp.minimum(gu[:, :dff], SWIGLU_LIMIT)
        up = jnp.clip(gu[:, dff:], -SWIGLU_LIMIT, SWIGLU_LIMIT)
        act = (up + 1.0) * (gate * (1.0 / (1.0 + jnp.exp(-SWIGLU_ALPHA * gate))))
        y = jnp.dot(act.astype(BF16), wd_bf[...], preferred_element_type=F32) + bd_ref[0]
        y = y * gs_ref[...]
        for sl in range(SUBLANES):
            ys_ref[pl.ds(sl, te, stride=SUBLANES), :] = y[:, sl * LANES:(sl + 1) * LANES]


def _experts(xs, gslot, tile_expert, tile_in, tile_out, n_used, w_gate_up, b_gate_up, w_down, b_down, te):
    ne, d, two_dff = w_gate_up.shape
    dff = two_dff // 2
    ntile = tile_expert.shape[0]
    return pl.pallas_call(
        _expert_kernel,
        out_shape=jax.ShapeDtypeStruct(((ntile * te + SEG_ROWS) * SUBLANES, LANES), F32),
        grid_spec=pltpu.PrefetchScalarGridSpec(
            num_scalar_prefetch=4,
            grid=(ntile,),
            in_specs=[
                pl.BlockSpec((te * SUBLANES, LANES), lambda i, te_r, ti_r, to_r, nu_r: (ti_r[i], 0)),
                pl.BlockSpec((te, 1), lambda i, te_r, ti_r, to_r, nu_r: (ti_r[i], 0)),
                pl.BlockSpec((1, d, two_dff), lambda i, te_r, ti_r, to_r, nu_r: (te_r[i], 0, 0)),
                pl.BlockSpec((1, 1, two_dff), lambda i, te_r, ti_r, to_r, nu_r: (te_r[i], 0, 0)),
                pl.BlockSpec((1, dff, d), lambda i, te_r, ti_r, to_r, nu_r: (te_r[i], 0, 0)),
                pl.BlockSpec((1, 1, d), lambda i, te_r, ti_r, to_r, nu_r: (te_r[i], 0, 0)),
            ],
            out_specs=pl.BlockSpec((te * SUBLANES, LANES), lambda i, te_r, ti_r, to_r, nu_r: (to_r[i], 0)),
            scratch_shapes=[pltpu.VMEM((d, two_dff), BF16), pltpu.VMEM((dff, d), BF16)],
        ),
        compiler_params=_cparams(("arbitrary",)),
        name="experts",
    )(tile_expert, tile_in, tile_out, n_used, xs, gslot, w_gate_up, b_gate_up.reshape(ne, 1, two_dff),
      w_down, b_down.reshape(ne, 1, d))


SEG_ROWS = 32


def _combine_kernel(seg_ref, segn_ref, lidx_ref, x2_ref, g_ref, ys_ref, o_ref, buf, obuf, sem, *, tm):
    step = pl.program_id(0) * pl.num_programs(1) + pl.program_id(1)
    nsteps = pl.num_programs(0) * pl.num_programs(1)
    slot = step % 2
    seg_tile = SEG_ROWS * SUBLANES

    def segments(seg, to_slot, start):
        def per_segment(j, carry):
            src = pl.multiple_of(seg[0, 0, 1 + j] * SUBLANES, SUBLANES)
            dst = pl.multiple_of(j * seg_tile, seg_tile)
            cp = _row_copy(ys_ref.at[pl.ds(src, seg_tile), :], buf.at[to_slot, pl.ds(dst, seg_tile), :],
                           sem.at[to_slot])
            if start:
                cp.start()
            else:
                cp.wait()
            return carry

        lax.fori_loop(0, seg[0, 0, 0], per_segment, 0)

    @pl.when(step == 0)
    def _():
        segments(seg_ref, 0, True)

    @pl.when(step + 1 < nsteps)
    def _():
        segments(segn_ref, 1 - slot, True)

    segments(seg_ref, slot, False)

    for t in range(tm):
        acc = None
        for k in range(TOP_K):
            p = pl.multiple_of(lidx_ref[0, 0, k * tm + t] * SUBLANES, SUBLANES)
            term = buf[slot, pl.ds(p, SUBLANES), :]
            acc = term if acc is None else acc + term
        obuf[pl.ds(t * SUBLANES, SUBLANES), :] = acc
    y = jnp.concatenate([obuf[pl.ds(sl, tm, stride=SUBLANES), :] for sl in range(SUBLANES)], axis=1)
    o_ref[0] = _rms(x2_ref[0] + y, g_ref[...])


def _combine(x2, seg, lidx3, ys, g_final, b0, nbatch, tm):
    _, seq, d = x2.shape
    nt = seq // tm
    last = (b0 + nbatch) * nt - 1
    tile = lambda b, i: (b0 + b) * nt + i
    buf_rows = -(-(TOP_K * tm + N_EXPERTS * (SEG_ROWS - 1)) // SEG_ROWS) * SEG_ROWS
    smem = lambda width, fn: pl.BlockSpec((1, 1, width), lambda b, i: (fn(b, i), 0, 0), memory_space=pltpu.SMEM)
    return pl.pallas_call(
        functools.partial(_combine_kernel, tm=tm),
        out_shape=jax.ShapeDtypeStruct((nbatch, seq, d), F32),
        grid_spec=pltpu.PrefetchScalarGridSpec(
            num_scalar_prefetch=0,
            grid=(nbatch, nt),
            in_specs=[
                smem(seg.shape[2], tile),
                smem(seg.shape[2], lambda b, i: jnp.minimum(tile(b, i) + 1, last)),
                smem(TOP_K * tm, tile),
                pl.BlockSpec((1, tm, d), lambda b, i: (b0 + b, i, 0)),
                pl.BlockSpec((1, d), lambda b, i: (0, 0)),
                pl.BlockSpec(memory_space=pl.ANY),
            ],
            out_specs=pl.BlockSpec((1, tm, d), lambda b, i: (b, i, 0)),
            scratch_shapes=[pltpu.VMEM((2, buf_rows * SUBLANES, LANES), F32),
                            pltpu.VMEM((tm * SUBLANES, LANES), F32),
                            pltpu.SemaphoreType.DMA((2,))],
        ),
        compiler_params=_cparams(("arbitrary", "arbitrary")),
        name="combine",
    )(seg, seg, lidx3, x2, g_final.reshape(1, d), ys)


def _routing_plan(meta, gates, cnt, te, tm, cap):
    tok = meta.shape[1]
    idx, rank = meta[:TOP_K], meta[TOP_K:]
    counts = cnt[:, 0].astype(I32)
    tiles = (counts + te - 1) // te
    tile_end = jnp.cumsum(tiles)
    tile_start = tile_end - tiles
    offsets = tile_start * te
    n_used = tile_end[-1]
    ntile_max = (tok * TOP_K) // te + N_EXPERTS
    t_ids = jnp.minimum(jnp.arange(ntile_max, dtype=I32), n_used - 1)
    tile_expert = jnp.sum((t_ids[:, None] >= tile_end[None, :]).astype(I32), axis=1)
    first = jnp.sum(jnp.where(tile_expert[:, None] == jnp.arange(N_EXPERTS, dtype=I32)[None, :],
                              tile_start[None, :], 0), axis=1)
    tile_in = tile_expert * (cap // te) + (t_ids - first)

    ng = tok // tm
    onehot = idx.reshape(TOP_K, ng, tm)[..., None] == jnp.arange(N_EXPERTS, dtype=I32)
    n_ge = jnp.sum(onehot.astype(I32), axis=(0, 2))
    before = jnp.cumsum(n_ge, axis=0) - n_ge
    nseg = (n_ge + SEG_ROWS - 1) // SEG_ROWS
    seg_end = jnp.cumsum(nseg, axis=1)
    dst = (seg_end - nseg) * SEG_ROWS
    max_seg = (TOP_K * tm) // SEG_ROWS + N_EXPERTS
    assert max_seg < LANES
    j = jnp.arange(max_seg, dtype=I32)[None, :, None]
    owner = j >= seg_end[:, None, :]
    e_j = jnp.minimum(jnp.sum(owner.astype(I32), axis=-1), N_EXPERTS - 1)
    pick = e_j[..., None] == jnp.arange(N_EXPERTS, dtype=I32)
    first_row = (offsets[None, :] + before - dst)[:, None, :]
    src = jnp.sum(jnp.where(pick, first_row, 0), axis=-1) + j[..., 0] * SEG_ROWS
    seg = jnp.concatenate([seg_end[:, -1:], src, jnp.zeros((ng, LANES - 1 - max_seg), I32)],
                          axis=1).reshape(ng, 1, LANES)
    shift = (dst - before)[None, :, None, :]
    lidx = rank.reshape(TOP_K, ng, tm) + jnp.sum(jnp.where(onehot, shift, 0), axis=-1)
    by_tile = lambda v: v.reshape(TOP_K, ng, tm).transpose(1, 0, 2).reshape(ng, 1, TOP_K * tm)
    gslot = jnp.zeros((N_EXPERTS * cap,), F32).at[(idx * cap + rank).reshape(-1)].add(
        gates[:TOP_K].reshape(-1), unique_indices=True).reshape(-1, 1)
    return tile_expert, tile_in, t_ids, n_used.reshape(1), seg, by_tile(lidx), gslot


def kernel(x_prompt, x_sample, mem_prompt, mem_sample, g_mix, w_in, sinks, g_attn_out, g_four_out, w_out,
           g_cross, g_mem, w_q_cross, w_kv_cross, w_o_cross, g_moe, w_router, b_router, w_gate_up,
           b_gate_up, w_down, b_down, g_final):
    n_prompt, seq, _ = x_prompt.shape
    n_sample = x_sample.shape[0]
    l = 0
    kvc = _memkv(mem_prompt, mem_sample, g_mem[l], w_kv_cross[l])
    q, kv, ab = _inproj(x_prompt, x_sample, g_mix[l], w_in[l])
    a = _swa(q, kv, sinks[l])
    f = _seqdft(ab)
    tok = (n_prompt + n_sample) * seq
    te = TM_EXPERT
    cap = -(-tok // te) * te
    x2, meta, gates, cnt, xs = _mid(x_prompt, x_sample, a, f, kvc, g_attn_out[l], g_four_out[l], w_out[l],
                                    g_cross[l], w_q_cross[l], w_o_cross[l], g_moe[l], w_router[l], b_router[l],
                                    cap, te)
    tm_rows = min(TM_ROWS, seq)
    tile_expert, tile_in, tile_out, n_used, seg, lidx3, gslot = _routing_plan(meta, gates, cnt, te, tm_rows, cap)
    ys = _experts(xs, gslot, tile_expert, tile_in, tile_out, n_used, w_gate_up[l], b_gate_up[l], w_down[l],
                  b_down[l], te)
    y_p = _combine(x2, seg, lidx3, ys, g_final, 0, n_prompt, tm_rows)
    y_s = _combine(x2, seg, lidx3, ys, g_final, n_prompt, n_sample, tm_rows)
    return (y_p, y_s)
```

```python
import functools

import numpy as np
import jax
import jax.numpy as jnp
from jax import lax
from jax.experimental import pallas as pl
from jax.experimental.pallas import tpu as pltpu

F32 = jnp.float32
BF16 = jnp.bfloat16
I32 = jnp.int32

HEAD_DIM = 64
N_Q_HEADS = 8
N_KV_HEADS = 2
ATTN_WIDTH = N_Q_HEADS * HEAD_DIM
KV_WIDTH = N_KV_HEADS * HEAD_DIM
FOURIER_WIDTH = 512
FOURIER_GROUP = 64
WINDOW = 128
ROPE_DIM = 16
ROPE_THETA = 500000.0
N_CROSS_HEADS = 4
CROSS_HEAD_DIM = 128
CROSS_WIDTH = N_CROSS_HEADS * CROSS_HEAD_DIM
N_EXPERTS = 32
TOP_K = 4
SWIGLU_LIMIT = 7.0
SWIGLU_ALPHA = 1.702
EPS = 1e-5
NEG_INF = -1e30
LOG2E = 1.4426950408889634

LANES = 128
SUBLANES = 8
VMEM_LIMIT_BYTES = 56 * 1024 * 1024

TS_INPROJ = 1024
TQ_ATTN = 1024
TR_DFT = 512
TM_MID = 512
TM_ROWS = 512
TM_EXPERT = 512
ISSUE_UNROLL = 4
MID_CHAINS = 1


def _cparams(sem):
    return pltpu.CompilerParams(dimension_semantics=sem, vmem_limit_bytes=VMEM_LIMIT_BYTES)


def _rms(x, g):
    return x * lax.rsqrt(jnp.mean(x * x, axis=-1, keepdims=True) + EPS) * g


def _row_copy(src_rows, dst_rows, sem):
    return pltpu.make_async_copy(src_rows, dst_rows, sem)


def _two_group_maps(n_prompt):
    def prompt_b(b):
        return jnp.minimum(b, n_prompt - 1)

    def sample_b(b):
        return jnp.maximum(b - n_prompt, 0)
    return prompt_b, sample_b


def _memkv_kernel(mp_ref, ms_ref, g_ref, w_ref, o_ref, *, n_prompt):
    b = pl.program_id(0)
    m = jnp.where(b < n_prompt, mp_ref[0], ms_ref[0])
    mn = _rms(m, g_ref[...]).astype(BF16)
    o_ref[0] = jnp.dot(mn, w_ref[...], preferred_element_type=F32).astype(BF16)


def _memkv(mem_p, mem_s, g_mem, w_kv):
    n_prompt, n_mem, d = mem_p.shape
    nb = n_prompt + mem_s.shape[0]
    pb, sb = _two_group_maps(n_prompt)
    return pl.pallas_call(
        functools.partial(_memkv_kernel, n_prompt=n_prompt),
        out_shape=jax.ShapeDtypeStruct((nb, n_mem, w_kv.shape[1]), BF16),
        grid=(nb,),
        in_specs=[
            pl.BlockSpec((1, n_mem, d), lambda b: (pb(b), 0, 0)),
            pl.BlockSpec((1, n_mem, d), lambda b: (sb(b), 0, 0)),
            pl.BlockSpec((1, d), lambda b: (0, 0)),
            pl.BlockSpec(w_kv.shape, lambda b: (0, 0)),
        ],
        out_specs=pl.BlockSpec((1, n_mem, w_kv.shape[1]), lambda b: (b, 0, 0)),
        compiler_params=_cparams(("arbitrary",)),
        name="memkv",
    )(mem_p, mem_s, g_mem.reshape(1, d), w_kv.astype(BF16))


_ROT_W = ATTN_WIDTH + 2 * KV_WIDTH


def _inproj_kernel(xp_ref, xs_ref, g_ref, w_ref, cos_ref, sin_ref, bd_ref,
                   q_ref, kv_ref, ab_ref, *, n_prompt):
    b = pl.program_id(1)
    x = jnp.where(b < n_prompt, xp_ref[0], xs_ref[0])
    h = _rms(x, g_ref[...]).astype(BF16)
    z = jnp.dot(h, w_ref[...], preferred_element_type=F32)
    cos = cos_ref[...]
    sin = sin_ref[...]
    lane = lax.broadcasted_iota(I32, cos.shape, 1) & (HEAD_DIM - 1)
    first_half = lane < ROPE_DIM // 2
    rot = []
    for c in range(_ROT_W // LANES):
        zc = z[:, c * LANES:(c + 1) * LANES]
        partner = jnp.where(first_half,
                            pltpu.roll(zc, LANES - ROPE_DIM // 2, axis=1),
                            pltpu.roll(zc, ROPE_DIM // 2, axis=1))
        rot.append(zc * cos + partner * sin)
    nq = ATTN_WIDTH // LANES
    q_ref[0] = (jnp.concatenate(rot[:nq], axis=1) * (LOG2E * HEAD_DIM ** -0.5)).astype(BF16)
    v_off = _ROT_W
    u_off = _ROT_W + 2 * KV_WIDTH
    kv_ref[0] = jnp.concatenate(rot[nq:] + [z[:, v_off:u_off]], axis=1).astype(BF16)
    half = FOURIER_WIDTH // 2
    u = z[:, u_off:].astype(BF16)
    r0 = jnp.dot(u[:, :half], bd_ref[...], preferred_element_type=F32)
    r1 = jnp.dot(u[:, half:], bd_ref[...], preferred_element_type=F32)
    ab_ref[0] = jnp.concatenate([r0[:, :half], r1[:, :half], r0[:, half:], r1[:, half:]],
                                axis=1).astype(BF16)


def _rope_tables(seq):
    half = ROPE_DIM // 2
    inv_freq = ROPE_THETA ** (-(jnp.arange(half, dtype=F32) * 2.0) / ROPE_DIM)
    ang = jnp.arange(seq).astype(F32)[:, None] * inv_freq[None, :]
    cos, sin = jnp.cos(ang), jnp.sin(ang)
    rest = HEAD_DIM - ROPE_DIM
    cos_h = jnp.concatenate([cos, cos, jnp.ones((seq, rest), F32)], axis=1)
    sin_h = jnp.concatenate([-sin, sin, jnp.zeros((seq, rest), F32)], axis=1)
    reps = LANES // HEAD_DIM
    return jnp.tile(cos_h, (1, reps)), jnp.tile(sin_h, (1, reps))


def _channel_dft_tables():
    n = FOURIER_GROUP
    idx = np.arange(n)
    ang = 2.0 * np.pi * ((idx[:, None] * idx[None, :]) % n) / n
    eye = np.eye(FOURIER_WIDTH // (2 * n))
    return jnp.asarray(np.concatenate([np.kron(eye, np.cos(ang)), np.kron(eye, np.sin(ang))], axis=1), BF16)


def _inproj(x_p, x_s, g_mix, w_in):
    n_prompt, seq, d = x_p.shape
    nb = n_prompt + x_s.shape[0]
    ts = min(TS_INPROJ, seq)
    o1, o2, o3 = ATTN_WIDTH, ATTN_WIDTH + KV_WIDTH, ATTN_WIDTH + 2 * KV_WIDTH
    swap = lambda w: jnp.concatenate([w[:, HEAD_DIM:], w[:, :HEAD_DIM]], axis=1)
    wk, wv = w_in[:, o1:o2], w_in[:, o2:o3]
    w_ext = jnp.concatenate([w_in[:, :o1], wk, swap(wk), wv, swap(wv), w_in[:, o3:]], axis=1).astype(BF16)
    cos_t, sin_t = _rope_tables(seq)
    bd = _channel_dft_tables()
    pb, sb = _two_group_maps(n_prompt)
    kvw = 4 * KV_WIDTH
    return pl.pallas_call(
        functools.partial(_inproj_kernel, n_prompt=n_prompt),
        out_shape=(jax.ShapeDtypeStruct((nb, seq, ATTN_WIDTH), BF16),
                   jax.ShapeDtypeStruct((nb, seq, kvw), BF16),
                   jax.ShapeDtypeStruct((nb, seq, 2 * FOURIER_WIDTH), BF16)),
        grid=(seq // ts, nb),
        in_specs=[
            pl.BlockSpec((1, ts, d), lambda i, b: (pb(b), i, 0)),
            pl.BlockSpec((1, ts, d), lambda i, b: (sb(b), i, 0)),
            pl.BlockSpec((1, d), lambda i, b: (0, 0)),
            pl.BlockSpec(w_ext.shape, lambda i, b: (0, 0)),
            pl.BlockSpec((ts, LANES), lambda i, b: (i, 0)),
            pl.BlockSpec((ts, LANES), lambda i, b: (i, 0)),
            pl.BlockSpec(bd.shape, lambda i, b: (0, 0)),
        ],
        out_specs=(pl.BlockSpec((1, ts, ATTN_WIDTH), lambda i, b: (b, i, 0)),
                   pl.BlockSpec((1, ts, kvw), lambda i, b: (b, i, 0)),
                   pl.BlockSpec((1, ts, 2 * FOURIER_WIDTH), lambda i, b: (b, i, 0))),
        compiler_params=_cparams(("arbitrary", "arbitrary")),
        name="inproj",
    )(x_p, x_s, g_mix.reshape(1, d), w_ext, cos_t, sin_t, bd)


def _swa_kernel(sink_ref, q_ref, kvm_ref, kvp_ref, kvn_ref, bias_ref, o_ref, *, nblk):
    i = pl.program_id(1)
    last = pl.num_programs(1) - 1
    kvw = jnp.concatenate([kvp_ref[0], kvm_ref[0], kvn_ref[0]], axis=0)
    k, ksw, v, vsw = [kvw[:, c * LANES:(c + 1) * LANES] for c in range(4)]
    lo_kv = lax.broadcasted_iota(I32, k.shape, 1) < HEAD_DIM
    k_dup = (jnp.where(lo_kv, k, ksw), jnp.where(lo_kv, ksw, k))
    v_dup = (jnp.where(lo_kv, v, vsw), jnp.where(lo_kv, vsw, v))
    lo_q = lax.broadcasted_iota(I32, (WINDOW, LANES), 1) < HEAD_DIM
    row = lax.broadcasted_iota(I32, (4 * WINDOW, 1), 0)
    for j in range(nblk):
        bias = bias_ref[1]
        if j == 0:
            bias = jnp.where(i == 0, bias_ref[0], bias)
        if j == nblk - 1:
            bias = jnp.where(i == last, bias_ref[2], bias)
        bias4 = jnp.concatenate([bias] * 4, axis=0)
        rows = slice(j * WINDOW, (j + 1) * WINDOW)
        win = slice(j * WINDOW, (j + 3) * WINDOW)
        for kvh in range(N_KV_HEADS):
            c0 = 2 * kvh
            qa = q_ref[0, rows, c0 * LANES:(c0 + 1) * LANES]
            qb = q_ref[0, rows, (c0 + 1) * LANES:(c0 + 2) * LANES]
            zero = jnp.zeros_like(qa)
            q4 = jnp.concatenate([jnp.where(lo_q, qa, zero), jnp.where(lo_q, zero, qa),
                                  jnp.where(lo_q, qb, zero), jnp.where(lo_q, zero, qb)], axis=0)
            s = lax.dot_general(q4, k_dup[kvh][win], (((1,), (1,)), ((), ())),
                                preferred_element_type=F32) + bias4
            h0 = 4 * kvh
            sink = jnp.where(row < WINDOW, sink_ref[h0],
                             jnp.where(row < 2 * WINDOW, sink_ref[h0 + 1],
                                       jnp.where(row < 3 * WINDOW, sink_ref[h0 + 2], sink_ref[h0 + 3])))
            m = jnp.maximum(jnp.max(s, axis=-1, keepdims=True), sink)
            e = jnp.exp2(s - m)
            den = jnp.sum(e, axis=-1, keepdims=True) + jnp.exp2(sink - m)
            o = jnp.dot(e.astype(BF16), v_dup[kvh][win], preferred_element_type=F32) * (1.0 / den)
            o_ref[0, rows, c0 * LANES:(c0 + 1) * LANES] = jnp.where(
                lo_q, o[:WINDOW], o[WINDOW:2 * WINDOW]).astype(o_ref.dtype)
            o_ref[0, rows, (c0 + 1) * LANES:(c0 + 2) * LANES] = jnp.where(
                lo_q, o[2 * WINDOW:3 * WINDOW], o[3 * WINDOW:]).astype(o_ref.dtype)


def _swa_bias():
    qi = np.arange(WINDOW)[:, None]
    c = np.arange(3 * WINDOW)[None, :]
    band = np.abs(c - WINDOW - qi) <= WINDOW
    first = band & (c >= WINDOW)
    lastb = band & (c < 2 * WINDOW)
    tab = np.stack([first, band, lastb]).astype(np.float32)
    return jnp.asarray((1.0 - tab) * NEG_INF, F32)


def _swa(q, kv, sinks):
    nb, seq, _ = q.shape
    tq = min(TQ_ATTN, seq)
    nblk = tq // WINDOW
    nseq_blk = seq // WINDOW
    kvw = kv.shape[-1]
    assert nblk >= 2 and seq % tq == 0, "first/last window masks are applied to distinct query blocks"
    bias = _swa_bias()
    return pl.pallas_call(
        functools.partial(_swa_kernel, nblk=nblk),
        out_shape=jax.ShapeDtypeStruct((nb, seq, ATTN_WIDTH), BF16),
        grid=(nb, seq // tq),
        in_specs=[
            pl.BlockSpec(memory_space=pltpu.SMEM),
            pl.BlockSpec((1, tq, ATTN_WIDTH), lambda b, i: (b, i, 0)),
            pl.BlockSpec((1, tq, kvw), lambda b, i: (b, i, 0)),
            pl.BlockSpec((1, WINDOW, kvw), lambda b, i: (b, jnp.maximum(i * nblk - 1, 0), 0)),
            pl.BlockSpec((1, WINDOW, kvw), lambda b, i: (b, jnp.minimum((i + 1) * nblk, nseq_blk - 1), 0)),
            pl.BlockSpec(bias.shape, lambda b, i: (0, 0, 0)),
        ],
        out_specs=pl.BlockSpec((1, tq, ATTN_WIDTH), lambda b, i: (b, i, 0)),
        compiler_params=_cparams(("arbitrary", "arbitrary")),
        name="swa",
    )(sinks.astype(F32) * LOG2E, q, kv, kv, kv, bias)


_DFT_PAD_ROWS = 16


def _seqdft_kernel(ab_ref, ct_hbm, st_hbm, perm_ref, o_ref, ct, st, aebo, hbuf, sem, *, scale, tk):
    n = ab_ref.shape[1]
    m = n // 2
    fw = FOURIER_WIDTH
    nblk = m // LANES

    @pl.when(pl.program_id(0) == 0)
    def _():
        copies = [pltpu.make_async_copy(ct_hbm, ct, sem.at[0]), pltpu.make_async_copy(st_hbm, st, sem.at[1])]
        for c in copies:
            c.start()
        for c in copies:
            c.wait()

    perm = perm_ref[...]
    for blk in range(nblk):
        mirrored = [ab_ref[0, n - LANES * (blk + 1):n - LANES * blk, :]]
        if blk > 0:
            mirrored.append(ab_ref[0, n - LANES * blk:n - LANES * (blk - 1), :])
        else:
            mirrored.append(jnp.zeros((LANES, 2 * fw), BF16))
        r = jnp.dot(perm, jnp.concatenate(mirrored, axis=0), preferred_element_type=F32)
        cur = ab_ref[0, LANES * blk:LANES * (blk + 1), :].astype(F32)
        aebo[LANES * blk:LANES * (blk + 1), :] = jnp.concatenate(
            [cur[:, :fw] + r[:, :fw], cur[:, fw:] - r[:, fw:]], axis=1).astype(BF16)

    a_mid = ab_ref[0, m:m + _DFT_PAD_ROWS, :fw][0:1, :].astype(F32)

    def pq(rows, nrows):
        p = jnp.dot(ct[rows, :], aebo[:, :fw], preferred_element_type=F32)
        q = jnp.dot(st[rows, :], aebo[:, fw:], preferred_element_type=F32)
        odd = (lax.broadcasted_iota(I32, (nrows, fw), 0) & 1) == 1
        return p + jnp.where(odd, -a_mid, a_mid), q

    for kt in range(m // tk):
        rows = slice(kt * tk, (kt + 1) * tk)
        p, q = pq(rows, tk)
        o_ref[0, rows, :] = ((p - q) * scale).astype(o_ref.dtype)
        hbuf[rows, :] = p + q
    p, q = pq(slice(m, m + _DFT_PAD_ROWS), _DFT_PAD_ROWS)
    hbuf[m:m + _DFT_PAD_ROWS, :] = p + q
    hbuf[m + _DFT_PAD_ROWS:, :] = jnp.zeros((LANES - _DFT_PAD_ROWS, fw), F32)
    for c in range(nblk):
        win = hbuf[LANES * c:LANES * (c + 2), :].astype(BF16)
        blk_out = jnp.dot(perm, win, preferred_element_type=F32)
        o_ref[0, n - LANES * (c + 1):n - LANES * c, :] = (blk_out * scale).astype(o_ref.dtype)


def _seq_dft_tables(seq):
    m = seq // 2
    k = np.arange(m + _DFT_PAD_ROWS)[:, None]
    ang = 2.0 * np.pi * ((k * np.arange(m)[None, :]) % seq) / seq
    return jnp.asarray(np.cos(ang), BF16), jnp.asarray(np.sin(ang), BF16)


def _seqdft(ab):
    nb, seq, _ = ab.shape
    m = seq // 2
    tk = min(TR_DFT, m)
    assert m % LANES == 0 and m % tk == 0 and m % 64 == 0
    ct, st = _seq_dft_tables(seq)
    perm = np.zeros((LANES, 2 * LANES), np.float32)
    perm[np.arange(LANES), LANES - np.arange(LANES)] = 1.0
    scale = float(1.0 / np.sqrt(seq * FOURIER_GROUP))
    return pl.pallas_call(
        functools.partial(_seqdft_kernel, scale=scale, tk=tk),
        out_shape=jax.ShapeDtypeStruct((nb, seq, FOURIER_WIDTH), BF16),
        grid=(nb,),
        in_specs=[
            pl.BlockSpec((1, seq, 2 * FOURIER_WIDTH), lambda b: (b, 0, 0)),
            pl.BlockSpec(memory_space=pl.ANY),
            pl.BlockSpec(memory_space=pl.ANY),
            pl.BlockSpec(perm.shape, lambda b: (0, 0)),
        ],
        out_specs=pl.BlockSpec((1, seq, FOURIER_WIDTH), lambda b: (b, 0, 0)),
        scratch_shapes=[pltpu.VMEM(ct.shape, BF16), pltpu.VMEM(st.shape, BF16),
                        pltpu.VMEM((m, 2 * FOURIER_WIDTH), BF16),
                        pltpu.VMEM((m + LANES, FOURIER_WIDTH), F32),
                        pltpu.SemaphoreType.DMA((2,))],
        compiler_params=_cparams(("arbitrary",)),
        name="seqdft",
    )(ab, ct, st, jnp.asarray(perm, BF16))


def _mid_kernel(xp_ref, xs_ref, a_ref, f_ref, kv_ref, gao_ref, gfo_ref, wout_ref, gc_ref, wq_ref,
                wo_ref, gm_ref, wr_ref, br_ref, tri_ref,
                x2_ref, meta_ref, gate_ref, cnt_ref, slots_ref,
                cnt_sc, stage, posv, pos_sm, cntv, cnt_sm, zbuf, rsem, psem, zsem, *, n_prompt, cap, te):
    b = pl.program_id(0)
    i = pl.program_id(1)
    tm = a_ref.shape[1]
    step = b * pl.num_programs(1) + i
    nsteps = pl.num_programs(0) * pl.num_programs(1)
    tile_rows = tm * SUBLANES
    dump_base = N_EXPERTS * cap

    def wait_scatter(sem_idx):
        for _ in range(TOP_K):
            _row_copy(stage.at[0], slots_ref.at[pl.ds(0, tile_rows), :], rsem.at[sem_idx]).wait()

    @pl.when(step == 0)
    def _():
        cnt_sc[...] = jnp.zeros_like(cnt_sc)
        stage[2] = jnp.zeros((tile_rows, LANES), F32)

        def init(t, carry):
            for k in range(TOP_K):
                pos_sm[1, k, t] = dump_base + k * tm + t
            return carry

        lax.fori_loop(0, tm, init, 0)

    @pl.when(step >= 2)
    def _():
        wait_scatter(step % 2)

    prev_stage = (step + 2) % 3
    prev_pos = (step + 1) % 2
    for t in range(tm):
        for k in range(TOP_K):
            r0 = pl.multiple_of(pos_sm[prev_pos, k, t] * SUBLANES, SUBLANES)
            _row_copy(stage.at[prev_stage, pl.ds(t * SUBLANES, SUBLANES), :],
                      slots_ref.at[pl.ds(r0, SUBLANES), :], rsem.at[step % 2]).start(priority=k % 2)
    cur_stage = step % 3

    def token_chain(r0, nr):
        rows = slice(r0, r0 + nr)
        x = jnp.where(b < n_prompt, xp_ref[0, rows, :], xs_ref[0, rows, :])
        an = _rms(a_ref[0, rows, :].astype(F32), gao_ref[...]).astype(BF16)
        fn = _rms(f_ref[0, rows, :].astype(F32), gfo_ref[...]).astype(BF16)
        x1 = (x + jnp.dot(an, wout_ref[:ATTN_WIDTH, :], preferred_element_type=F32)
              + jnp.dot(fn, wout_ref[ATTN_WIDTH:, :], preferred_element_type=F32))

        h2 = _rms(x1, gc_ref[...]).astype(BF16)
        qc = (jnp.dot(h2, wq_ref[...], preferred_element_type=F32)
              * (LOG2E * CROSS_HEAD_DIM ** -0.5)).astype(BF16)
        heads = []
        for hd in range(N_CROSS_HEADS):
            cols = slice(hd * CROSS_HEAD_DIM, (hd + 1) * CROSS_HEAD_DIM)
            vcols = slice(CROSS_WIDTH + hd * CROSS_HEAD_DIM, CROSS_WIDTH + (hd + 1) * CROSS_HEAD_DIM)
            s = lax.dot_general(qc[:, cols], kv_ref[0, :, cols], (((1,), (1,)), ((), ())),
                                preferred_element_type=F32)
            e = jnp.exp2(s - jnp.max(s, axis=-1, keepdims=True))
            den = jnp.sum(e, axis=-1, keepdims=True)
            heads.append(jnp.dot(e.astype(BF16), kv_ref[0, :, vcols], preferred_element_type=F32) * (1.0 / den))
        oc = jnp.concatenate(heads, axis=1).astype(BF16)
        x2 = x1 + jnp.dot(oc, wo_ref[...], preferred_element_type=F32)
        x2_ref[0, rows, :] = x2

        h3 = _rms(x2, gm_ref[...])
        for sl in range(SUBLANES):
            stage[cur_stage, pl.ds(r0 * SUBLANES + sl, nr, stride=SUBLANES), :] = h3[:, sl * LANES:(sl + 1) * LANES]

        hi = h3.astype(BF16)
        lo = (h3 - hi.astype(F32)).astype(BF16)
        hw = jnp.dot(hi, wr_ref[...], preferred_element_type=F32)
        return (hw[:, :LANES] + hw[:, LANES:]
                + jnp.dot(lo, wr_ref[:, :LANES], preferred_element_type=F32) + br_ref[...])

    nr = tm // MID_CHAINS
    logits = jnp.concatenate([token_chain(c * nr, nr) for c in range(MID_CHAINS)], axis=0)
    lt = logits.T[:N_EXPERTS, :]
    eidx = lax.broadcasted_iota(I32, lt.shape, 0)
    vals, idxs, sels = [], [], []
    for _ in range(TOP_K):
        m = jnp.max(lt, axis=0, keepdims=True)
        idx = jnp.min(jnp.where(lt == m, eidx, N_EXPERTS), axis=0, keepdims=True)
        sel = eidx == idx
        lt = jnp.where(sel, -jnp.inf, lt)
        vals.append(m)
        idxs.append(idx)
        sels.append(sel)
    ex = [jnp.exp(vk - vals[0]) for vk in vals]
    inv = 1.0 / (ex[0] + ex[1] + ex[2] + ex[3])
    zrow = jnp.zeros_like(ex[0])
    gate_ref[...] = jnp.concatenate([e_ * inv for e_ in ex] + [zrow] * (SUBLANES - TOP_K), axis=0)

    onehot = jnp.zeros(lt.shape, F32)
    for sel in sels:
        onehot = onehot + sel.astype(F32)
    before = cnt_sc[:, 0:1] + jnp.dot(onehot.astype(BF16), tri_ref[...], preferred_element_type=F32)
    ranks = [jnp.sum(jnp.where(sel, before, 0.0), axis=0, keepdims=True).astype(I32) for sel in sels]
    meta_ref[...] = jnp.concatenate(idxs + ranks, axis=0)
    cnt_sc[...] = cnt_sc[...] + jnp.sum(onehot, axis=1, keepdims=True)
    cnt_ref[...] = cnt_sc[...]

    posv[...] = jnp.concatenate([ix * cap + rk for ix, rk in zip(idxs, ranks)], axis=0)
    to_smem = pltpu.make_async_copy(posv, pos_sm.at[step % 2], psem)
    to_smem.start()
    to_smem.wait()

    @pl.when(step == nsteps - 1)
    def _():
        @pl.when(step >= 1)
        def _():
            wait_scatter((step + 1) % 2)

        def issue(tb, carry):
            for u in range(ISSUE_UNROLL):
                t = tb * ISSUE_UNROLL + u
                for k in range(TOP_K):
                    r0 = pl.multiple_of(pos_sm[step % 2, k, t] * SUBLANES, SUBLANES)
                    _row_copy(stage.at[cur_stage, pl.ds(pl.multiple_of(t * SUBLANES, SUBLANES), SUBLANES), :],
                              slots_ref.at[pl.ds(r0, SUBLANES), :], rsem.at[(step + 1) % 2]).start(priority=k % 2)
            return carry

        lax.fori_loop(0, tm // ISSUE_UNROLL, issue, 0)
        wait_scatter(step % 2)
        wait_scatter((step + 1) % 2)

        cntv[...] = cnt_sc[...].astype(I32)
        counts_to_smem = pltpu.make_async_copy(cntv, cnt_sm, psem)
        counts_to_smem.start()
        counts_to_smem.wait()
        zbuf[...] = jnp.zeros_like(zbuf)
        zrows = zbuf.shape[0] // SUBLANES
        bits = [1 << j for j in range(zrows.bit_length() - 1, -1, -1)]
        assert sum(bits) == 2 * zrows - 1 and zrows * 2 == te

        def fill(start):
            for e in range(N_EXPERTS):
                cnt = cnt_sm[e, 0]
                pad = (te - cnt % te) % te
                row = e * cap + cnt
                for bit in bits:
                    @pl.when((pad & bit) != 0)
                    def _(row=row, bit=bit):
                        cp = _row_copy(zbuf.at[pl.ds(0, bit * SUBLANES), :],
                                       slots_ref.at[pl.ds(pl.multiple_of(row * SUBLANES, SUBLANES),
                                                          bit * SUBLANES), :], zsem)
                        cp.start() if start else cp.wait()
                    row = row + (pad & bit)

        fill(True)
        fill(False)


def _mid(x_p, x_s, a, f, kvc, g_attn_out, g_four_out, w_out, g_cross, w_q, w_o, g_moe, w_router, b_router,
         cap, te):
    n_prompt, seq, d = x_p.shape
    nb = a.shape[0]
    tm = min(TM_MID, seq)
    nt = seq // tm
    n_mem = kvc.shape[1]
    pb, sb = _two_group_maps(n_prompt)
    wr = jnp.zeros((d, LANES), F32).at[:, :N_EXPERTS].set(w_router)
    wr_hi = wr.astype(BF16)
    wr_lo = (wr - wr_hi.astype(F32)).astype(BF16)
    br = jnp.zeros((1, LANES), F32).at[0, :N_EXPERTS].set(b_router)
    tri = jnp.asarray(np.triu(np.ones((tm, tm), np.float32), 1), BF16)
    full = lambda arr: pl.BlockSpec(arr.shape, lambda b, i: (0,) * arr.ndim)
    row = lambda v: v.reshape(1, -1)
    args = [x_p, x_s, a, f, kvc, row(g_attn_out), row(g_four_out), w_out.astype(BF16), row(g_cross),
            w_q.astype(BF16), w_o.astype(BF16), row(g_moe), jnp.concatenate([wr_hi, wr_lo], axis=1), br, tri]
    in_specs = [
        pl.BlockSpec((1, tm, d), lambda b, i: (pb(b), i, 0)),
        pl.BlockSpec((1, tm, d), lambda b, i: (sb(b), i, 0)),
        pl.BlockSpec((1, tm, ATTN_WIDTH), lambda b, i: (b, i, 0)),
        pl.BlockSpec((1, tm, FOURIER_WIDTH), lambda b, i: (b, i, 0)),
        pl.BlockSpec((1, n_mem, 2 * CROSS_WIDTH), lambda b, i: (b, 0, 0)),
    ] + [full(v) for v in args[5:]]
    tok = nb * seq
    slot_rows = N_EXPERTS * cap + TOP_K * tm
    return pl.pallas_call(
        functools.partial(_mid_kernel, n_prompt=n_prompt, cap=cap, te=te),
        out_shape=(jax.ShapeDtypeStruct((nb, seq, d), F32),
                   jax.ShapeDtypeStruct((SUBLANES, tok), I32),
                   jax.ShapeDtypeStruct((SUBLANES, tok), F32),
                   jax.ShapeDtypeStruct((N_EXPERTS, LANES), F32),
                   jax.ShapeDtypeStruct((slot_rows * SUBLANES, LANES), F32)),
        grid=(nb, nt),
        in_specs=in_specs,
        out_specs=(pl.BlockSpec((1, tm, d), lambda b, i: (b, i, 0)),
                   pl.BlockSpec((SUBLANES, tm), lambda b, i: (0, b * nt + i)),
                   pl.BlockSpec((SUBLANES, tm), lambda b, i: (0, b * nt + i)),
                   pl.BlockSpec((N_EXPERTS, LANES), lambda b, i: (0, 0)),
                   pl.BlockSpec(memory_space=pl.ANY)),
        scratch_shapes=[pltpu.VMEM((N_EXPERTS, LANES), F32),
                        pltpu.VMEM((3, tm * SUBLANES, LANES), F32),
                        pltpu.VMEM((TOP_K, tm), I32), pltpu.SMEM((2, TOP_K, tm), I32),
                        pltpu.VMEM((N_EXPERTS, LANES), I32), pltpu.SMEM((N_EXPERTS, LANES), I32),
                        pltpu.VMEM((te // 2 * SUBLANES, LANES), F32),
                        pltpu.SemaphoreType.DMA((2,)), pltpu.SemaphoreType.DMA, pltpu.SemaphoreType.DMA],
        compiler_params=_cparams(("arbitrary", "arbitrary")),
        name="mid",
    )(*args)


def _expert_kernel(te_ref, ti_ref, to_ref, nu_ref, xs_ref, wgu_ref, bgu_ref, wd_ref, bd_ref, ys_ref, wgu_bf, wd_bf):
    i = pl.program_id(0)
    te = xs_ref.shape[0] // SUBLANES
    dff = wd_ref.shape[1]

    @pl.when((i == 0) | (te_ref[i] != te_ref[jnp.maximum(i - 1, 0)]))
    def _():
        wgu_bf[...] = wgu_ref[0].astype(BF16)
        wd_bf[...] = wd_ref[0].astype(BF16)

    @pl.when(i < nu_ref[0])
    def _():
        x = jnp.concatenate([xs_ref[pl.ds(sl, te, stride=SUBLANES), :] for sl in range(SUBLANES)],
                            axis=1).astype(BF16)
        gu = jnp.dot(x, wgu_bf[...], preferred_element_type=F32) + bgu_ref[0]
        gate = jnp.minimum(gu[:, :dff], SWIGLU_LIMIT)
        up = jnp.clip(gu[:, dff:], -SWIGLU_LIMIT, SWIGLU_LIMIT)
        act = (up + 1.0) * (gate * (1.0 / (1.0 + jnp.exp(-SWIGLU_ALPHA * gate))))
        y = jnp.dot(act.astype(BF16), wd_bf[...], preferred_element_type=F32) + bd_ref[0]
        for sl in range(SUBLANES):
            ys_ref[pl.ds(sl, te, stride=SUBLANES), :] = y[:, sl * LANES:(sl + 1) * LANES]


def _experts(xs, tile_expert, tile_in, tile_out, n_used, w_gate_up, b_gate_up, w_down, b_down, te):
    ne, d, two_dff = w_gate_up.shape
    dff = two_dff // 2
    ntile = tile_expert.shape[0]
    return pl.pallas_call(
        _expert_kernel,
        out_shape=jax.ShapeDtypeStruct(((ntile * te + SEG_ROWS) * SUBLANES, LANES), F32),
        grid_spec=pltpu.PrefetchScalarGridSpec(
            num_scalar_prefetch=4,
            grid=(ntile,),
            in_specs=[
                pl.BlockSpec((te * SUBLANES, LANES), lambda i, te_r, ti_r, to_r, nu_r: (ti_r[i], 0)),
                pl.BlockSpec((1, d, two_dff), lambda i, te_r, ti_r, to_r, nu_r: (te_r[i], 0, 0)),
                pl.BlockSpec((1, 1, two_dff), lambda i, te_r, ti_r, to_r, nu_r: (te_r[i], 0, 0)),
                pl.BlockSpec((1, dff, d), lambda i, te_r, ti_r, to_r, nu_r: (te_r[i], 0, 0)),
                pl.BlockSpec((1, 1, d), lambda i, te_r, ti_r, to_r, nu_r: (te_r[i], 0, 0)),
            ],
            out_specs=pl.BlockSpec((te * SUBLANES, LANES), lambda i, te_r, ti_r, to_r, nu_r: (to_r[i], 0)),
            scratch_shapes=[pltpu.VMEM((d, two_dff), BF16), pltpu.VMEM((dff, d), BF16)],
        ),
        compiler_params=_cparams(("arbitrary",)),
        name="experts",
    )(tile_expert, tile_in, tile_out, n_used, xs, w_gate_up, b_gate_up.reshape(ne, 1, two_dff),
      w_down, b_down.reshape(ne, 1, d))


SEG_ROWS = 32


def _combine_kernel(seg_a, seg_b, seg_n, lidx_a, lidx_b, gate_a, gate_b, x2_ref, g_ref, ys_ref, o_ref,
                    buf_a, buf_b, obuf, sem, *, tm):
    step = pl.program_id(0) * pl.num_programs(1) + pl.program_id(1)
    nsteps = pl.num_programs(0) * pl.num_programs(1)
    seg_tile = SEG_ROWS * SUBLANES

    def segments(seg, buf, s, start):
        def per_segment(j, carry):
            src = pl.multiple_of(seg[0, 0, 1 + j] * SUBLANES, SUBLANES)
            dst = pl.multiple_of(j * seg_tile, seg_tile)
            cp = _row_copy(ys_ref.at[pl.ds(src, seg_tile), :], buf.at[pl.ds(dst, seg_tile), :], sem.at[s])
            if start:
                cp.start()
            else:
                cp.wait()
            return carry

        lax.fori_loop(0, seg[0, 0, 0], per_segment, 0)

    def sum_tile(rows, buf, lidx_ref, gate_ref):
        for t in range(tm):
            acc = None
            for k in range(TOP_K):
                p = pl.multiple_of(lidx_ref[0, 0, k * tm + t] * SUBLANES, SUBLANES)
                term = gate_ref[0, 0, k * tm + t] * buf[pl.ds(p, SUBLANES), :]
                acc = term if acc is None else acc + term
            obuf[pl.ds(t * SUBLANES, SUBLANES), :] = acc
        y = jnp.concatenate([obuf[pl.ds(sl, tm, stride=SUBLANES), :] for sl in range(SUBLANES)], axis=1)
        o_ref[0, rows, :] = _rms(x2_ref[0, rows, :] + y, g_ref[...])

    @pl.when(step == 0)
    def _():
        segments(seg_a, buf_a, 0, True)

    segments(seg_b, buf_b, 1, True)
    segments(seg_a, buf_a, 0, False)
    sum_tile(slice(0, tm), buf_a, lidx_a, gate_a)

    @pl.when(step + 1 < nsteps)
    def _():
        segments(seg_n, buf_a, 0, True)

    segments(seg_b, buf_b, 1, False)
    sum_tile(slice(tm, 2 * tm), buf_b, lidx_b, gate_b)


def _combine(x2, seg, lidx3, gate3, ys, g_final, b0, nbatch, tm):
    _, seq, d = x2.shape
    nt = seq // (2 * tm)
    assert seq % (2 * tm) == 0
    last = 2 * (b0 + nbatch) * nt - 1
    tile_a = lambda b, i: 2 * ((b0 + b) * nt + i)
    tile_b = lambda b, i: tile_a(b, i) + 1
    buf_rows = -(-(TOP_K * tm + N_EXPERTS * (SEG_ROWS - 1)) // SEG_ROWS) * SEG_ROWS
    smem = lambda width, fn: pl.BlockSpec((1, 1, width), lambda b, i: (fn(b, i), 0, 0), memory_space=pltpu.SMEM)
    return pl.pallas_call(
        functools.partial(_combine_kernel, tm=tm),
        out_shape=jax.ShapeDtypeStruct((nbatch, seq, d), F32),
        grid_spec=pltpu.PrefetchScalarGridSpec(
            num_scalar_prefetch=0,
            grid=(nbatch, nt),
            in_specs=[
                smem(seg.shape[2], tile_a),
                smem(seg.shape[2], tile_b),
                smem(seg.shape[2], lambda b, i: jnp.minimum(tile_a(b, i) + 2, last)),
                smem(TOP_K * tm, tile_a),
                smem(TOP_K * tm, tile_b),
                smem(TOP_K * tm, tile_a),
                smem(TOP_K * tm, tile_b),
                pl.BlockSpec((1, 2 * tm, d), lambda b, i: (b0 + b, i, 0)),
                pl.BlockSpec((1, d), lambda b, i: (0, 0)),
                pl.BlockSpec(memory_space=pl.ANY),
            ],
            out_specs=pl.BlockSpec((1, 2 * tm, d), lambda b, i: (b, i, 0)),
            scratch_shapes=[pltpu.VMEM((buf_rows * SUBLANES, LANES), F32),
                            pltpu.VMEM((buf_rows * SUBLANES, LANES), F32),
                            pltpu.VMEM((tm * SUBLANES, LANES), F32),
                            pltpu.SemaphoreType.DMA((2,))],
        ),
        compiler_params=_cparams(("arbitrary", "arbitrary")),
        name="combine",
    )(seg, seg, seg, lidx3, lidx3, gate3, gate3, x2, g_final.reshape(1, d), ys)


def _routing_plan(meta, gates, cnt, te, tm, cap):
    tok = meta.shape[1]
    idx, rank = meta[:TOP_K], meta[TOP_K:]
    counts = cnt[:, 0].astype(I32)
    tiles = (counts + te - 1) // te
    tile_end = jnp.cumsum(tiles)
    tile_start = tile_end - tiles
    offsets = tile_start * te
    n_used = tile_end[-1]
    ntile_max = (tok * TOP_K) // te + N_EXPERTS
    t_ids = jnp.minimum(jnp.arange(ntile_max, dtype=I32), n_used - 1)
    tile_expert = jnp.sum((t_ids[:, None] >= tile_end[None, :]).astype(I32), axis=1)
    first = jnp.sum(jnp.where(tile_expert[:, None] == jnp.arange(N_EXPERTS, dtype=I32)[None, :],
                              tile_start[None, :], 0), axis=1)
    tile_in = tile_expert * (cap // te) + (t_ids - first)

    ng = tok // tm
    onehot = idx.reshape(TOP_K, ng, tm)[..., None] == jnp.arange(N_EXPERTS, dtype=I32)
    n_ge = jnp.sum(onehot.astype(I32), axis=(0, 2))
    before = jnp.cumsum(n_ge, axis=0) - n_ge
    nseg = (n_ge + SEG_ROWS - 1) // SEG_ROWS
    seg_end = jnp.cumsum(nseg, axis=1)
    dst = (seg_end - nseg) * SEG_ROWS
    max_seg = (TOP_K * tm) // SEG_ROWS + N_EXPERTS
    assert max_seg < LANES
    j = jnp.arange(max_seg, dtype=I32)[None, :, None]
    owner = j >= seg_end[:, None, :]
    e_j = jnp.minimum(jnp.sum(owner.astype(I32), axis=-1), N_EXPERTS - 1)
    pick = e_j[..., None] == jnp.arange(N_EXPERTS, dtype=I32)
    first_row = (offsets[None, :] + before - dst)[:, None, :]
    src = jnp.sum(jnp.where(pick, first_row, 0), axis=-1) + j[..., 0] * SEG_ROWS
    seg = jnp.concatenate([seg_end[:, -1:], src, jnp.zeros((ng, LANES - 1 - max_seg), I32)],
                          axis=1).reshape(ng, 1, LANES)
    shift = (dst - before)[None, :, None, :]
    lidx = rank.reshape(TOP_K, ng, tm) + jnp.sum(jnp.where(onehot, shift, 0), axis=-1)
    by_tile = lambda v: v.reshape(TOP_K, ng, tm).transpose(1, 0, 2).reshape(ng, 1, TOP_K * tm)
    return tile_expert, tile_in, t_ids, n_used.reshape(1), seg, by_tile(lidx), by_tile(gates[:TOP_K])


def kernel(x_prompt, x_sample, mem_prompt, mem_sample, g_mix, w_in, sinks, g_attn_out, g_four_out, w_out,
           g_cross, g_mem, w_q_cross, w_kv_cross, w_o_cross, g_moe, w_router, b_router, w_gate_up,
           b_gate_up, w_down, b_down, g_final):
    n_prompt, seq, _ = x_prompt.shape
    n_sample = x_sample.shape[0]
    l = 0
    kvc = _memkv(mem_prompt, mem_sample, g_mem[l], w_kv_cross[l])
    q, kv, ab = _inproj(x_prompt, x_sample, g_mix[l], w_in[l])
    a = _swa(q, kv, sinks[l])
    f = _seqdft(ab)
    tok = (n_prompt + n_sample) * seq
    te = TM_EXPERT
    cap = -(-tok // te) * te
    x2, meta, gates, cnt, xs = _mid(x_prompt, x_sample, a, f, kvc, g_attn_out[l], g_four_out[l], w_out[l],
                                    g_cross[l], w_q_cross[l], w_o_cross[l], g_moe[l], w_router[l], b_router[l],
                                    cap, te)
    tm_rows = min(TM_ROWS, seq // 2)
    tile_expert, tile_in, tile_out, n_used, seg, lidx3, gate3 = _routing_plan(meta, gates, cnt, te, tm_rows, cap)
    ys = _experts(xs, tile_expert, tile_in, tile_out, n_used, w_gate_up[l], b_gate_up[l], w_down[l], b_down[l], te)
    y_p = _combine(x2, seg, lidx3, gate3, ys, g_final, 0, n_prompt, tm_rows)
    y_s = _combine(x2, seg, lidx3, gate3, ys, g_final, n_prompt, n_sample, tm_rows)
    return (y_p, y_s)
```

```python
import functools

import numpy as np
import jax
import jax.numpy as jnp
from jax import lax
from jax.experimental import pallas as pl
from jax.experimental.pallas import tpu as pltpu

F32 = jnp.float32
BF16 = jnp.bfloat16
I32 = jnp.int32

HEAD_DIM = 64
N_Q_HEADS = 8
N_KV_HEADS = 2
ATTN_WIDTH = N_Q_HEADS * HEAD_DIM
KV_WIDTH = N_KV_HEADS * HEAD_DIM
FOURIER_WIDTH = 512
FOURIER_GROUP = 64
WINDOW = 128
ROPE_DIM = 16
ROPE_THETA = 500000.0
N_CROSS_HEADS = 4
CROSS_HEAD_DIM = 128
CROSS_WIDTH = N_CROSS_HEADS * CROSS_HEAD_DIM
N_EXPERTS = 32
TOP_K = 4
SWIGLU_LIMIT = 7.0
SWIGLU_ALPHA = 1.702
EPS = 1e-5
NEG_INF = -1e30
LOG2E = 1.4426950408889634

LANES = 128
SUBLANES = 8
VMEM_LIMIT_BYTES = 56 * 1024 * 1024

TS_INPROJ = 1024
TQ_ATTN = 1024
TR_DFT = 512
TM_MID = 512
TM_ROWS = 512
TM_EXPERT = 512
ISSUE_UNROLL = 4
MID_CHAINS = 1


def _cparams(sem):
    return pltpu.CompilerParams(dimension_semantics=sem, vmem_limit_bytes=VMEM_LIMIT_BYTES)


def _rms(x, g):
    return x * lax.rsqrt(jnp.mean(x * x, axis=-1, keepdims=True) + EPS) * g


def _row_copy(src_rows, dst_rows, sem):
    return pltpu.make_async_copy(src_rows, dst_rows, sem)


def _two_group_maps(n_prompt):
    def prompt_b(b):
        return jnp.minimum(b, n_prompt - 1)

    def sample_b(b):
        return jnp.maximum(b - n_prompt, 0)
    return prompt_b, sample_b


def _memkv_kernel(mp_ref, ms_ref, g_ref, w_ref, o_ref, *, n_prompt):
    b = pl.program_id(0)
    m = jnp.where(b < n_prompt, mp_ref[0], ms_ref[0])
    mn = _rms(m, g_ref[...]).astype(BF16)
    o_ref[0] = jnp.dot(mn, w_ref[...], preferred_element_type=F32).astype(BF16)


def _memkv(mem_p, mem_s, g_mem, w_kv):
    n_prompt, n_mem, d = mem_p.shape
    nb = n_prompt + mem_s.shape[0]
    pb, sb = _two_group_maps(n_prompt)
    return pl.pallas_call(
        functools.partial(_memkv_kernel, n_prompt=n_prompt),
        out_shape=jax.ShapeDtypeStruct((nb, n_mem, w_kv.shape[1]), BF16),
        grid=(nb,),
        in_specs=[
            pl.BlockSpec((1, n_mem, d), lambda b: (pb(b), 0, 0)),
            pl.BlockSpec((1, n_mem, d), lambda b: (sb(b), 0, 0)),
            pl.BlockSpec((1, d), lambda b: (0, 0)),
            pl.BlockSpec(w_kv.shape, lambda b: (0, 0)),
        ],
        out_specs=pl.BlockSpec((1, n_mem, w_kv.shape[1]), lambda b: (b, 0, 0)),
        compiler_params=_cparams(("arbitrary",)),
        name="memkv",
    )(mem_p, mem_s, g_mem.reshape(1, d), w_kv.astype(BF16))


_ROT_W = ATTN_WIDTH + 2 * KV_WIDTH


def _inproj_kernel(xp_ref, xs_ref, g_ref, w_ref, cos_ref, sin_ref, bd_ref,
                   q_ref, kv_ref, ab_ref, *, n_prompt):
    b = pl.program_id(1)
    x = jnp.where(b < n_prompt, xp_ref[0], xs_ref[0])
    h = _rms(x, g_ref[...]).astype(BF16)
    z = jnp.dot(h, w_ref[...], preferred_element_type=F32)
    cos = cos_ref[...]
    sin = sin_ref[...]
    lane = lax.broadcasted_iota(I32, cos.shape, 1) & (HEAD_DIM - 1)
    first_half = lane < ROPE_DIM // 2
    rot = []
    for c in range(_ROT_W // LANES):
        zc = z[:, c * LANES:(c + 1) * LANES]
        partner = jnp.where(first_half,
                            pltpu.roll(zc, LANES - ROPE_DIM // 2, axis=1),
                            pltpu.roll(zc, ROPE_DIM // 2, axis=1))
        rot.append(zc * cos + partner * sin)
    nq = ATTN_WIDTH // LANES
    q_ref[0] = (jnp.concatenate(rot[:nq], axis=1) * (LOG2E * HEAD_DIM ** -0.5)).astype(BF16)
    v_off = _ROT_W
    u_off = _ROT_W + 2 * KV_WIDTH
    kv_ref[0] = jnp.concatenate(rot[nq:] + [z[:, v_off:u_off]], axis=1).astype(BF16)
    half = FOURIER_WIDTH // 2
    u = z[:, u_off:].astype(BF16)
    r0 = jnp.dot(u[:, :half], bd_ref[...], preferred_element_type=F32)
    r1 = jnp.dot(u[:, half:], bd_ref[...], preferred_element_type=F32)
    ab_ref[0] = jnp.concatenate([r0[:, :half], r1[:, :half], r0[:, half:], r1[:, half:]],
                                axis=1).astype(BF16)


def _rope_tables(seq):
    half = ROPE_DIM // 2
    inv_freq = ROPE_THETA ** (-(jnp.arange(half, dtype=F32) * 2.0) / ROPE_DIM)
    ang = jnp.arange(seq).astype(F32)[:, None] * inv_freq[None, :]
    cos, sin = jnp.cos(ang), jnp.sin(ang)
    rest = HEAD_DIM - ROPE_DIM
    cos_h = jnp.concatenate([cos, cos, jnp.ones((seq, rest), F32)], axis=1)
    sin_h = jnp.concatenate([-sin, sin, jnp.zeros((seq, rest), F32)], axis=1)
    reps = LANES // HEAD_DIM
    return jnp.tile(cos_h, (1, reps)), jnp.tile(sin_h, (1, reps))


def _channel_dft_tables():
    n = FOURIER_GROUP
    idx = np.arange(n)
    ang = 2.0 * np.pi * ((idx[:, None] * idx[None, :]) % n) / n
    eye = np.eye(FOURIER_WIDTH // (2 * n))
    return jnp.asarray(np.concatenate([np.kron(eye, np.cos(ang)), np.kron(eye, np.sin(ang))], axis=1), BF16)


def _inproj(x_p, x_s, g_mix, w_in):
    n_prompt, seq, d = x_p.shape
    nb = n_prompt + x_s.shape[0]
    ts = min(TS_INPROJ, seq)
    o1, o2, o3 = ATTN_WIDTH, ATTN_WIDTH + KV_WIDTH, ATTN_WIDTH + 2 * KV_WIDTH
    swap = lambda w: jnp.concatenate([w[:, HEAD_DIM:], w[:, :HEAD_DIM]], axis=1)
    wk, wv = w_in[:, o1:o2], w_in[:, o2:o3]
    w_ext = jnp.concatenate([w_in[:, :o1], wk, swap(wk), wv, swap(wv), w_in[:, o3:]], axis=1).astype(BF16)
    cos_t, sin_t = _rope_tables(seq)
    bd = _channel_dft_tables()
    pb, sb = _two_group_maps(n_prompt)
    kvw = 4 * KV_WIDTH
    return pl.pallas_call(
        functools.partial(_inproj_kernel, n_prompt=n_prompt),
        out_shape=(jax.ShapeDtypeStruct((nb, seq, ATTN_WIDTH), BF16),
                   jax.ShapeDtypeStruct((nb, seq, kvw), BF16),
                   jax.ShapeDtypeStruct((nb, seq, 2 * FOURIER_WIDTH), BF16)),
        grid=(seq // ts, nb),
        in_specs=[
            pl.BlockSpec((1, ts, d), lambda i, b: (pb(b), i, 0)),
            pl.BlockSpec((1, ts, d), lambda i, b: (sb(b), i, 0)),
            pl.BlockSpec((1, d), lambda i, b: (0, 0)),
            pl.BlockSpec(w_ext.shape, lambda i, b: (0, 0)),
            pl.BlockSpec((ts, LANES), lambda i, b: (i, 0)),
            pl.BlockSpec((ts, LANES), lambda i, b: (i, 0)),
            pl.BlockSpec(bd.shape, lambda i, b: (0, 0)),
        ],
        out_specs=(pl.BlockSpec((1, ts, ATTN_WIDTH), lambda i, b: (b, i, 0)),
                   pl.BlockSpec((1, ts, kvw), lambda i, b: (b, i, 0)),
                   pl.BlockSpec((1, ts, 2 * FOURIER_WIDTH), lambda i, b: (b, i, 0))),
        compiler_params=_cparams(("arbitrary", "arbitrary")),
        name="inproj",
    )(x_p, x_s, g_mix.reshape(1, d), w_ext, cos_t, sin_t, bd)


def _swa_kernel(sink_ref, q_ref, kvm_ref, kvp_ref, kvn_ref, bias_ref, o_ref, *, nblk):
    i = pl.program_id(1)
    last = pl.num_programs(1) - 1
    kvw = jnp.concatenate([kvp_ref[0], kvm_ref[0], kvn_ref[0]], axis=0)
    k, ksw, v, vsw = [kvw[:, c * LANES:(c + 1) * LANES] for c in range(4)]
    lo_kv = lax.broadcasted_iota(I32, k.shape, 1) < HEAD_DIM
    k_dup = (jnp.where(lo_kv, k, ksw), jnp.where(lo_kv, ksw, k))
    v_dup = (jnp.where(lo_kv, v, vsw), jnp.where(lo_kv, vsw, v))
    lo_q = lax.broadcasted_iota(I32, (WINDOW, LANES), 1) < HEAD_DIM
    row = lax.broadcasted_iota(I32, (4 * WINDOW, 1), 0)
    for j in range(nblk):
        bias = bias_ref[1]
        if j == 0:
            bias = jnp.where(i == 0, bias_ref[0], bias)
        if j == nblk - 1:
            bias = jnp.where(i == last, bias_ref[2], bias)
        bias4 = jnp.concatenate([bias] * 4, axis=0)
        rows = slice(j * WINDOW, (j + 1) * WINDOW)
        win = slice(j * WINDOW, (j + 3) * WINDOW)
        for kvh in range(N_KV_HEADS):
            c0 = 2 * kvh
            qa = q_ref[0, rows, c0 * LANES:(c0 + 1) * LANES]
            qb = q_ref[0, rows, (c0 + 1) * LANES:(c0 + 2) * LANES]
            zero = jnp.zeros_like(qa)
            q4 = jnp.concatenate([jnp.where(lo_q, qa, zero), jnp.where(lo_q, zero, qa),
                                  jnp.where(lo_q, qb, zero), jnp.where(lo_q, zero, qb)], axis=0)
            s = lax.dot_general(q4, k_dup[kvh][win], (((1,), (1,)), ((), ())),
                                preferred_element_type=F32) + bias4
            h0 = 4 * kvh
            sink = jnp.where(row < WINDOW, sink_ref[h0],
                             jnp.where(row < 2 * WINDOW, sink_ref[h0 + 1],
                                       jnp.where(row < 3 * WINDOW, sink_ref[h0 + 2], sink_ref[h0 + 3])))
            m = jnp.maximum(jnp.max(s, axis=-1, keepdims=True), sink)
            e = jnp.exp2(s - m)
            den = jnp.sum(e, axis=-1, keepdims=True) + jnp.exp2(sink - m)
            o = jnp.dot(e.astype(BF16), v_dup[kvh][win], preferred_element_type=F32) * (1.0 / den)
            o_ref[0, rows, c0 * LANES:(c0 + 1) * LANES] = jnp.where(
                lo_q, o[:WINDOW], o[WINDOW:2 * WINDOW]).astype(o_ref.dtype)
            o_ref[0, rows, (c0 + 1) * LANES:(c0 + 2) * LANES] = jnp.where(
                lo_q, o[2 * WINDOW:3 * WINDOW], o[3 * WINDOW:]).astype(o_ref.dtype)


def _swa_bias():
    qi = np.arange(WINDOW)[:, None]
    c = np.arange(3 * WINDOW)[None, :]
    band = np.abs(c - WINDOW - qi) <= WINDOW
    first = band & (c >= WINDOW)
    lastb = band & (c < 2 * WINDOW)
    tab = np.stack([first, band, lastb]).astype(np.float32)
    return jnp.asarray((1.0 - tab) * NEG_INF, F32)


def _swa(q, kv, sinks):
    nb, seq, _ = q.shape
    tq = min(TQ_ATTN, seq)
    nblk = tq // WINDOW
    nseq_blk = seq // WINDOW
    kvw = kv.shape[-1]
    assert nblk >= 2 and seq % tq == 0, "first/last window masks are applied to distinct query blocks"
    bias = _swa_bias()
    return pl.pallas_call(
        functools.partial(_swa_kernel, nblk=nblk),
        out_shape=jax.ShapeDtypeStruct((nb, seq, ATTN_WIDTH), BF16),
        grid=(nb, seq // tq),
        in_specs=[
            pl.BlockSpec(memory_space=pltpu.SMEM),
            pl.BlockSpec((1, tq, ATTN_WIDTH), lambda b, i: (b, i, 0)),
            pl.BlockSpec((1, tq, kvw), lambda b, i: (b, i, 0)),
            pl.BlockSpec((1, WINDOW, kvw), lambda b, i: (b, jnp.maximum(i * nblk - 1, 0), 0)),
            pl.BlockSpec((1, WINDOW, kvw), lambda b, i: (b, jnp.minimum((i + 1) * nblk, nseq_blk - 1), 0)),
            pl.BlockSpec(bias.shape, lambda b, i: (0, 0, 0)),
        ],
        out_specs=pl.BlockSpec((1, tq, ATTN_WIDTH), lambda b, i: (b, i, 0)),
        compiler_params=_cparams(("arbitrary", "arbitrary")),
        name="swa",
    )(sinks.astype(F32) * LOG2E, q, kv, kv, kv, bias)


_DFT_PAD_ROWS = 16


def _seqdft_kernel(ab_ref, ct_hbm, st_hbm, perm_ref, o_ref, ct, st, aebo, hbuf, sem, *, scale, tk):
    n = ab_ref.shape[1]
    m = n // 2
    fw = FOURIER_WIDTH
    nblk = m // LANES

    @pl.when(pl.program_id(0) == 0)
    def _():
        copies = [pltpu.make_async_copy(ct_hbm, ct, sem.at[0]), pltpu.make_async_copy(st_hbm, st, sem.at[1])]
        for c in copies:
            c.start()
        for c in copies:
            c.wait()

    perm = perm_ref[...]
    for blk in range(nblk):
        mirrored = [ab_ref[0, n - LANES * (blk + 1):n - LANES * blk, :]]
        if blk > 0:
            mirrored.append(ab_ref[0, n - LANES * blk:n - LANES * (blk - 1), :])
        else:
            mirrored.append(jnp.zeros((LANES, 2 * fw), BF16))
        r = jnp.dot(perm, jnp.concatenate(mirrored, axis=0), preferred_element_type=F32)
        cur = ab_ref[0, LANES * blk:LANES * (blk + 1), :].astype(F32)
        aebo[LANES * blk:LANES * (blk + 1), :] = jnp.concatenate(
            [cur[:, :fw] + r[:, :fw], cur[:, fw:] - r[:, fw:]], axis=1).astype(BF16)

    a_mid = ab_ref[0, m:m + _DFT_PAD_ROWS, :fw][0:1, :].astype(F32)

    def pq(rows, nrows):
        p = jnp.dot(ct[rows, :], aebo[:, :fw], preferred_element_type=F32)
        q = jnp.dot(st[rows, :], aebo[:, fw:], preferred_element_type=F32)
        odd = (lax.broadcasted_iota(I32, (nrows, fw), 0) & 1) == 1
        return p + jnp.where(odd, -a_mid, a_mid), q

    for kt in range(m // tk):
        rows = slice(kt * tk, (kt + 1) * tk)
        p, q = pq(rows, tk)
        o_ref[0, rows, :] = ((p - q) * scale).astype(o_ref.dtype)
        hbuf[rows, :] = p + q
    p, q = pq(slice(m, m + _DFT_PAD_ROWS), _DFT_PAD_ROWS)
    hbuf[m:m + _DFT_PAD_ROWS, :] = p + q
    hbuf[m + _DFT_PAD_ROWS:, :] = jnp.zeros((LANES - _DFT_PAD_ROWS, fw), F32)
    for c in range(nblk):
        win = hbuf[LANES * c:LANES * (c + 2), :].astype(BF16)
        blk_out = jnp.dot(perm, win, preferred_element_type=F32)
        o_ref[0, n - LANES * (c + 1):n - LANES * c, :] = (blk_out * scale).astype(o_ref.dtype)


def _seq_dft_tables(seq):
    m = seq // 2
    k = np.arange(m + _DFT_PAD_ROWS)[:, None]
    ang = 2.0 * np.pi * ((k * np.arange(m)[None, :]) % seq) / seq
    return jnp.asarray(np.cos(ang), BF16), jnp.asarray(np.sin(ang), BF16)


def _seqdft(ab):
    nb, seq, _ = ab.shape
    m = seq // 2
    tk = min(TR_DFT, m)
    assert m % LANES == 0 and m % tk == 0 and m % 64 == 0
    ct, st = _seq_dft_tables(seq)
    perm = np.zeros((LANES, 2 * LANES), np.float32)
    perm[np.arange(LANES), LANES - np.arange(LANES)] = 1.0
    scale = float(1.0 / np.sqrt(seq * FOURIER_GROUP))
    return pl.pallas_call(
        functools.partial(_seqdft_kernel, scale=scale, tk=tk),
        out_shape=jax.ShapeDtypeStruct((nb, seq, FOURIER_WIDTH), BF16),
        grid=(nb,),
        in_specs=[
            pl.BlockSpec((1, seq, 2 * FOURIER_WIDTH), lambda b: (b, 0, 0)),
            pl.BlockSpec(memory_space=pl.ANY),
            pl.BlockSpec(memory_space=pl.ANY),
            pl.BlockSpec(perm.shape, lambda b: (0, 0)),
        ],
        out_specs=pl.BlockSpec((1, seq, FOURIER_WIDTH), lambda b: (b, 0, 0)),
        scratch_shapes=[pltpu.VMEM(ct.shape, BF16), pltpu.VMEM(st.shape, BF16),
                        pltpu.VMEM((m, 2 * FOURIER_WIDTH), BF16),
                        pltpu.VMEM((m + LANES, FOURIER_WIDTH), F32),
                        pltpu.SemaphoreType.DMA((2,))],
        compiler_params=_cparams(("arbitrary",)),
        name="seqdft",
    )(ab, ct, st, jnp.asarray(perm, BF16))


def _mid_kernel(xp_ref, xs_ref, a_ref, f_ref, kv_ref, gao_ref, gfo_ref, wout_ref, gc_ref, wq_ref,
                wo_ref, gm_ref, wr_ref, br_ref, tri_ref,
                x2_ref, meta_ref, gate_ref, cnt_ref, slots_ref,
                cnt_sc, stage, posv, pos_sm, cntv, cnt_sm, zbuf, rsem, psem, zsem, *, n_prompt, cap, te):
    b = pl.program_id(0)
    i = pl.program_id(1)
    tm = a_ref.shape[1]
    step = b * pl.num_programs(1) + i
    nsteps = pl.num_programs(0) * pl.num_programs(1)
    tile_rows = tm * SUBLANES
    dump_base = N_EXPERTS * cap

    def wait_scatter(sem_idx):
        for _ in range(TOP_K):
            _row_copy(stage.at[0], slots_ref.at[pl.ds(0, tile_rows), :], rsem.at[sem_idx]).wait()

    @pl.when(step == 0)
    def _():
        cnt_sc[...] = jnp.zeros_like(cnt_sc)
        stage[2] = jnp.zeros((tile_rows, LANES), F32)

        def init(t, carry):
            for k in range(TOP_K):
                pos_sm[1, k, t] = dump_base + k * tm + t
            return carry

        lax.fori_loop(0, tm, init, 0)

    @pl.when(step >= 2)
    def _():
        wait_scatter(step % 2)

    prev_stage = (step + 2) % 3
    prev_pos = (step + 1) % 2
    for t in range(tm):
        for k in range(TOP_K):
            r0 = pl.multiple_of(pos_sm[prev_pos, k, t] * SUBLANES, SUBLANES)
            _row_copy(stage.at[prev_stage, pl.ds(t * SUBLANES, SUBLANES), :],
                      slots_ref.at[pl.ds(r0, SUBLANES), :], rsem.at[step % 2]).start(priority=k % 2)
    cur_stage = step % 3

    def token_chain(r0, nr):
        rows = slice(r0, r0 + nr)
        x = jnp.where(b < n_prompt, xp_ref[0, rows, :], xs_ref[0, rows, :])
        an = _rms(a_ref[0, rows, :].astype(F32), gao_ref[...]).astype(BF16)
        fn = _rms(f_ref[0, rows, :].astype(F32), gfo_ref[...]).astype(BF16)
        x1 = (x + jnp.dot(an, wout_ref[:ATTN_WIDTH, :], preferred_element_type=F32)
              + jnp.dot(fn, wout_ref[ATTN_WIDTH:, :], preferred_element_type=F32))

        h2 = _rms(x1, gc_ref[...]).astype(BF16)
        qc = (jnp.dot(h2, wq_ref[...], preferred_element_type=F32)
              * (LOG2E * CROSS_HEAD_DIM ** -0.5)).astype(BF16)
        heads = []
        for hd in range(N_CROSS_HEADS):
            cols = slice(hd * CROSS_HEAD_DIM, (hd + 1) * CROSS_HEAD_DIM)
            vcols = slice(CROSS_WIDTH + hd * CROSS_HEAD_DIM, CROSS_WIDTH + (hd + 1) * CROSS_HEAD_DIM)
            s = lax.dot_general(qc[:, cols], kv_ref[0, :, cols], (((1,), (1,)), ((), ())),
                                preferred_element_type=F32)
            e = jnp.exp2(s - jnp.max(s, axis=-1, keepdims=True))
            den = jnp.sum(e, axis=-1, keepdims=True)
            heads.append(jnp.dot(e.astype(BF16), kv_ref[0, :, vcols], preferred_element_type=F32) * (1.0 / den))
        oc = jnp.concatenate(heads, axis=1).astype(BF16)
        x2 = x1 + jnp.dot(oc, wo_ref[...], preferred_element_type=F32)
        x2_ref[0, rows, :] = x2

        h3 = _rms(x2, gm_ref[...])
        for sl in range(SUBLANES):
            stage[cur_stage, pl.ds(r0 * SUBLANES + sl, nr, stride=SUBLANES), :] = h3[:, sl * LANES:(sl + 1) * LANES]

        hi = h3.astype(BF16)
        lo = (h3 - hi.astype(F32)).astype(BF16)
        hw = jnp.dot(hi, wr_ref[...], preferred_element_type=F32)
        return (hw[:, :LANES] + hw[:, LANES:]
                + jnp.dot(lo, wr_ref[:, :LANES], preferred_element_type=F32) + br_ref[...])

    nr = tm // MID_CHAINS
    logits = jnp.concatenate([token_chain(c * nr, nr) for c in range(MID_CHAINS)], axis=0)
    lt = logits.T[:N_EXPERTS, :]
    eidx = lax.broadcasted_iota(I32, lt.shape, 0)
    vals, idxs, sels = [], [], []
    for _ in range(TOP_K):
        m = jnp.max(lt, axis=0, keepdims=True)
        idx = jnp.min(jnp.where(lt == m, eidx, N_EXPERTS), axis=0, keepdims=True)
        sel = eidx == idx
        lt = jnp.where(sel, -jnp.inf, lt)
        vals.append(m)
        idxs.append(idx)
        sels.append(sel)
    ex = [jnp.exp(vk - vals[0]) for vk in vals]
    inv = 1.0 / (ex[0] + ex[1] + ex[2] + ex[3])
    zrow = jnp.zeros_like(ex[0])
    gate_ref[...] = jnp.concatenate([e_ * inv for e_ in ex] + [zrow] * (SUBLANES - TOP_K), axis=0)

    onehot = jnp.zeros(lt.shape, F32)
    for sel in sels:
        onehot = onehot + sel.astype(F32)
    before = cnt_sc[:, 0:1] + jnp.dot(onehot.astype(BF16), tri_ref[...], preferred_element_type=F32)
    ranks = [jnp.sum(jnp.where(sel, before, 0.0), axis=0, keepdims=True).astype(I32) for sel in sels]
    meta_ref[...] = jnp.concatenate(idxs + ranks, axis=0)
    cnt_sc[...] = cnt_sc[...] + jnp.sum(onehot, axis=1, keepdims=True)
    cnt_ref[...] = cnt_sc[...]

    posv[...] = jnp.concatenate([ix * cap + rk for ix, rk in zip(idxs, ranks)], axis=0)
    to_smem = pltpu.make_async_copy(posv, pos_sm.at[step % 2], psem)
    to_smem.start()
    to_smem.wait()

    @pl.when(step == nsteps - 1)
    def _():
        @pl.when(step >= 1)
        def _():
            wait_scatter((step + 1) % 2)

        def issue(tb, carry):
            for u in range(ISSUE_UNROLL):
                t = tb * ISSUE_UNROLL + u
                for k in range(TOP_K):
                    r0 = pl.multiple_of(pos_sm[step % 2, k, t] * SUBLANES, SUBLANES)
                    _row_copy(stage.at[cur_stage, pl.ds(pl.multiple_of(t * SUBLANES, SUBLANES), SUBLANES), :],
                              slots_ref.at[pl.ds(r0, SUBLANES), :], rsem.at[(step + 1) % 2]).start(priority=k % 2)
            return carry

        lax.fori_loop(0, tm // ISSUE_UNROLL, issue, 0)
        wait_scatter(step % 2)
        wait_scatter((step + 1) % 2)

        cntv[...] = cnt_sc[...].astype(I32)
        counts_to_smem = pltpu.make_async_copy(cntv, cnt_sm, psem)
        counts_to_smem.start()
        counts_to_smem.wait()
        zbuf[...] = jnp.zeros_like(zbuf)
        zrows = zbuf.shape[0] // SUBLANES
        bits = [1 << j for j in range(zrows.bit_length() - 1, -1, -1)]
        assert sum(bits) == 2 * zrows - 1 and zrows * 2 == te

        def fill(start):
            for e in range(N_EXPERTS):
                cnt = cnt_sm[e, 0]
                pad = (te - cnt % te) % te
                row = e * cap + cnt
                for bit in bits:
                    @pl.when((pad & bit) != 0)
                    def _(row=row, bit=bit):
                        cp = _row_copy(zbuf.at[pl.ds(0, bit * SUBLANES), :],
                                       slots_ref.at[pl.ds(pl.multiple_of(row * SUBLANES, SUBLANES),
                                                          bit * SUBLANES), :], zsem)
                        cp.start() if start else cp.wait()
                    row = row + (pad & bit)

        fill(True)
        fill(False)


def _mid(x_p, x_s, a, f, kvc, g_attn_out, g_four_out, w_out, g_cross, w_q, w_o, g_moe, w_router, b_router,
         cap, te):
    n_prompt, seq, d = x_p.shape
    nb = a.shape[0]
    tm = min(TM_MID, seq)
    nt = seq // tm
    n_mem = kvc.shape[1]
    pb, sb = _two_group_maps(n_prompt)
    wr = jnp.zeros((d, LANES), F32).at[:, :N_EXPERTS].set(w_router)
    wr_hi = wr.astype(BF16)
    wr_lo = (wr - wr_hi.astype(F32)).astype(BF16)
    br = jnp.zeros((1, LANES), F32).at[0, :N_EXPERTS].set(b_router)
    tri = jnp.asarray(np.triu(np.ones((tm, tm), np.float32), 1), BF16)
    full = lambda arr: pl.BlockSpec(arr.shape, lambda b, i: (0,) * arr.ndim)
    row = lambda v: v.reshape(1, -1)
    args = [x_p, x_s, a, f, kvc, row(g_attn_out), row(g_four_out), w_out.astype(BF16), row(g_cross),
            w_q.astype(BF16), w_o.astype(BF16), row(g_moe), jnp.concatenate([wr_hi, wr_lo], axis=1), br, tri]
    in_specs = [
        pl.BlockSpec((1, tm, d), lambda b, i: (pb(b), i, 0)),
        pl.BlockSpec((1, tm, d), lambda b, i: (sb(b), i, 0)),
        pl.BlockSpec((1, tm, ATTN_WIDTH), lambda b, i: (b, i, 0)),
        pl.BlockSpec((1, tm, FOURIER_WIDTH), lambda b, i: (b, i, 0)),
        pl.BlockSpec((1, n_mem, 2 * CROSS_WIDTH), lambda b, i: (b, 0, 0)),
    ] + [full(v) for v in args[5:]]
    tok = nb * seq
    slot_rows = N_EXPERTS * cap + TOP_K * tm
    return pl.pallas_call(
        functools.partial(_mid_kernel, n_prompt=n_prompt, cap=cap, te=te),
        out_shape=(jax.ShapeDtypeStruct((nb, seq, d), F32),
                   jax.ShapeDtypeStruct((SUBLANES, tok), I32),
                   jax.ShapeDtypeStruct((SUBLANES, tok), F32),
                   jax.ShapeDtypeStruct((N_EXPERTS, LANES), F32),
                   jax.ShapeDtypeStruct((slot_rows * SUBLANES, LANES), F32)),
        grid=(nb, nt),
        in_specs=in_specs,
        out_specs=(pl.BlockSpec((1, tm, d), lambda b, i: (b, i, 0)),
                   pl.BlockSpec((SUBLANES, tm), lambda b, i: (0, b * nt + i)),
                   pl.BlockSpec((SUBLANES, tm), lambda b, i: (0, b * nt + i)),
                   pl.BlockSpec((N_EXPERTS, LANES), lambda b, i: (0, 0)),
                   pl.BlockSpec(memory_space=pl.ANY)),
        scratch_shapes=[pltpu.VMEM((N_EXPERTS, LANES), F32),
                        pltpu.VMEM((3, tm * SUBLANES, LANES), F32),
                        pltpu.VMEM((TOP_K, tm), I32), pltpu.SMEM((2, TOP_K, tm), I32),
                        pltpu.VMEM((N_EXPERTS, LANES), I32), pltpu.SMEM((N_EXPERTS, LANES), I32),
                        pltpu.VMEM((te // 2 * SUBLANES, LANES), F32),
                        pltpu.SemaphoreType.DMA((2,)), pltpu.SemaphoreType.DMA, pltpu.SemaphoreType.DMA],
        compiler_params=_cparams(("arbitrary", "arbitrary")),
        name="mid",
    )(*args)


def _expert_kernel(te_ref, ti_ref, to_ref, nu_ref, xs_ref, wgu_ref, bgu_ref, wd_ref, bd_ref, ys_ref, wgu_bf, wd_bf):
    i = pl.program_id(0)
    te = xs_ref.shape[0] // SUBLANES
    dff = wd_ref.shape[1]

    @pl.when((i == 0) | (te_ref[i] != te_ref[jnp.maximum(i - 1, 0)]))
    def _():
        wgu_bf[...] = wgu_ref[0].astype(BF16)
        wd_bf[...] = wd_ref[0].astype(BF16)

    @pl.when(i < nu_ref[0])
    def _():
        x = jnp.concatenate([xs_ref[pl.ds(sl, te, stride=SUBLANES), :] for sl in range(SUBLANES)],
                            axis=1).astype(BF16)
        gu = jnp.dot(x, wgu_bf[...], preferred_element_type=F32) + bgu_ref[0]
        gate = jnp.minimum(gu[:, :dff], SWIGLU_LIMIT)
        up = jnp.clip(gu[:, dff:], -SWIGLU_LIMIT, SWIGLU_LIMIT)
        act = (up + 1.0) * (gate * (1.0 / (1.0 + jnp.exp(-SWIGLU_ALPHA * gate))))
        y = jnp.dot(act.astype(BF16), wd_bf[...], preferred_element_type=F32) + bd_ref[0]
        for sl in range(SUBLANES):
            ys_ref[pl.ds(sl, te, stride=SUBLANES), :] = y[:, sl * LANES:(sl + 1) * LANES]


def _experts(xs, tile_expert, tile_in, tile_out, n_used, w_gate_up, b_gate_up, w_down, b_down, te):
    ne, d, two_dff = w_gate_up.shape
    dff = two_dff // 2
    ntile = tile_expert.shape[0]
    return pl.pallas_call(
        _expert_kernel,
        out_shape=jax.ShapeDtypeStruct(((ntile * te + SEG_ROWS) * SUBLANES, LANES), F32),
        grid_spec=pltpu.PrefetchScalarGridSpec(
            num_scalar_prefetch=4,
            grid=(ntile,),
            in_specs=[
                pl.BlockSpec((te * SUBLANES, LANES), lambda i, te_r, ti_r, to_r, nu_r: (ti_r[i], 0)),
                pl.BlockSpec((1, d, two_dff), lambda i, te_r, ti_r, to_r, nu_r: (te_r[i], 0, 0)),
                pl.BlockSpec((1, 1, two_dff), lambda i, te_r, ti_r, to_r, nu_r: (te_r[i], 0, 0)),
                pl.BlockSpec((1, dff, d), lambda i, te_r, ti_r, to_r, nu_r: (te_r[i], 0, 0)),
                pl.BlockSpec((1, 1, d), lambda i, te_r, ti_r, to_r, nu_r: (te_r[i], 0, 0)),
            ],
            out_specs=pl.BlockSpec((te * SUBLANES, LANES), lambda i, te_r, ti_r, to_r, nu_r: (to_r[i], 0)),
            scratch_shapes=[pltpu.VMEM((d, two_dff), BF16), pltpu.VMEM((dff, d), BF16)],
        ),
        compiler_params=_cparams(("arbitrary",)),
        name="experts",
    )(tile_expert, tile_in, tile_out, n_used, xs, w_gate_up, b_gate_up.reshape(ne, 1, two_dff),
      w_down, b_down.reshape(ne, 1, d))


SEG_ROWS = 32


def _combine_kernel(seg_ref, segn_ref, lidx_ref, gate_ref, x2_ref, g_ref, ys_ref, o_ref, buf, obuf, sem, *, tm):
    step = pl.program_id(0) * pl.num_programs(1) + pl.program_id(1)
    nsteps = pl.num_programs(0) * pl.num_programs(1)
    slot = step % 2
    seg_tile = SEG_ROWS * SUBLANES

    def segments(seg, to_slot, start):
        def per_segment(j, carry):
            src = pl.multiple_of(seg[0, 0, 1 + j] * SUBLANES, SUBLANES)
            dst = pl.multiple_of(j * seg_tile, seg_tile)
            cp = _row_copy(ys_ref.at[pl.ds(src, seg_tile), :], buf.at[to_slot, pl.ds(dst, seg_tile), :],
                           sem.at[to_slot])
            if start:
                cp.start()
            else:
                cp.wait()
            return carry

        lax.fori_loop(0, seg[0, 0, 0], per_segment, 0)

    @pl.when(step == 0)
    def _():
        segments(seg_ref, 0, True)

    @pl.when(step + 1 < nsteps)
    def _():
        segments(segn_ref, 1 - slot, True)

    segments(seg_ref, slot, False)

    def sum_rows(static_slot):
        for t in range(tm):
            acc = None
            for k in range(TOP_K):
                p = pl.multiple_of(lidx_ref[0, 0, k * tm + t] * SUBLANES, SUBLANES)
                term = gate_ref[0, 0, k * tm + t] * buf[static_slot, pl.ds(p, SUBLANES), :]
                acc = term if acc is None else acc + term
            obuf[pl.ds(t * SUBLANES, SUBLANES), :] = acc
        y = jnp.concatenate([obuf[pl.ds(sl, tm, stride=SUBLANES), :] for sl in range(SUBLANES)], axis=1)
        o_ref[0] = _rms(x2_ref[0] + y, g_ref[...])

    for static_slot in range(2):
        pl.when(slot == static_slot)(functools.partial(sum_rows, static_slot))


def _combine(x2, seg, lidx3, gate3, ys, g_final, b0, nbatch, tm):
    _, seq, d = x2.shape
    nt = seq // tm
    last = (b0 + nbatch) * nt - 1
    tile = lambda b, i: (b0 + b) * nt + i
    buf_rows = -(-(TOP_K * tm + N_EXPERTS * (SEG_ROWS - 1)) // SEG_ROWS) * SEG_ROWS
    smem = lambda width, fn: pl.BlockSpec((1, 1, width), lambda b, i: (fn(b, i), 0, 0), memory_space=pltpu.SMEM)
    return pl.pallas_call(
        functools.partial(_combine_kernel, tm=tm),
        out_shape=jax.ShapeDtypeStruct((nbatch, seq, d), F32),
        grid_spec=pltpu.PrefetchScalarGridSpec(
            num_scalar_prefetch=0,
            grid=(nbatch, nt),
            in_specs=[
                smem(seg.shape[2], tile),
                smem(seg.shape[2], lambda b, i: jnp.minimum(tile(b, i) + 1, last)),
                smem(TOP_K * tm, tile),
                smem(TOP_K * tm, tile),
                pl.BlockSpec((1, tm, d), lambda b, i: (b0 + b, i, 0)),
                pl.BlockSpec((1, d), lambda b, i: (0, 0)),
                pl.BlockSpec(memory_space=pl.ANY),
            ],
            out_specs=pl.BlockSpec((1, tm, d), lambda b, i: (b, i, 0)),
            scratch_shapes=[pltpu.VMEM((2, buf_rows * SUBLANES, LANES), F32),
                            pltpu.VMEM((tm * SUBLANES, LANES), F32),
                            pltpu.SemaphoreType.DMA((2,))],
        ),
        compiler_params=_cparams(("arbitrary", "arbitrary")),
        name="combine",
    )(seg, seg, lidx3, gate3, x2, g_final.reshape(1, d), ys)


def _routing_plan(meta, gates, cnt, te, tm, cap):
    tok = meta.shape[1]
    idx, rank = meta[:TOP_K], meta[TOP_K:]
    counts = cnt[:, 0].astype(I32)
    tiles = (counts + te - 1) // te
    tile_end = jnp.cumsum(tiles)
    tile_start = tile_end - tiles
    offsets = tile_start * te
    n_used = tile_end[-1]
    ntile_max = (tok * TOP_K) // te + N_EXPERTS
    t_ids = jnp.minimum(jnp.arange(ntile_max, dtype=I32), n_used - 1)
    tile_expert = jnp.sum((t_ids[:, None] >= tile_end[None, :]).astype(I32), axis=1)
    first = jnp.sum(jnp.where(tile_expert[:, None] == jnp.arange(N_EXPERTS, dtype=I32)[None, :],
                              tile_start[None, :], 0), axis=1)
    tile_in = tile_expert * (cap // te) + (t_ids - first)

    ng = tok // tm
    onehot = idx.reshape(TOP_K, ng, tm)[..., None] == jnp.arange(N_EXPERTS, dtype=I32)
    n_ge = jnp.sum(onehot.astype(I32), axis=(0, 2))
    before = jnp.cumsum(n_ge, axis=0) - n_ge
    nseg = (n_ge + SEG_ROWS - 1) // SEG_ROWS
    seg_end = jnp.cumsum(nseg, axis=1)
    dst = (seg_end - nseg) * SEG_ROWS
    max_seg = (TOP_K * tm) // SEG_ROWS + N_EXPERTS
    assert max_seg < LANES
    j = jnp.arange(max_seg, dtype=I32)[None, :, None]
    owner = j >= seg_end[:, None, :]
    e_j = jnp.minimum(jnp.sum(owner.astype(I32), axis=-1), N_EXPERTS - 1)
    pick = e_j[..., None] == jnp.arange(N_EXPERTS, dtype=I32)
    first_row = (offsets[None, :] + before - dst)[:, None, :]
    src = jnp.sum(jnp.where(pick, first_row, 0), axis=-1) + j[..., 0] * SEG_ROWS
    seg = jnp.concatenate([seg_end[:, -1:], src, jnp.zeros((ng, LANES - 1 - max_seg), I32)],
                          axis=1).reshape(ng, 1, LANES)
    shift = (dst - before)[None, :, None, :]
    lidx = rank.reshape(TOP_K, ng, tm) + jnp.sum(jnp.where(onehot, shift, 0), axis=-1)
    by_tile = lambda v: v.reshape(TOP_K, ng, tm).transpose(1, 0, 2).reshape(ng, 1, TOP_K * tm)
    return tile_expert, tile_in, t_ids, n_used.reshape(1), seg, by_tile(lidx), by_tile(gates[:TOP_K])


def kernel(x_prompt, x_sample, mem_prompt, mem_sample, g_mix, w_in, sinks, g_attn_out, g_four_out, w_out,
           g_cross, g_mem, w_q_cross, w_kv_cross, w_o_cross, g_moe, w_router, b_router, w_gate_up,
           b_gate_up, w_down, b_down, g_final):
    n_prompt, seq, _ = x_prompt.shape
    n_sample = x_sample.shape[0]
    l = 0
    kvc = _memkv(mem_prompt, mem_sample, g_mem[l], w_kv_cross[l])
    q, kv, ab = _inproj(x_prompt, x_sample, g_mix[l], w_in[l])
    a = _swa(q, kv, sinks[l])
    f = _seqdft(ab)
    tok = (n_prompt + n_sample) * seq
    te = TM_EXPERT
    cap = -(-tok // te) * te
    x2, meta, gates, cnt, xs = _mid(x_prompt, x_sample, a, f, kvc, g_attn_out[l], g_four_out[l], w_out[l],
                                    g_cross[l], w_q_cross[l], w_o_cross[l], g_moe[l], w_router[l], b_router[l],
                                    cap, te)
    tm_rows = min(TM_ROWS, seq)
    tile_expert, tile_in, tile_out, n_used, seg, lidx3, gate3 = _routing_plan(meta, gates, cnt, te, tm_rows, cap)
    ys = _experts(xs, tile_expert, tile_in, tile_out, n_used, w_gate_up[l], b_gate_up[l], w_down[l], b_down[l], te)
    y_p = _combine(x2, seg, lidx3, gate3, ys, g_final, 0, n_prompt, tm_rows)
    y_s = _combine(x2, seg, lidx3, gate3, ys, g_final, n_prompt, n_sample, tm_rows)
    return (y_p, y_s)
```
